```python
import jax, jax.numpy as jnp
from jax import lax
import numpy as np

D_MODEL = 2048
BATCH = 4
SEQ = 2048
DEPTH = 1
DEC_BATCH = 128
DEC_SEQ = 1
PAST_LEN = 16384
PAGE_SIZE = 128

A_W = 1024
A_GROUPS = 8
A_GROUP_W = A_W // A_GROUPS
CHUNK = 128
B_HEADS = 32
HEAD_SIZE = 64
B_W = B_HEADS * HEAD_SIZE
DECAY_LORA = 96
AAA_LORA = 96
SHIFT_W = 3 * B_W + DECAY_LORA + AAA_LORA
IN_SIZES = (A_W, A_W, A_W, SHIFT_W, B_W, D_MODEL, D_MODEL)
SHIFT_SIZES = (B_W, DECAY_LORA, B_W, B_W, AAA_LORA)
IN_W = 3 * A_W + SHIFT_W + B_W + 2 * D_MODEL
NORM_EPS = 1e-6
GN_EPS = HEAD_SIZE * 1e-5

kernel_name = "hybrid_gmlp_rwkv7_gated_step"


def _split(t, sizes):
    out, o = [], 0
    for s in sizes:
        out.append(t[..., o:o + s])
        o += s
    return out


def _rmsnorm(x, g):
    xf = x.astype(jnp.float32)
    y = xf * lax.rsqrt(jnp.mean(xf * xf, axis=-1, keepdims=True) + NORM_EPS)
    return (y * g.astype(jnp.float32)).astype(x.dtype)


def _layernorm(x, g, b):
    xf = x.astype(jnp.float32)
    mu = jnp.mean(xf, axis=-1, keepdims=True)
    var = jnp.mean(jnp.square(xf - mu), axis=-1, keepdims=True)
    y = (xf - mu) * lax.rsqrt(var + NORM_EPS)
    return (y * g.astype(jnp.float32) + b.astype(jnp.float32)).astype(x.dtype)


def _chunk_mix(v, w_s, b_s):
    bsz, T, _ = v.shape
    n_chunks = -(-T // CHUNK)
    tp = n_chunks * CHUNK
    vp = jnp.pad(v, ((0, 0), (0, tp - T), (0, 0)))
    vp = vp.reshape(bsz, n_chunks, CHUNK, A_GROUPS, A_GROUP_W)
    causal = jnp.tril(jnp.ones((CHUNK, CHUNK), dtype=bool))
    wm = jnp.where(causal[None], w_s, 0)
    out = jnp.einsum('gts,bnsgc->bntgc', wm, vp) + b_s.T[None, None, :, :, None]
    return out.reshape(bsz, tp, A_W)[:, :T]


def _wkv_scan(s0, r, w, k, v, a, b):
    def step(s, inp):
        r_t, w_t, k_t, v_t, a_t, b_t = inp
        sa = jnp.einsum('bhij,bhj->bhi', s, a_t)
        s = s * w_t[:, :, None, :] + sa[..., None] * b_t[:, :, None, :] + v_t[..., None] * k_t[:, :, None, :]
        y_t = jnp.einsum('bhij,bhj->bhi', s, r_t)
        return s, y_t
    xs = tuple(jnp.swapaxes(t.astype(jnp.float32), 0, 1) for t in (r, w, k, v, a, b))
    s, ys = lax.scan(step, s0.astype(jnp.float32), xs)
    return s, jnp.swapaxes(ys, 0, 1)


def _layer(x, c, wkv0, shift0, norm_g, w_c, b_c, w_in, ln_v_g, ln_v_b, w_s, b_s,
           mu_shift, w0, w2, a0, a2, k_k, k_a, r_k, gn_g, gn_b, p_a, p_b, w_out):
    bsz, T, _ = x.shape
    f32 = jnp.float32
    mod = c @ w_c + b_c
    c_shift, c_scale, c_gate = jnp.split(mod, 3, axis=-1)
    h = _rmsnorm(x, norm_g) * (1 + c_scale[:, None]) + c_shift[:, None]
    proj = jnp.einsum('btd,de->bte', h, w_in)
    u_a, v_a, z_a, p_rwkv, z_b, g_a, g_b = _split(proj, IN_SIZES)

    u = jax.nn.gelu(u_a)
    v = _layernorm(jax.nn.gelu(v_a), ln_v_g, ln_v_b)
    y_a = u * _chunk_mix(v, w_s, b_s) * jax.nn.silu(z_a)

    prev = jnp.concatenate([shift0[:, None].astype(p_rwkv.dtype), p_rwkv[:, :-1]], axis=1)
    ps = p_rwkv + mu_shift * (prev - p_rwkv)
    r, wd, k, vv, ad = _split(ps, SHIFT_SIZES)
    w_log = -jax.nn.softplus(-(w0 + jnp.tanh(wd) @ w2).astype(f32)) - 0.5
    decay = jnp.exp(-jnp.exp(w_log))
    a = jax.nn.sigmoid((a0 + ad @ a2).astype(f32))
    kk = (k * k_k).astype(f32).reshape(bsz, T, B_HEADS, HEAD_SIZE)
    kk = kk / jnp.maximum(jnp.linalg.norm(kk, axis=-1, keepdims=True), 1e-12)
    k = k.astype(f32) * (1 + (a - 1) * k_a.astype(f32))
    hd = lambda t: t.astype(f32).reshape(bsz, T, B_HEADS, HEAD_SIZE)
    r_h, w_h, k_h, v_h, a_h = hd(r), hd(decay), hd(k), hd(vv), hd(a)
    s_new, y = _wkv_scan(wkv0, r_h, w_h, k_h, v_h, -kk, kk * a_h)
    mu = jnp.mean(y, axis=-1, keepdims=True)
    var = jnp.mean(jnp.square(y - mu), axis=-1, keepdims=True)
    y = (y - mu) * lax.rsqrt(var + GN_EPS)
    y = y * gn_g.astype(f32).reshape(B_HEADS, HEAD_SIZE) + gn_b.astype(f32).reshape(B_HEADS, HEAD_SIZE)
    bonus = jnp.sum(r_h * k_h * r_k.astype(f32), axis=-1, keepdims=True) * v_h
    y_b = (y + bonus).reshape(bsz, T, B_W).astype(x.dtype) * jax.nn.silu(z_b)

    merged = jax.nn.sigmoid(g_a) * (y_a @ p_a) + jax.nn.sigmoid(g_b) * (y_b @ p_b)
    out = x + c_gate[:, None] * (merged @ w_out)
    return out, s_new, p_rwkv[:, -1], v


def setup_inputs(seed: int = 0) -> dict:
    key = jax.random.key(seed)
    ks = jax.random.split(key, 40)
    n = lambda i, shape: jax.random.normal(ks[i], shape, jnp.float32)
    L = DEPTH
    return {
        "x_prompt": n(0, (BATCH, SEQ, D_MODEL)),
        "x_sample": n(1, (DEC_BATCH, DEC_SEQ, D_MODEL)),
        "c_prompt": n(2, (BATCH, D_MODEL)),
        "c_sample": n(3, (DEC_BATCH, D_MODEL)),
        "state_wkv": 0.1 * n(4, (L, DEC_BATCH, B_HEADS, HEAD_SIZE, HEAD_SIZE)),
        "state_shift": n(5, (L, DEC_BATCH, SHIFT_W)),
        "norm_g": 1.0 + 0.02 * n(6, (L, D_MODEL)),
        "w_c": 0.5 * D_MODEL ** -0.5 * n(7, (L, D_MODEL, 3 * D_MODEL)),
        "b_c": 0.02 * n(8, (L, 3 * D_MODEL)),
        "w_in": D_MODEL ** -0.5 * n(9, (L, D_MODEL, IN_W)),
        "ln_v_g": 1.0 + 0.02 * n(10, (L, A_W)),
        "ln_v_b": 0.02 * n(11, (L, A_W)),
        "w_s": CHUNK ** -0.5 * n(12, (L, A_GROUPS, CHUNK, CHUNK)),
        "b_s": 1.0 + 0.1 * n(13, (L, A_GROUPS, CHUNK)),
        "mu_shift": jax.random.uniform(ks[14], (L, SHIFT_W), jnp.float32),
        "w0": jax.random.uniform(ks[15], (L, B_W), jnp.float32, -4.0, 1.0),
        "w2": 0.1 * DECAY_LORA ** -0.5 * n(16, (L, DECAY_LORA, B_W)),
        "a0": 0.1 * n(17, (L, B_W)),
        "a2": 0.1 * AAA_LORA ** -0.5 * n(18, (L, AAA_LORA, B_W)),
        "k_k": 0.85 + 0.05 * n(19, (L, B_W)),
        "k_a": 1.0 + 0.05 * n(20, (L, B_W)),
        "r_k": 0.1 * n(21, (L, B_HEADS, HEAD_SIZE)),
        "gn_g": 1.0 + 0.02 * n(22, (L, B_W)),
        "gn_b": 0.02 * n(23, (L, B_W)),
        "p_a": A_W ** -0.5 * n(24, (L, A_W, D_MODEL)),
        "p_b": B_W ** -0.5 * n(25, (L, B_W, D_MODEL)),
        "w_out": D_MODEL ** -0.5 * n(26, (L, D_MODEL, D_MODEL)),
        "final_g": 1.0 + 0.02 * n(27, (D_MODEL,)),
    }


def reference(x_prompt, x_sample, c_prompt, c_sample, state_wkv, state_shift,
              norm_g, w_c, b_c, w_in, ln_v_g, ln_v_b, w_s, b_s,
              mu_shift, w0, w2, a0, a2, k_k, k_a, r_k, gn_g, gn_b,
              p_a, p_b, w_out, final_g):
    bp = x_prompt.shape[0]
    wkv_zero = jnp.zeros((bp, B_HEADS, HEAD_SIZE, HEAD_SIZE), jnp.float32)
    shift_zero = jnp.zeros((bp, SHIFT_W), x_prompt.dtype)
    hp, hs = x_prompt, x_sample
    wkv_p, sh_p, wkv_s, sh_s, cv_s = [], [], [], [], []
    for l in range(DEPTH):
        params = (norm_g[l], w_c[l], b_c[l], w_in[l], ln_v_g[l], ln_v_b[l], w_s[l], b_s[l],
                  mu_shift[l], w0[l], w2[l], a0[l], a2[l], k_k[l], k_a[l], r_k[l],
                  gn_g[l], gn_b[l], p_a[l], p_b[l], w_out[l])
        hp, s_p, shp, _ = _layer(hp, c_prompt, wkv_zero, shift_zero, *params)
        hs, s_s, shs, v_s = _layer(hs, c_sample, state_wkv[l], state_shift[l], *params)
        wkv_p.append(s_p); sh_p.append(shp)
        wkv_s.append(s_s); sh_s.append(shs); cv_s.append(v_s)
    y_prompt = _rmsnorm(hp, final_g)
    y_sample = _rmsnorm(hs, final_g)
    wkv_prompt = jnp.stack(wkv_p)
    shift_prompt = jnp.stack(sh_p)
    wkv_sample = jnp.stack(wkv_s)
    shift_sample = jnp.stack(sh_s)
    chunk_v_sample = jnp.stack(cv_s)
    return (y_prompt, y_sample, wkv_prompt, shift_prompt, wkv_sample, shift_sample, chunk_v_sample)
```

```python
import functools

import jax
import jax.numpy as jnp
from jax import lax
from jax.experimental import pallas as pl
from jax.experimental.pallas import tpu as pltpu

F32 = jnp.float32
BF16 = jnp.bfloat16

D_MODEL = 2048
A_W = 1024
A_GROUPS = 8
CHUNK = 128
B_HEADS = 32
HEAD_SIZE = 64
B_W = B_HEADS * HEAD_SIZE
LORA = 96
NORM_EPS = 1e-6
GN_EPS = HEAD_SIZE * 1e-5

LORA_W = 512
PROJ_W = 6 * B_W + 3 * A_W + LORA_W
COL_R, COL_K, COL_V, COL_ZB, COL_GA, COL_GB = 0, 1, 2, 3, 4, 5
COL_UA, COL_VA, COL_ZA = 12, 13, 14
COL_LORA = (6 * B_W + 3 * A_W) // LORA_W

V7X_LANES = 128
V7X_SUBLANES = 8
VMEM_LIMIT = 56 * 1024 * 1024


def _cparams(*sem):
    return pltpu.CompilerParams(dimension_semantics=sem, vmem_limit_bytes=VMEM_LIMIT)


def _mod_kernel(c_ref, w_ref, b_ref, o_ref):
    acc = jnp.dot(c_ref[...].astype(BF16), w_ref[...].astype(BF16), preferred_element_type=F32)
    o_ref[...] = acc + b_ref[...]


def _modulation(c_all, w_c, b_c):
    m, d = c_all.shape
    n = w_c.shape[1]
    tn = 768
    return pl.pallas_call(
        _mod_kernel,
        grid=(n // tn,),
        in_specs=[pl.BlockSpec((m, d), lambda j: (0, 0)),
                  pl.BlockSpec((d, tn), lambda j: (0, j)),
                  pl.BlockSpec((1, tn), lambda j: (0, j))],
        out_specs=pl.BlockSpec((m, tn), lambda j: (0, j)),
        out_shape=jax.ShapeDtypeStruct((m, n), F32),
        compiler_params=_cparams("arbitrary"),
        name="modulation",
    )(c_all, w_c, b_c.reshape(1, n))


def _normmod_kernel(x_ref, g_ref, sc_ref, sh_ref, o_ref):
    x = x_ref[0]
    ms = jnp.mean(x * x, axis=-1, keepdims=True)
    y = x * lax.rsqrt(ms + NORM_EPS) * g_ref[...]
    o_ref[0] = (y * (1.0 + sc_ref[0]) + sh_ref[0]).astype(BF16)


def _norm_modulate(x, g, scale, shift, tm):
    bsz, t, d = x.shape
    ts = scale.shape[1]
    sspec = (pl.BlockSpec((1, 1, d), lambda b, i: (b, 0, 0)) if ts == 1
             else pl.BlockSpec((1, tm, d), lambda b, i: (b, i, 0)))
    return pl.pallas_call(
        _normmod_kernel,
        grid=(bsz, t // tm),
        in_specs=[pl.BlockSpec((1, tm, d), lambda b, i: (b, i, 0)),
                  pl.BlockSpec((1, d), lambda b, i: (0, 0)),
                  sspec, sspec],
        out_specs=pl.BlockSpec((1, tm, d), lambda b, i: (b, i, 0)),
        out_shape=jax.ShapeDtypeStruct((bsz, t, d), BF16),
        compiler_params=_cparams("arbitrary", "arbitrary"),
        name="norm_modulate",
    )(x, g.reshape(1, d), scale, shift)


def _inproj_kernel(h_ref, w_ref, o_ref):
    o_ref[0] = jnp.dot(h_ref[0], w_ref[...], preferred_element_type=F32)


def _in_projection(h, w_p, tn=512):
    bsz, t, d = h.shape
    n = w_p.shape[1]
    return pl.pallas_call(
        _inproj_kernel,
        grid=(bsz, n // tn),
        in_specs=[pl.BlockSpec((1, t, d), lambda b, j: (b, 0, 0)),
                  pl.BlockSpec((d, tn), lambda b, j: (0, j))],
        out_specs=pl.BlockSpec((1, t, tn), lambda b, j: (b, 0, j)),
        out_shape=jax.ShapeDtypeStruct((bsz, t, n), F32),
        compiler_params=_cparams("arbitrary", "arbitrary"),
        name="in_projection",
    )(h, w_p)


def _prep_core(p, q, mu, w0, a0, w2p, a2p):
    r, k, v, lo = (pi + mi * (qi - pi) for pi, qi, mi in zip(p, q, mu))
    col = lax.broadcasted_iota(jnp.int32, lo.shape, 1)
    lt = jnp.where(col < LORA, jnp.tanh(lo), lo).astype(BF16)
    wl = w0 + jnp.dot(lt, w2p, preferred_element_type=F32)
    al = a0 + jnp.dot(lt, a2p, preferred_element_type=F32)
    w_log = -jax.nn.softplus(-wl) - 0.5
    decay = jnp.exp(-jnp.exp(w_log))
    a = jax.nn.sigmoid(al)
    return r, decay, k, v, a


def _prep_seq_kernel(pr, pk, pv, plo, tr, tk, tv, tlo, sr, sk, sv, slo,
                     mr, mk, mv, mlo, w0, a0, w2p, a2p,
                     o_r, o_w, o_k, o_v, o_a):
    i = pl.program_id(1)

    def prev(p_ref, tail_ref, s_ref):
        p = p_ref[0]
        last = tail_ref[0][V7X_SUBLANES - 1:V7X_SUBLANES, :]
        first = jnp.where(i == 0, s_ref[0], last)
        rolled = pltpu.roll(p, 1, axis=0)
        row = lax.broadcasted_iota(jnp.int32, p.shape, 0)
        return p, jnp.where(row == 0, first, rolled)

    pq = [prev(a, b, c) for a, b, c in ((pr, tr, sr), (pk, tk, sk), (pv, tv, sv), (plo, tlo, slo))]
    outs = _prep_core([x[0] for x in pq], [x[1] for x in pq],
                      (mr[...], mk[...], mv[...], mlo[...]), w0[...], a0[...], w2p[...], a2p[...])
    for o, val in zip((o_r, o_w, o_k, o_v, o_a), outs):
        o[0] = val


def _prep_tok_kernel(pr, pk, pv, plo, sr, sk, sv, slo,
                     mr, mk, mv, mlo, w0, a0, w2p, a2p,
                     o_r, o_w, o_k, o_v, o_a):
    outs = _prep_core((pr[0], pk[0], pv[0], plo[0]), (sr[...], sk[...], sv[...], slo[...]),
                      (mr[...], mk[...], mv[...], mlo[...]), w0[...], a0[...], w2p[...], a2p[...])
    for o, val in zip((o_r, o_w, o_k, o_v, o_a), outs):
        o[0] = val


def _rwkv_prep_seq(proj, shift0, mu, w0, a0, w2p, a2p, tm=128):
    bsz, t, _ = proj.shape
    tail = tm // V7X_SUBLANES
    cols = ((B_W, COL_R), (B_W, COL_K), (B_W, COL_V), (LORA_W, COL_LORA))
    cur = [pl.BlockSpec((1, tm, w), functools.partial(lambda b, i, c: (b, i, c), c=c)) for w, c in cols]
    tails = [pl.BlockSpec((1, V7X_SUBLANES, w),
                          functools.partial(lambda b, i, c: (b, jnp.maximum(i * tail - 1, 0), c), c=c))
             for w, c in cols]
    s0 = [pl.BlockSpec((1, 1, w), lambda b, i: (b, 0, 0)) for w, _ in cols]
    rowv = [pl.BlockSpec((1, w), lambda b, i: (0, 0)) for w, _ in cols]
    full = lambda shape: pl.BlockSpec(shape, lambda b, i: (0,) * len(shape))
    out_spec = pl.BlockSpec((1, tm, B_W), lambda b, i: (b, i, 0))
    return pl.pallas_call(
        _prep_seq_kernel,
        grid=(bsz, t // tm),
        in_specs=cur + tails + s0 + rowv + [full((1, B_W)), full((1, B_W)),
                                            full((LORA_W, B_W)), full((LORA_W, B_W))],
        out_specs=[out_spec] * 5,
        out_shape=[jax.ShapeDtypeStruct((bsz, t, B_W), F32)] * 5,
        compiler_params=_cparams("arbitrary", "arbitrary"),
        name="rwkv_prep_seq",
    )(proj, proj, proj, proj, proj, proj, proj, proj, *shift0, *mu, w0, a0, w2p, a2p)


def _rwkv_prep_tok(proj, shift0, mu, w0, a0, w2p, a2p):
    _, m, _ = proj.shape
    cols = ((B_W, COL_R), (B_W, COL_K), (B_W, COL_V), (LORA_W, COL_LORA))
    cur = [pl.BlockSpec((1, m, w), functools.partial(lambda i, c: (0, 0, c), c=c)) for w, c in cols]
    s0 = [pl.BlockSpec((m, w), lambda i: (0, 0)) for w, _ in cols]
    rowv = [pl.BlockSpec((1, w), lambda i: (0, 0)) for w, _ in cols]
    full = lambda shape: pl.BlockSpec(shape, lambda i: (0,) * len(shape))
    out_spec = pl.BlockSpec((1, m, B_W), lambda i: (0, 0, 0))
    return pl.pallas_call(
        _prep_tok_kernel,
        grid=(1,),
        in_specs=cur + s0 + rowv + [full((1, B_W)), full((1, B_W)),
                                    full((LORA_W, B_W)), full((LORA_W, B_W))],
        out_specs=[out_spec] * 5,
        out_shape=[jax.ShapeDtypeStruct((1, m, B_W), F32)] * 5,
        compiler_params=_cparams("arbitrary"),
        name="rwkv_prep_tok",
    )(proj, proj, proj, proj, *shift0, *mu, w0, a0, w2p, a2p)


def _wkv_kernel(r_ref, w_ref, k_ref, v_ref, a_ref, s0_ref, kk_c, ka_c, rk_c, gg_c, gb_c,
                y_ref, s_ref, nkk_s, b_s, kp_s, ys_s, *, tt):
    step = pl.program_id(1)

    @pl.when(step == 0)
    def _():
        s_ref[...] = s0_ref[...]

    def prep(t, c):
        k = k_ref[t]
        a = a_ref[t]
        kk = k * kk_c[...]
        nrm = jnp.sqrt(jnp.sum(kk * kk, axis=0, keepdims=True))
        kk = kk / jnp.maximum(nrm, 1e-12)
        nkk_s[t] = -kk
        b_s[t] = kk * a
        kp_s[t] = k * (1.0 + (a - 1.0) * ka_c[...])
        return c

    lax.fori_loop(0, tt, prep, 0)

    def bcast(ref, t, j):
        return jnp.broadcast_to(ref[t, pl.ds(j, 1), :], (HEAD_SIZE, V7X_LANES))

    def token(t, c):
        acc = [jnp.zeros((HEAD_SIZE, V7X_LANES), F32) for _ in range(2)]
        for j in range(HEAD_SIZE):
            acc[j % 2] = acc[j % 2] + s_ref[j] * bcast(nkk_s, t, j)
        u = acc[0] + acc[1]
        v = v_ref[t]
        acc = [jnp.zeros((HEAD_SIZE, V7X_LANES), F32) for _ in range(2)]
        for j in range(HEAD_SIZE):
            sj = (s_ref[j] * bcast(w_ref, t, j) + u * bcast(b_s, t, j)) + v * bcast(kp_s, t, j)
            s_ref[j] = sj
            acc[j % 2] = acc[j % 2] + sj * bcast(r_ref, t, j)
        ys_s[t] = acc[0] + acc[1]
        return c

    lax.fori_loop(0, tt, token, 0)

    def post(t, c):
        y = ys_s[t]
        mu = jnp.mean(y, axis=0, keepdims=True)
        yc = y - mu
        var = jnp.mean(yc * yc, axis=0, keepdims=True)
        yn = yc * lax.rsqrt(var + GN_EPS) * gg_c[...] + gb_c[...]
        bonus = jnp.sum(r_ref[t] * kp_s[t] * rk_c[...], axis=0, keepdims=True) * v_ref[t]
        y_ref[t] = yn + bonus
        return c

    lax.fori_loop(0, tt, post, 0)


def _wkv(r, w, k, v, a, s0, consts, tt):
    t, n, lanes = r.shape
    seq = pl.BlockSpec((tt, n, V7X_LANES), lambda l, i: (i, 0, l))
    st = pl.BlockSpec((n, n, V7X_LANES), lambda l, i: (0, 0, l))
    cst = pl.BlockSpec((n, V7X_LANES), lambda l, i: (0, l))
    return pl.pallas_call(
        functools.partial(_wkv_kernel, tt=tt),
        grid=(lanes // V7X_LANES, t // tt),
        in_specs=[seq] * 5 + [st] + [cst] * 5,
        out_specs=[seq, st],
        out_shape=[jax.ShapeDtypeStruct((t, n, lanes), F32),
                   jax.ShapeDtypeStruct((n, n, lanes), F32)],
        scratch_shapes=[pltpu.VMEM((tt, n, V7X_LANES), F32)] * 4,
        compiler_params=_cparams("arbitrary", "arbitrary"),
        name="wkv",
    )(r, w, k, v, a, s0, *consts)


def _to_heads_on_lanes(x):
    bsz, t, _ = x.shape
    x = x.reshape(bsz, t, B_HEADS, HEAD_SIZE)
    return jnp.transpose(x, (1, 3, 0, 2)).reshape(t, HEAD_SIZE, bsz * B_HEADS)


def _from_heads_on_lanes(y, bsz):
    t = y.shape[0]
    y = y.reshape(t, HEAD_SIZE, bsz, B_HEADS)
    return jnp.transpose(y, (2, 0, 3, 1)).reshape(bsz, t, B_W)


def _head_const(x, bsz):
    return jnp.tile(x.reshape(B_HEADS, HEAD_SIZE).T, (1, bsz))


def _gelu(x):
    return jax.nn.gelu(x)


def _layernorm(x, g, b):
    mu = jnp.mean(x, axis=-1, keepdims=True)
    xc = x - mu
    var = jnp.mean(xc * xc, axis=-1, keepdims=True)
    return xc * lax.rsqrt(var + NORM_EPS) * g + b


def _branch_a_seq_kernel(u_ref, v_ref, z_ref, g_ref, lg, lb, ws_ref, bias_ref, pa_ref, o_ref, *, tm):
    u = _gelu(u_ref[0])
    v = _layernorm(_gelu(v_ref[0]), lg[...], lb[...])
    vb = v.astype(BF16)
    row = lax.broadcasted_iota(jnp.int32, (CHUNK, CHUNK), 0)
    col = lax.broadcasted_iota(jnp.int32, (CHUNK, CHUNK), 1)
    rows = []
    for c in range(tm // CHUNK):
        blocks = []
        for g in range(A_GROUPS):
            wm = jnp.where(row >= col, ws_ref[g], 0.0).astype(BF16)
            blk = vb[c * CHUNK:(c + 1) * CHUNK, g * CHUNK:(g + 1) * CHUNK]
            blocks.append(jnp.dot(wm, blk, preferred_element_type=F32))
        rows.append(jnp.concatenate(blocks, axis=1) + bias_ref[...])
    mix = jnp.concatenate(rows, axis=0)
    ya = u * mix * jax.nn.silu(z_ref[0])
    o_ref[0] = jax.nn.sigmoid(g_ref[0]) * jnp.dot(ya.astype(BF16), pa_ref[...],
                                                  preferred_element_type=F32)


def _branch_a_tok_kernel(u_ref, v_ref, z_ref, g_ref, lg, lb, w00, b00, pa_ref, o_ref, vo_ref):
    u = _gelu(u_ref[0])
    v = _layernorm(_gelu(v_ref[0]), lg[...], lb[...])
    vo_ref[...] = v
    mix = v * w00[...] + b00[...]
    ya = u * mix * jax.nn.silu(z_ref[0])
    o_ref[0] = jax.nn.sigmoid(g_ref[0]) * jnp.dot(ya.astype(BF16), pa_ref[...],
                                                  preferred_element_type=F32)


def _branch_a_seq(proj, ln_g, ln_b, w_s, bias2d, p_a, tm=256):
    bsz, t, _ = proj.shape
    full = lambda shape: pl.BlockSpec(shape, lambda b, i: (0,) * len(shape))
    sec = lambda w, c: pl.BlockSpec((1, tm, w), lambda b, i: (b, i, c))
    return pl.pallas_call(
        functools.partial(_branch_a_seq_kernel, tm=tm),
        grid=(bsz, t // tm),
        in_specs=[sec(A_W, COL_UA), sec(A_W, COL_VA), sec(A_W, COL_ZA), sec(D_MODEL, COL_GA),
                  full((1, A_W)), full((1, A_W)), full((A_GROUPS, CHUNK, CHUNK)),
                  full((CHUNK, A_W)), full((A_W, D_MODEL))],
        out_specs=pl.BlockSpec((1, tm, D_MODEL), lambda b, i: (b, i, 0)),
        out_shape=jax.ShapeDtypeStruct((bsz, t, D_MODEL), F32),
        compiler_params=_cparams("arbitrary", "arbitrary"),
        name="branch_a_seq",
    )(proj, proj, proj, proj, ln_g, ln_b, w_s, bias2d, p_a)


def _branch_a_tok(proj, ln_g, ln_b, w00, b00, p_a):
    _, m, _ = proj.shape
    full = lambda shape: pl.BlockSpec(shape, lambda i: (0,) * len(shape))
    sec = lambda w, c: pl.BlockSpec((1, m, w), lambda i: (0, 0, c))
    return pl.pallas_call(
        _branch_a_tok_kernel,
        grid=(1,),
        in_specs=[sec(A_W, COL_UA), sec(A_W, COL_VA), sec(A_W, COL_ZA), sec(D_MODEL, COL_GA),
                  full((1, A_W)), full((1, A_W)), full((1, A_W)), full((1, A_W)),
                  full((A_W, D_MODEL))],
        out_specs=[pl.BlockSpec((1, m, D_MODEL), lambda i: (0, 0, 0)),
                   pl.BlockSpec((m, A_W), lambda i: (0, 0))],
        out_shape=[jax.ShapeDtypeStruct((1, m, D_MODEL), F32),
                   jax.ShapeDtypeStruct((m, A_W), F32)],
        compiler_params=_cparams("arbitrary"),
        name="branch_a_tok",
    )(proj, proj, proj, proj, ln_g, ln_b, w00, b00, p_a)


def _merge_kernel(yb_ref, zb_ref, gb_ref, oa_ref, x_ref, cg_ref, pb_ref, wo_ref, fg_ref, o_ref):
    yb = (yb_ref[0] * jax.nn.silu(zb_ref[0])).astype(BF16)
    merged = oa_ref[0] + jax.nn.sigmoid(gb_ref[0]) * jnp.dot(yb, pb_ref[...],
                                                            preferred_element_type=F32)
    out = x_ref[0] + cg_ref[0] * jnp.dot(merged.astype(BF16), wo_ref[...],
                                         preferred_element_type=F32)
    ms = jnp.mean(out * out, axis=-1, keepdims=True)
    o_ref[0] = out * lax.rsqrt(ms + NORM_EPS) * fg_ref[...]


def _merge(yb, proj, oa, x, c_gate, p_b, w_out, final_g, tm):
    bsz, t, d = x.shape
    tg = c_gate.shape[1]
    row = lambda: pl.BlockSpec((1, tm, d), lambda b, i: (b, i, 0))
    sec = lambda c: pl.BlockSpec((1, tm, d), lambda b, i: (b, i, c))
    gate = (pl.BlockSpec((1, 1, d), lambda b, i: (b, 0, 0)) if tg == 1
            else pl.BlockSpec((1, tm, d), lambda b, i: (b, i, 0)))
    wspec = lambda: pl.BlockSpec((d, d), lambda b, i: (0, 0), pipeline_mode=pl.Buffered(1))
    return pl.pallas_call(
        _merge_kernel,
        grid=(bsz, t // tm),
        in_specs=[row(), sec(COL_ZB), sec(COL_GB), row(), row(), gate, wspec(), wspec(),
                  pl.BlockSpec((1, d), lambda b, i: (0, 0))],
        out_specs=row(),
        out_shape=jax.ShapeDtypeStruct((bsz, t, d), F32),
        compiler_params=_cparams("arbitrary", "arbitrary"),
        name="merge",
    )(yb, proj, proj, oa, x, c_gate, p_b, w_out, final_g.reshape(1, d))


def _split(t, sizes):
    out, o = [], 0
    for s in sizes:
        out.append(t[..., o:o + s])
        o += s
    return out


def _permute_shift_cols(x):
    r, wd, k, v, ad = _split(x, (B_W, LORA, B_W, B_W, LORA))
    pad = jnp.zeros(x.shape[:-1] + (LORA_W - 2 * LORA,), x.dtype)
    return r, k, v, jnp.concatenate([wd, ad, pad], axis=-1)


def _shift_row(proj_row):
    r = proj_row[..., 0:B_W]
    k = proj_row[..., B_W:2 * B_W]
    v = proj_row[..., 2 * B_W:3 * B_W]
    lo = proj_row[..., COL_LORA * LORA_W:]
    return jnp.concatenate([r, lo[..., :LORA], k, v, lo[..., LORA:2 * LORA]], axis=-1)


def kernel(x_prompt, x_sample, c_prompt, c_sample, state_wkv, state_shift, norm_g, w_c, b_c, w_in, ln_v_g, ln_v_b, w_s, b_s, mu_shift, w0, w2, a0, a2, k_k, k_a, r_k, gn_g, gn_b, p_a, p_b, w_out, final_g):
    assert norm_g.shape[0] == 1, "single-layer trunk"
    bp, t, d = x_prompt.shape
    bs = x_sample.shape[0]

    ua, va, za, prw, zb, ga, gb = _split(w_in[0], (A_W, A_W, A_W, 3 * B_W + 2 * LORA, B_W, D_MODEL, D_MODEL))
    wr, wk, wv, wlo = _permute_shift_cols(prw)
    w_p = jnp.concatenate([wr, wk, wv, zb, ga, gb, ua, va, za, wlo], axis=1).astype(BF16)
    mu = [m.reshape(1, -1) for m in _permute_shift_cols(mu_shift[0])]
    zrow = jnp.zeros((LORA_W - LORA, B_W), F32)
    w2p = jnp.concatenate([w2[0], zrow], axis=0).astype(BF16)
    a2p = jnp.concatenate([jnp.zeros((LORA, B_W), F32), a2[0],
                           jnp.zeros((LORA_W - 2 * LORA, B_W), F32)], axis=0).astype(BF16)
    w0r, a0r = w0[0].reshape(1, B_W), a0[0].reshape(1, B_W)
    p_a_b, p_b_b, w_out_b = p_a[0].astype(BF16), p_b[0].astype(BF16), w_out[0].astype(BF16)
    ln_g, ln_b = ln_v_g[0].reshape(1, A_W), ln_v_b[0].reshape(1, A_W)
    bias2d = jnp.repeat(b_s[0].T, CHUNK, axis=1)
    w00 = jnp.repeat(w_s[0][:, 0, 0], CHUNK).reshape(1, A_W)
    b00 = jnp.repeat(b_s[0][:, 0], CHUNK).reshape(1, A_W)

    m_rows = bp + bs
    m_pad = -m_rows % V7X_SUBLANES
    c_all = jnp.concatenate([c_prompt, c_sample, jnp.zeros((m_pad, d), F32)], axis=0)
    mod = _modulation(c_all, w_c[0], b_c[0])
    shift_p, scale_p, gate_p = (mod[:bp, i * d:(i + 1) * d].reshape(bp, 1, d) for i in range(3))
    shift_s, scale_s, gate_s = (mod[bp:m_rows, i * d:(i + 1) * d].reshape(1, bs, d) for i in range(3))

    def wkv_consts(bsz):
        return [_head_const(c, bsz) for c in (k_k[0], k_a[0], r_k[0], gn_g[0], gn_b[0])]

    h_p = _norm_modulate(x_prompt, norm_g[0], scale_p, shift_p, tm=512)
    proj_p = _in_projection(h_p, w_p)
    zeros_shift = [jnp.zeros((bp, 1, w), F32) for w in (B_W, B_W, B_W, LORA_W)]
    feats = _rwkv_prep_seq(proj_p, zeros_shift, mu, w0r, a0r, w2p, a2p)
    feats = [_to_heads_on_lanes(f) for f in feats]
    s0_p = jnp.zeros((HEAD_SIZE, HEAD_SIZE, bp * B_HEADS), F32)
    y_hl, s_hl = _wkv(*feats, s0_p, wkv_consts(bp), tt=32)
    yb_p = _from_heads_on_lanes(y_hl, bp)
    oa_p = _branch_a_seq(proj_p, ln_g, ln_b, w_s[0], bias2d, p_a_b)
    y_prompt = _merge(yb_p, proj_p, oa_p, x_prompt, gate_p, p_b_b, w_out_b, final_g, tm=256)
    wkv_prompt = jnp.transpose(s_hl.reshape(HEAD_SIZE, HEAD_SIZE, bp, B_HEADS), (2, 3, 1, 0))[None]
    shift_prompt = _shift_row(proj_p[:, t - 1, :])[None]

    xs = x_sample.reshape(1, bs, d)
    h_s = _norm_modulate(xs, norm_g[0], scale_s, shift_s, tm=bs)
    proj_s = _in_projection(h_s, w_p)
    feats = _rwkv_prep_tok(proj_s, _permute_shift_cols(state_shift[0]), mu, w0r, a0r, w2p, a2p)
    feats = [_to_heads_on_lanes(f.reshape(bs, 1, B_W)) for f in feats]
    s0_s = jnp.transpose(state_wkv[0], (3, 2, 0, 1)).reshape(HEAD_SIZE, HEAD_SIZE, bs * B_HEADS)
    y_hl, s_hl = _wkv(*feats, s0_s, wkv_consts(bs), tt=1)
    yb_s = _from_heads_on_lanes(y_hl, bs).reshape(1, bs, B_W)
    oa_s, v_s = _branch_a_tok(proj_s, ln_g, ln_b, w00, b00, p_a_b)
    y_s = _merge(yb_s, proj_s, oa_s, xs, gate_s, p_b_b, w_out_b, final_g, tm=bs)
    y_sample = y_s.reshape(bs, 1, d)
    wkv_sample = jnp.transpose(s_hl.reshape(HEAD_SIZE, HEAD_SIZE, bs, B_HEADS), (2, 3, 1, 0))[None]
    shift_sample = _shift_row(proj_s[0])[None]
    chunk_v_sample = v_s.reshape(1, bs, 1, A_W)

    return (y_prompt, y_sample, wkv_prompt, shift_prompt, wkv_sample, shift_sample, chunk_v_sample)
```

```python
import functools

import jax
import jax.numpy as jnp
from jax import lax
from jax.experimental import pallas as pl
from jax.experimental.pallas import tpu as pltpu

F32 = jnp.float32
BF16 = jnp.bfloat16

D_MODEL = 2048
A_W = 1024
A_GROUPS = 8
CHUNK = 128
B_HEADS = 32
HEAD_SIZE = 64
B_W = B_HEADS * HEAD_SIZE
LORA = 96
NORM_EPS = 1e-6
GN_EPS = HEAD_SIZE * 1e-5

LORA_W = 512
PROJ_W = 6 * B_W + 3 * A_W + LORA_W
COL_R, COL_K, COL_V, COL_ZB, COL_GA, COL_GB = 0, 1, 2, 3, 4, 5
COL_UA, COL_VA, COL_ZA = 12, 13, 14
COL_LORA = (6 * B_W + 3 * A_W) // LORA_W

V7X_LANES = 128
V7X_SUBLANES = 8
VMEM_LIMIT = 56 * 1024 * 1024


def _cparams(*sem):
    return pltpu.CompilerParams(dimension_semantics=sem, vmem_limit_bytes=VMEM_LIMIT)


def _mod_kernel(c_ref, w_ref, b_ref, o_ref):
    acc = jnp.dot(c_ref[...].astype(BF16), w_ref[...].astype(BF16), preferred_element_type=F32)
    o_ref[...] = acc + b_ref[...]


def _modulation(c_all, w_c, b_c):
    m, d = c_all.shape
    n = w_c.shape[1]
    tn = 768
    return pl.pallas_call(
        _mod_kernel,
        grid=(n // tn,),
        in_specs=[pl.BlockSpec((m, d), lambda j: (0, 0)),
                  pl.BlockSpec((d, tn), lambda j: (0, j)),
                  pl.BlockSpec((1, tn), lambda j: (0, j))],
        out_specs=pl.BlockSpec((m, tn), lambda j: (0, j)),
        out_shape=jax.ShapeDtypeStruct((m, n), F32),
        compiler_params=_cparams("arbitrary"),
        name="modulation",
    )(c_all, w_c, b_c.reshape(1, n))


def _normmod_kernel(x_ref, g_ref, sc_ref, sh_ref, o_ref):
    x = x_ref[0]
    ms = jnp.mean(x * x, axis=-1, keepdims=True)
    y = x * lax.rsqrt(ms + NORM_EPS) * g_ref[...]
    o_ref[0] = (y * (1.0 + sc_ref[0]) + sh_ref[0]).astype(BF16)


def _norm_modulate(x, g, scale, shift, tm):
    bsz, t, d = x.shape
    ts = scale.shape[1]
    sspec = (pl.BlockSpec((1, 1, d), lambda b, i: (b, 0, 0)) if ts == 1
             else pl.BlockSpec((1, tm, d), lambda b, i: (b, i, 0)))
    return pl.pallas_call(
        _normmod_kernel,
        grid=(bsz, t // tm),
        in_specs=[pl.BlockSpec((1, tm, d), lambda b, i: (b, i, 0)),
                  pl.BlockSpec((1, d), lambda b, i: (0, 0)),
                  sspec, sspec],
        out_specs=pl.BlockSpec((1, tm, d), lambda b, i: (b, i, 0)),
        out_shape=jax.ShapeDtypeStruct((bsz, t, d), BF16),
        compiler_params=_cparams("arbitrary", "arbitrary"),
        name="norm_modulate",
    )(x, g.reshape(1, d), scale, shift)


def _inproj_kernel(h_ref, w_ref, o_ref):
    o_ref[0] = jnp.dot(h_ref[0], w_ref[...], preferred_element_type=F32)


def _in_projection(h, w_p, tn=512):
    bsz, t, d = h.shape
    n = w_p.shape[1]
    return pl.pallas_call(
        _inproj_kernel,
        grid=(bsz, n // tn),
        in_specs=[pl.BlockSpec((1, t, d), lambda b, j: (b, 0, 0)),
                  pl.BlockSpec((d, tn), lambda b, j: (0, j))],
        out_specs=pl.BlockSpec((1, t, tn), lambda b, j: (b, 0, j)),
        out_shape=jax.ShapeDtypeStruct((bsz, t, n), F32),
        compiler_params=_cparams("arbitrary", "arbitrary"),
        name="in_projection",
    )(h, w_p)


def _prep_core(p, q, mu, w0, a0, w2p, a2p):
    r, k, v, lo = (pi + mi * (qi - pi) for pi, qi, mi in zip(p, q, mu))
    col = lax.broadcasted_iota(jnp.int32, lo.shape, 1)
    lt = jnp.where(col < LORA, jnp.tanh(lo), lo).astype(BF16)
    wl = w0 + jnp.dot(lt, w2p, preferred_element_type=F32)
    al = a0 + jnp.dot(lt, a2p, preferred_element_type=F32)
    w_log = -jax.nn.softplus(-wl) - 0.5
    decay = jnp.exp(-jnp.exp(w_log))
    a = jax.nn.sigmoid(al)
    return r, decay, k, v, a


def _prep_seq_kernel(pr, pk, pv, plo, tr, tk, tv, tlo, sr, sk, sv, slo,
                     mr, mk, mv, mlo, w0, a0, w2p, a2p,
                     o_r, o_w, o_k, o_v, o_a):
    i = pl.program_id(1)

    def prev(p_ref, tail_ref, s_ref):
        p = p_ref[0]
        last = tail_ref[0][V7X_SUBLANES - 1:V7X_SUBLANES, :]
        first = jnp.where(i == 0, s_ref[0], last)
        rolled = pltpu.roll(p, 1, axis=0)
        row = lax.broadcasted_iota(jnp.int32, p.shape, 0)
        return p, jnp.where(row == 0, first, rolled)

    pq = [prev(a, b, c) for a, b, c in ((pr, tr, sr), (pk, tk, sk), (pv, tv, sv), (plo, tlo, slo))]
    outs = _prep_core([x[0] for x in pq], [x[1] for x in pq],
                      (mr[...], mk[...], mv[...], mlo[...]), w0[...], a0[...], w2p[...], a2p[...])
    for o, val in zip((o_r, o_w, o_k, o_v, o_a), outs):
        o[0] = val


def _prep_tok_kernel(pr, pk, pv, plo, sr, sk, sv, slo,
                     mr, mk, mv, mlo, w0, a0, w2p, a2p,
                     o_r, o_w, o_k, o_v, o_a):
    outs = _prep_core((pr[0], pk[0], pv[0], plo[0]), (sr[...], sk[...], sv[...], slo[...]),
                      (mr[...], mk[...], mv[...], mlo[...]), w0[...], a0[...], w2p[...], a2p[...])
    for o, val in zip((o_r, o_w, o_k, o_v, o_a), outs):
        o[0] = val


def _rwkv_prep_seq(proj, shift0, mu, w0, a0, w2p, a2p, tm=128):
    bsz, t, _ = proj.shape
    tail = tm // V7X_SUBLANES
    cols = ((B_W, COL_R), (B_W, COL_K), (B_W, COL_V), (LORA_W, COL_LORA))
    cur = [pl.BlockSpec((1, tm, w), functools.partial(lambda b, i, c: (b, i, c), c=c)) for w, c in cols]
    tails = [pl.BlockSpec((1, V7X_SUBLANES, w),
                          functools.partial(lambda b, i, c: (b, jnp.maximum(i * tail - 1, 0), c), c=c))
             for w, c in cols]
    s0 = [pl.BlockSpec((1, 1, w), lambda b, i: (b, 0, 0)) for w, _ in cols]
    rowv = [pl.BlockSpec((1, w), lambda b, i: (0, 0)) for w, _ in cols]
    full = lambda shape: pl.BlockSpec(shape, lambda b, i: (0,) * len(shape))
    out_spec = pl.BlockSpec((1, tm, B_W), lambda b, i: (b, i, 0))
    return pl.pallas_call(
        _prep_seq_kernel,
        grid=(bsz, t // tm),
        in_specs=cur + tails + s0 + rowv + [full((1, B_W)), full((1, B_W)),
                                            full((LORA_W, B_W)), full((LORA_W, B_W))],
        out_specs=[out_spec] * 5,
        out_shape=[jax.ShapeDtypeStruct((bsz, t, B_W), F32)] * 5,
        compiler_params=_cparams("arbitrary", "arbitrary"),
        name="rwkv_prep_seq",
    )(proj, proj, proj, proj, proj, proj, proj, proj, *shift0, *mu, w0, a0, w2p, a2p)


def _rwkv_prep_tok(proj, shift0, mu, w0, a0, w2p, a2p):
    _, m, _ = proj.shape
    cols = ((B_W, COL_R), (B_W, COL_K), (B_W, COL_V), (LORA_W, COL_LORA))
    cur = [pl.BlockSpec((1, m, w), functools.partial(lambda i, c: (0, 0, c), c=c)) for w, c in cols]
    s0 = [pl.BlockSpec((m, w), lambda i: (0, 0)) for w, _ in cols]
    rowv = [pl.BlockSpec((1, w), lambda i: (0, 0)) for w, _ in cols]
    full = lambda shape: pl.BlockSpec(shape, lambda i: (0,) * len(shape))
    out_spec = pl.BlockSpec((1, m, B_W), lambda i: (0, 0, 0))
    return pl.pallas_call(
        _prep_tok_kernel,
        grid=(1,),
        in_specs=cur + s0 + rowv + [full((1, B_W)), full((1, B_W)),
                                    full((LORA_W, B_W)), full((LORA_W, B_W))],
        out_specs=[out_spec] * 5,
        out_shape=[jax.ShapeDtypeStruct((1, m, B_W), F32)] * 5,
        compiler_params=_cparams("arbitrary"),
        name="rwkv_prep_tok",
    )(proj, proj, proj, proj, *shift0, *mu, w0, a0, w2p, a2p)


WKV_ROWS = 32
WKV_GROUP = 4


def _wkv_kernel(r_ref, w_ref, k_ref, v_ref, a_ref, s0_ref, kk_c, ka_c, rk_c, gg_c, gb_c,
                y_ref, s_ref, nkk_s, b_s, kp_s, ys_s, *, tt):
    step = pl.program_id(1)

    @pl.when(step == 0)
    def _():
        s_ref[...] = s0_ref[...]

    def prep(q, c):
        for d in range(WKV_GROUP):
            t = q * WKV_GROUP + d
            k = k_ref[t]
            a = a_ref[t]
            kk = k * kk_c[...]
            nrm = jnp.sqrt(jnp.sum(kk * kk, axis=0, keepdims=True))
            kk = kk / jnp.maximum(nrm, 1e-12)
            nkk_s[t] = -kk
            b_s[t] = kk * a
            kp_s[t] = k * (1.0 + (a - 1.0) * ka_c[...])
        return c

    lax.fori_loop(0, tt // WKV_GROUP, prep, 0)

    halves = [slice(h * WKV_ROWS, (h + 1) * WKV_ROWS) for h in range(HEAD_SIZE // WKV_ROWS)]

    def bcast(ref, t, j):
        return jnp.broadcast_to(ref[t, pl.ds(j, 1), :], (WKV_ROWS, V7X_LANES))

    def first_u(sl):
        acc = jnp.zeros((WKV_ROWS, V7X_LANES), F32)
        for j in range(HEAD_SIZE):
            acc = acc + s_ref[j, sl, :] * bcast(nkk_s, 0, j)
        return acc

    def token(t, u):
        t_next = jnp.minimum(t + 1, tt - 1)
        u_next = []
        for sl, uh in zip(halves, u):
            v = v_ref[t, sl, :]
            y_acc = jnp.zeros((WKV_ROWS, V7X_LANES), F32)
            u_acc = jnp.zeros((WKV_ROWS, V7X_LANES), F32)
            for j in range(HEAD_SIZE):
                sj = (s_ref[j, sl, :] * bcast(w_ref, t, j) + uh * bcast(b_s, t, j)) + v * bcast(kp_s, t, j)
                s_ref[j, sl, :] = sj
                y_acc = y_acc + sj * bcast(r_ref, t, j)
                u_acc = u_acc + sj * bcast(nkk_s, t_next, j)
            ys_s[t, sl, :] = y_acc
            u_next.append(u_acc)
        return tuple(u_next)

    lax.fori_loop(0, tt, token, tuple(first_u(sl) for sl in halves))

    def post(q, c):
        for d in range(WKV_GROUP):
            t = q * WKV_GROUP + d
            y = ys_s[t]
            mu = jnp.mean(y, axis=0, keepdims=True)
            yc = y - mu
            var = jnp.mean(yc * yc, axis=0, keepdims=True)
            yn = yc * lax.rsqrt(var + GN_EPS) * gg_c[...] + gb_c[...]
            bonus = jnp.sum(r_ref[t] * kp_s[t] * rk_c[...], axis=0, keepdims=True) * v_ref[t]
            y_ref[t] = yn + bonus
        return c

    lax.fori_loop(0, tt // WKV_GROUP, post, 0)


def _wkv(r, w, k, v, a, s0, consts, tt):
    t, n, lanes = r.shape
    seq = pl.BlockSpec((tt, n, V7X_LANES), lambda l, i: (i, 0, l))
    st = pl.BlockSpec((n, n, V7X_LANES), lambda l, i: (0, 0, l))
    cst = pl.BlockSpec((n, V7X_LANES), lambda l, i: (0, l))
    return pl.pallas_call(
        functools.partial(_wkv_kernel, tt=tt),
        grid=(lanes // V7X_LANES, t // tt),
        in_specs=[seq] * 5 + [st] + [cst] * 5,
        out_specs=[seq, st],
        out_shape=[jax.ShapeDtypeStruct((t, n, lanes), F32),
                   jax.ShapeDtypeStruct((n, n, lanes), F32)],
        scratch_shapes=[pltpu.VMEM((tt, n, V7X_LANES), F32)] * 4,
        compiler_params=_cparams("arbitrary", "arbitrary"),
        name="wkv",
    )(r, w, k, v, a, s0, *consts)


def _wkv_tok_kernel(r_ref, w_ref, k_ref, a_ref, rt_ref, kt_ref, at_ref, vt_ref, s0_ref,
                    kk_c, ka_c, kat_c, rkt_c, ggt_c, gbt_c, yt_ref, s_ref, ycol_s, *, bb):
    def per_seq(bi, c):
        k = k_ref[bi]
        a = a_ref[bi]
        r = r_ref[bi]
        w = w_ref[bi]
        kk = k * kk_c[...]
        nrm = jnp.sqrt(jnp.sum(kk * kk, axis=1, keepdims=True))
        kk = kk / jnp.maximum(nrm, 1e-12)
        bvec = kk * a
        kp = k * (1.0 + (a - 1.0) * ka_c[...])
        vt = vt_ref[bi]
        nkk = -kk
        wr = w * r
        br = jnp.sum(bvec * r, axis=1, keepdims=True)
        kr = jnp.sum(kp * r, axis=1, keepdims=True)
        for h in range(B_HEADS):
            s = s0_ref[bi, h]
            u = jnp.sum(s * nkk[h:h + 1, :], axis=1, keepdims=True)
            yw = jnp.sum(s * wr[h:h + 1, :], axis=1, keepdims=True)
            vc = vt[:, h:h + 1]
            s_ref[bi, h] = (s * w[h:h + 1, :] + u * bvec[h:h + 1, :]) + vc * kp[h:h + 1, :]
            ycol_s[:, h:h + 1] = (yw + u * br[h:h + 1, :]) + vc * kr[h:h + 1, :]
        y = ycol_s[...]
        mu = jnp.mean(y, axis=0, keepdims=True)
        yc = y - mu
        var = jnp.mean(yc * yc, axis=0, keepdims=True)
        yn = yc * lax.rsqrt(var + GN_EPS) * ggt_c[...] + gbt_c[...]
        kpt = kt_ref[bi] * (1.0 + (at_ref[bi] - 1.0) * kat_c[...])
        rks = jnp.sum(rt_ref[bi] * kpt * rkt_c[...], axis=0, keepdims=True)
        yt_ref[bi] = yn + rks * vt
        return c

    lax.fori_loop(0, bb, per_seq, 0)


def _wkv_tok(r, w, k, v, a, s0, k_k, k_a, r_k, gn_g, gn_b, bb=4):
    bsz = r.shape[0]
    head = lambda x: x.reshape(B_HEADS, HEAD_SIZE)
    tr = lambda x: jnp.swapaxes(x, -1, -2)
    row = pl.BlockSpec((bb, B_HEADS, HEAD_SIZE), lambda i: (i, 0, 0))
    colv = pl.BlockSpec((bb, HEAD_SIZE, B_HEADS), lambda i: (i, 0, 0))
    st = pl.BlockSpec((bb, B_HEADS, HEAD_SIZE, HEAD_SIZE), lambda i: (i, 0, 0, 0))
    crow = pl.BlockSpec((B_HEADS, HEAD_SIZE), lambda i: (0, 0))
    ccol = pl.BlockSpec((HEAD_SIZE, B_HEADS), lambda i: (0, 0))
    yt, s = pl.pallas_call(
        functools.partial(_wkv_tok_kernel, bb=bb),
        grid=(bsz // bb,),
        in_specs=[row] * 4 + [colv] * 4 + [st] + [crow] * 2 + [ccol] * 4,
        out_specs=[colv, st],
        out_shape=[jax.ShapeDtypeStruct((bsz, HEAD_SIZE, B_HEADS), F32),
                   jax.ShapeDtypeStruct(s0.shape, F32)],
        scratch_shapes=[pltpu.VMEM((HEAD_SIZE, B_HEADS), F32)],
        compiler_params=_cparams("arbitrary"),
        name="wkv_tok",
    )(r, w, k, a, tr(r), tr(k), tr(a), tr(v), s0,
      head(k_k), head(k_a), tr(head(k_a)), tr(head(r_k)), tr(head(gn_g)), tr(head(gn_b)))
    return tr(yt).reshape(bsz, B_W), s


def _to_heads_on_lanes(x):
    bsz, t, _ = x.shape
    x = x.reshape(bsz, t, B_HEADS, HEAD_SIZE)
    return jnp.transpose(x, (1, 3, 0, 2)).reshape(t, HEAD_SIZE, bsz * B_HEADS)


def _from_heads_on_lanes(y, bsz):
    t = y.shape[0]
    y = y.reshape(t, HEAD_SIZE, bsz, B_HEADS)
    return jnp.transpose(y, (2, 0, 3, 1)).reshape(bsz, t, B_W)


def _head_const(x, bsz):
    return jnp.tile(x.reshape(B_HEADS, HEAD_SIZE).T, (1, bsz))


def _gelu(x):
    return jax.nn.gelu(x)


def _layernorm(x, g, b):
    mu = jnp.mean(x, axis=-1, keepdims=True)
    xc = x - mu
    var = jnp.mean(xc * xc, axis=-1, keepdims=True)
    return xc * lax.rsqrt(var + NORM_EPS) * g + b


def _branch_a_seq_kernel(u_ref, v_ref, z_ref, g_ref, lg, lb, ws_ref, bias_ref, pa_ref, o_ref, *, tm):
    u = _gelu(u_ref[0])
    v = _layernorm(_gelu(v_ref[0]), lg[...], lb[...])
    vb = v.astype(BF16)
    row = lax.broadcasted_iota(jnp.int32, (CHUNK, CHUNK), 0)
    col = lax.broadcasted_iota(jnp.int32, (CHUNK, CHUNK), 1)
    rows = []
    for c in range(tm // CHUNK):
        blocks = []
        for g in range(A_GROUPS):
            wm = jnp.where(row >= col, ws_ref[g], 0.0).astype(BF16)
            blk = vb[c * CHUNK:(c + 1) * CHUNK, g * CHUNK:(g + 1) * CHUNK]
            blocks.append(jnp.dot(wm, blk, preferred_element_type=F32))
        rows.append(jnp.concatenate(blocks, axis=1) + bias_ref[...])
    mix = jnp.concatenate(rows, axis=0)
    ya = u * mix * jax.nn.silu(z_ref[0])
    o_ref[0] = jax.nn.sigmoid(g_ref[0]) * jnp.dot(ya.astype(BF16), pa_ref[...],
                                                  preferred_element_type=F32)


def _branch_a_tok_kernel(u_ref, v_ref, z_ref, g_ref, lg, lb, w00, b00, pa_ref, o_ref, vo_ref):
    u = _gelu(u_ref[0])
    v = _layernorm(_gelu(v_ref[0]), lg[...], lb[...])
    vo_ref[...] = v
    mix = v * w00[...] + b00[...]
    ya = u * mix * jax.nn.silu(z_ref[0])
    o_ref[0] = jax.nn.sigmoid(g_ref[0]) * jnp.dot(ya.astype(BF16), pa_ref[...],
                                                  preferred_element_type=F32)


def _branch_a_seq(proj, ln_g, ln_b, w_s, bias2d, p_a, tm=256):
    bsz, t, _ = proj.shape
    full = lambda shape: pl.BlockSpec(shape, lambda b, i: (0,) * len(shape))
    sec = lambda w, c: pl.BlockSpec((1, tm, w), lambda b, i: (b, i, c))
    return pl.pallas_call(
        functools.partial(_branch_a_seq_kernel, tm=tm),
        grid=(bsz, t // tm),
        in_specs=[sec(A_W, COL_UA), sec(A_W, COL_VA), sec(A_W, COL_ZA), sec(D_MODEL, COL_GA),
                  full((1, A_W)), full((1, A_W)), full((A_GROUPS, CHUNK, CHUNK)),
                  full((CHUNK, A_W)), full((A_W, D_MODEL))],
        out_specs=pl.BlockSpec((1, tm, D_MODEL), lambda b, i: (b, i, 0)),
        out_shape=jax.ShapeDtypeStruct((bsz, t, D_MODEL), F32),
        compiler_params=_cparams("arbitrary", "arbitrary"),
        name="branch_a_seq",
    )(proj, proj, proj, proj, ln_g, ln_b, w_s, bias2d, p_a)


def _branch_a_tok(proj, ln_g, ln_b, w00, b00, p_a):
    _, m, _ = proj.shape
    full = lambda shape: pl.BlockSpec(shape, lambda i: (0,) * len(shape))
    sec = lambda w, c: pl.BlockSpec((1, m, w), lambda i: (0, 0, c))
    return pl.pallas_call(
        _branch_a_tok_kernel,
        grid=(1,),
        in_specs=[sec(A_W, COL_UA), sec(A_W, COL_VA), sec(A_W, COL_ZA), sec(D_MODEL, COL_GA),
                  full((1, A_W)), full((1, A_W)), full((1, A_W)), full((1, A_W)),
                  full((A_W, D_MODEL))],
        out_specs=[pl.BlockSpec((1, m, D_MODEL), lambda i: (0, 0, 0)),
                   pl.BlockSpec((m, A_W), lambda i: (0, 0))],
        out_shape=[jax.ShapeDtypeStruct((1, m, D_MODEL), F32),
                   jax.ShapeDtypeStruct((m, A_W), F32)],
        compiler_params=_cparams("arbitrary"),
        name="branch_a_tok",
    )(proj, proj, proj, proj, ln_g, ln_b, w00, b00, p_a)


def _merge_kernel(yb_ref, zb_ref, gb_ref, oa_ref, x_ref, cg_ref, pb_ref, wo_ref, fg_ref, o_ref):
    yb = (yb_ref[0] * jax.nn.silu(zb_ref[0])).astype(BF16)
    merged = oa_ref[0] + jax.nn.sigmoid(gb_ref[0]) * jnp.dot(yb, pb_ref[...],
                                                            preferred_element_type=F32)
    out = x_ref[0] + cg_ref[0] * jnp.dot(merged.astype(BF16), wo_ref[...],
                                         preferred_element_type=F32)
    ms = jnp.mean(out * out, axis=-1, keepdims=True)
    o_ref[0] = out * lax.rsqrt(ms + NORM_EPS) * fg_ref[...]


def _merge(yb, proj, oa, x, c_gate, p_b, w_out, final_g, tm):
    bsz, t, d = x.shape
    tg = c_gate.shape[1]
    row = lambda: pl.BlockSpec((1, tm, d), lambda b, i: (b, i, 0))
    sec = lambda c: pl.BlockSpec((1, tm, d), lambda b, i: (b, i, c))
    gate = (pl.BlockSpec((1, 1, d), lambda b, i: (b, 0, 0)) if tg == 1
            else pl.BlockSpec((1, tm, d), lambda b, i: (b, i, 0)))
    wspec = lambda: pl.BlockSpec((d, d), lambda b, i: (0, 0), pipeline_mode=pl.Buffered(1))
    return pl.pallas_call(
        _merge_kernel,
        grid=(bsz, t // tm),
        in_specs=[row(), sec(COL_ZB), sec(COL_GB), row(), row(), gate, wspec(), wspec(),
                  pl.BlockSpec((1, d), lambda b, i: (0, 0))],
        out_specs=row(),
        out_shape=jax.ShapeDtypeStruct((bsz, t, d), F32),
        compiler_params=_cparams("arbitrary", "arbitrary"),
        name="merge",
    )(yb, proj, proj, oa, x, c_gate, p_b, w_out, final_g.reshape(1, d))


def _split(t, sizes):
    out, o = [], 0
    for s in sizes:
        out.append(t[..., o:o + s])
        o += s
    return out


def _permute_shift_cols(x):
    r, wd, k, v, ad = _split(x, (B_W, LORA, B_W, B_W, LORA))
    pad = jnp.zeros(x.shape[:-1] + (LORA_W - 2 * LORA,), x.dtype)
    return r, k, v, jnp.concatenate([wd, ad, pad], axis=-1)


def _shift_row(proj_row):
    r = proj_row[..., 0:B_W]
    k = proj_row[..., B_W:2 * B_W]
    v = proj_row[..., 2 * B_W:3 * B_W]
    lo = proj_row[..., COL_LORA * LORA_W:]
    return jnp.concatenate([r, lo[..., :LORA], k, v, lo[..., LORA:2 * LORA]], axis=-1)


def kernel(x_prompt, x_sample, c_prompt, c_sample, state_wkv, state_shift, norm_g, w_c, b_c, w_in, ln_v_g, ln_v_b, w_s, b_s, mu_shift, w0, w2, a0, a2, k_k, k_a, r_k, gn_g, gn_b, p_a, p_b, w_out, final_g):
    assert norm_g.shape[0] == 1, "single-layer trunk"
    bp, t, d = x_prompt.shape
    bs = x_sample.shape[0]

    ua, va, za, prw, zb, ga, gb = _split(w_in[0], (A_W, A_W, A_W, 3 * B_W + 2 * LORA, B_W, D_MODEL, D_MODEL))
    wr, wk, wv, wlo = _permute_shift_cols(prw)
    w_p = jnp.concatenate([wr, wk, wv, zb, ga, gb, ua, va, za, wlo], axis=1).astype(BF16)
    mu = [m.reshape(1, -1) for m in _permute_shift_cols(mu_shift[0])]
    zrow = jnp.zeros((LORA_W - LORA, B_W), F32)
    w2p = jnp.concatenate([w2[0], zrow], axis=0).astype(BF16)
    a2p = jnp.concatenate([jnp.zeros((LORA, B_W), F32), a2[0],
                           jnp.zeros((LORA_W - 2 * LORA, B_W), F32)], axis=0).astype(BF16)
    w0r, a0r = w0[0].reshape(1, B_W), a0[0].reshape(1, B_W)
    p_a_b, p_b_b, w_out_b = p_a[0].astype(BF16), p_b[0].astype(BF16), w_out[0].astype(BF16)
    ln_g, ln_b = ln_v_g[0].reshape(1, A_W), ln_v_b[0].reshape(1, A_W)
    bias2d = jnp.repeat(b_s[0].T, CHUNK, axis=1)
    w00 = jnp.repeat(w_s[0][:, 0, 0], CHUNK).reshape(1, A_W)
    b00 = jnp.repeat(b_s[0][:, 0], CHUNK).reshape(1, A_W)

    m_rows = bp + bs
    m_pad = -m_rows % V7X_SUBLANES
    c_all = jnp.concatenate([c_prompt, c_sample, jnp.zeros((m_pad, d), F32)], axis=0)
    mod = _modulation(c_all, w_c[0], b_c[0])
    shift_p, scale_p, gate_p = (mod[:bp, i * d:(i + 1) * d].reshape(bp, 1, d) for i in range(3))
    shift_s, scale_s, gate_s = (mod[bp:m_rows, i * d:(i + 1) * d].reshape(1, bs, d) for i in range(3))

    def wkv_consts(bsz):
        return [_head_const(c, bsz) for c in (k_k[0], k_a[0], r_k[0], gn_g[0], gn_b[0])]

    h_p = _norm_modulate(x_prompt, norm_g[0], scale_p, shift_p, tm=512)
    proj_p = _in_projection(h_p, w_p)
    zeros_shift = [jnp.zeros((bp, 1, w), F32) for w in (B_W, B_W, B_W, LORA_W)]
    feats = _rwkv_prep_seq(proj_p, zeros_shift, mu, w0r, a0r, w2p, a2p)
    feats = [_to_heads_on_lanes(f) for f in feats]
    s0_p = jnp.zeros((HEAD_SIZE, HEAD_SIZE, bp * B_HEADS), F32)
    y_hl, s_hl = _wkv(*feats, s0_p, wkv_consts(bp), tt=64)
    yb_p = _from_heads_on_lanes(y_hl, bp)
    oa_p = _branch_a_seq(proj_p, ln_g, ln_b, w_s[0], bias2d, p_a_b)
    y_prompt = _merge(yb_p, proj_p, oa_p, x_prompt, gate_p, p_b_b, w_out_b, final_g, tm=256)
    wkv_prompt = jnp.transpose(s_hl.reshape(HEAD_SIZE, HEAD_SIZE, bp, B_HEADS), (2, 3, 1, 0))[None]
    shift_prompt = _shift_row(proj_p[:, t - 1, :])[None]

    xs = x_sample.reshape(1, bs, d)
    h_s = _norm_modulate(xs, norm_g[0], scale_s, shift_s, tm=bs)
    proj_s = _in_projection(h_s, w_p)
    feats = _rwkv_prep_tok(proj_s, _permute_shift_cols(state_shift[0]), mu, w0r, a0r, w2p, a2p)
    f_r, f_w, f_k, f_v, f_a = (f.reshape(bs, B_HEADS, HEAD_SIZE) for f in feats)
    yb_s, wkv_s = _wkv_tok(f_r, f_w, f_k, f_v, f_a, state_wkv[0],
                           k_k[0], k_a[0], r_k[0], gn_g[0], gn_b[0])
    yb_s = yb_s.reshape(1, bs, B_W)
    oa_s, v_s = _branch_a_tok(proj_s, ln_g, ln_b, w00, b00, p_a_b)
    y_s = _merge(yb_s, proj_s, oa_s, xs, gate_s, p_b_b, w_out_b, final_g, tm=bs)
    y_sample = y_s.reshape(bs, 1, d)
    wkv_sample = wkv_s[None]
    shift_sample = _shift_row(proj_s[0])[None]
    chunk_v_sample = v_s.reshape(1, bs, 1, A_W)

    return (y_prompt, y_sample, wkv_prompt, shift_prompt, wkv_sample, shift_sample, chunk_v_sample)
```

```python
import functools

import jax
import jax.numpy as jnp
import numpy as np
from jax import lax
from jax.experimental import pallas as pl
from jax.experimental.pallas import tpu as pltpu

F32 = jnp.float32
BF16 = jnp.bfloat16

D_MODEL = 2048
A_W = 1024
A_GROUPS = 8
CHUNK = 128
B_HEADS = 32
HEAD_SIZE = 64
B_W = B_HEADS * HEAD_SIZE
LORA = 96
NORM_EPS = 1e-6
GN_EPS = HEAD_SIZE * 1e-5

LORA_W = 512
PROJ_W = 6 * B_W + 3 * A_W + LORA_W
COL_R, COL_K, COL_V, COL_ZB, COL_GA, COL_GB = 0, 1, 2, 3, 4, 5
COL_UA, COL_VA, COL_ZA = 12, 13, 14
COL_LORA = (6 * B_W + 3 * A_W) // LORA_W

V7X_LANES = 128
V7X_SUBLANES = 8
VMEM_LIMIT = 56 * 1024 * 1024


def _cparams(*sem):
    return pltpu.CompilerParams(dimension_semantics=sem, vmem_limit_bytes=VMEM_LIMIT)


def _mod_kernel(c_ref, w_ref, b_ref, o_ref):
    acc = jnp.dot(c_ref[...].astype(BF16), w_ref[...].astype(BF16), preferred_element_type=F32)
    o_ref[...] = acc + b_ref[...]


def _modulation(c_all, w_c, b_c):
    m, d = c_all.shape
    n = w_c.shape[1]
    tn = 768
    return pl.pallas_call(
        _mod_kernel,
        grid=(n // tn,),
        in_specs=[pl.BlockSpec((m, d), lambda j: (0, 0)),
                  pl.BlockSpec((d, tn), lambda j: (0, j)),
                  pl.BlockSpec((1, tn), lambda j: (0, j))],
        out_specs=pl.BlockSpec((m, tn), lambda j: (0, j)),
        out_shape=jax.ShapeDtypeStruct((m, n), F32),
        compiler_params=_cparams("arbitrary"),
        name="modulation",
    )(c_all, w_c, b_c.reshape(1, n))


def _normmod_kernel(x_ref, g_ref, sc_ref, sh_ref, o_ref):
    x = x_ref[0]
    ms = jnp.mean(x * x, axis=-1, keepdims=True)
    y = x * lax.rsqrt(ms + NORM_EPS) * g_ref[...]
    o_ref[0] = (y * (1.0 + sc_ref[0]) + sh_ref[0]).astype(BF16)


def _norm_modulate(x, g, scale, shift, tm):
    bsz, t, d = x.shape
    ts = scale.shape[1]
    sspec = (pl.BlockSpec((1, 1, d), lambda b, i: (b, 0, 0)) if ts == 1
             else pl.BlockSpec((1, tm, d), lambda b, i: (b, i, 0)))
    return pl.pallas_call(
        _normmod_kernel,
        grid=(bsz, t // tm),
        in_specs=[pl.BlockSpec((1, tm, d), lambda b, i: (b, i, 0)),
                  pl.BlockSpec((1, d), lambda b, i: (0, 0)),
                  sspec, sspec],
        out_specs=pl.BlockSpec((1, tm, d), lambda b, i: (b, i, 0)),
        out_shape=jax.ShapeDtypeStruct((bsz, t, d), BF16),
        compiler_params=_cparams("arbitrary", "arbitrary"),
        name="norm_modulate",
    )(x, g.reshape(1, d), scale, shift)


def _inproj_kernel(h_ref, w_ref, o_ref):
    o_ref[0] = lax.dot_general(h_ref[0], w_ref[...], (((1,), (1,)), ((), ())),
                               preferred_element_type=F32)


def _in_projection(h, w_t, tn=512):
    bsz, t, d = h.shape
    n = w_t.shape[0]
    return pl.pallas_call(
        _inproj_kernel,
        grid=(bsz, n // tn),
        in_specs=[pl.BlockSpec((1, t, d), lambda b, j: (b, 0, 0)),
                  pl.BlockSpec((tn, d), lambda b, j: (j, 0))],
        out_specs=pl.BlockSpec((1, t, tn), lambda b, j: (b, 0, j)),
        out_shape=jax.ShapeDtypeStruct((bsz, t, n), F32),
        compiler_params=_cparams("arbitrary", "arbitrary"),
        name="in_projection",
    )(h, w_t)


def _prep_core(p, q, mu, w0, a0, w2p, a2p):
    r, k, v, lo = (pi + mi * (qi - pi) for pi, qi, mi in zip(p, q, mu))
    col = lax.broadcasted_iota(jnp.int32, lo.shape, 1)
    lt = jnp.where(col < LORA, jnp.tanh(lo), lo).astype(BF16)
    wl = w0 + jnp.dot(lt, w2p, preferred_element_type=F32)
    al = a0 + jnp.dot(lt, a2p, preferred_element_type=F32)
    w_log = -jax.nn.softplus(-wl) - 0.5
    decay = jnp.exp(-jnp.exp(w_log))
    a = jax.nn.sigmoid(al)
    return r, decay, k, v, a


def _prep_seq_kernel(pr, pk, pv, plo, tr, tk, tv, tlo, sr, sk, sv, slo,
                     mr, mk, mv, mlo, w0, a0, w2p, a2p,
                     o_r, o_w, o_k, o_v, o_a):
    i = pl.program_id(1)

    def prev(p_ref, tail_ref, s_ref):
        p = p_ref[0]
        last = tail_ref[0][V7X_SUBLANES - 1:V7X_SUBLANES, :]
        first = jnp.where(i == 0, s_ref[0], last)
        rolled = pltpu.roll(p, 1, axis=0)
        row = lax.broadcasted_iota(jnp.int32, p.shape, 0)
        return p, jnp.where(row == 0, first, rolled)

    pq = [prev(a, b, c) for a, b, c in ((pr, tr, sr), (pk, tk, sk), (pv, tv, sv), (plo, tlo, slo))]
    outs = _prep_core([x[0] for x in pq], [x[1] for x in pq],
                      (mr[...], mk[...], mv[...], mlo[...]), w0[...], a0[...], w2p[...], a2p[...])
    for o, val in zip((o_r, o_w, o_k, o_v, o_a), outs):
        o[...] = val


def _prep_tok_kernel(pr, pk, pv, plo, sr, sk, sv, slo,
                     mr, mk, mv, mlo, w0, a0, w2p, a2p,
                     o_r, o_w, o_k, o_v, o_a):
    outs = _prep_core((pr[0], pk[0], pv[0], plo[0]), (sr[...], sk[...], sv[...], slo[...]),
                      (mr[...], mk[...], mv[...], mlo[...]), w0[...], a0[...], w2p[...], a2p[...])
    for o, val in zip((o_r, o_w, o_k, o_v, o_a), outs):
        o[0] = val


def _rwkv_prep_seq(proj, shift0, mu, w0, a0, w2p, a2p, tm=128):
    bsz, t, _ = proj.shape
    tail = tm // V7X_SUBLANES
    cols = ((B_W, COL_R), (B_W, COL_K), (B_W, COL_V), (LORA_W, COL_LORA))
    cur = [pl.BlockSpec((1, tm, w), functools.partial(lambda b, i, c: (b, i, c), c=c)) for w, c in cols]
    tails = [pl.BlockSpec((1, V7X_SUBLANES, w),
                          functools.partial(lambda b, i, c: (b, jnp.maximum(i * tail - 1, 0), c), c=c))
             for w, c in cols]
    s0 = [pl.BlockSpec((1, 1, w), lambda b, i: (b, 0, 0)) for w, _ in cols]
    rowv = [pl.BlockSpec((1, w), lambda b, i: (0, 0)) for w, _ in cols]
    full = lambda shape: pl.BlockSpec(shape, lambda b, i: (0,) * len(shape))
    out_spec = pl.BlockSpec((tm, B_W), lambda b, i: (i, b))
    return pl.pallas_call(
        _prep_seq_kernel,
        grid=(bsz, t // tm),
        in_specs=cur + tails + s0 + rowv + [full((1, B_W)), full((1, B_W)),
                                            full((LORA_W, B_W)), full((LORA_W, B_W))],
        out_specs=[out_spec] * 5,
        out_shape=[jax.ShapeDtypeStruct((t, bsz * B_W), F32)] * 5,
        compiler_params=_cparams("arbitrary", "arbitrary"),
        name="rwkv_prep_seq",
    )(proj, proj, proj, proj, proj, proj, proj, proj, *shift0, *mu, w0, a0, w2p, a2p)


def _rwkv_prep_tok(proj, shift0, mu, w0, a0, w2p, a2p):
    _, m, _ = proj.shape
    cols = ((B_W, COL_R), (B_W, COL_K), (B_W, COL_V), (LORA_W, COL_LORA))
    cur = [pl.BlockSpec((1, m, w), functools.partial(lambda i, c: (0, 0, c), c=c)) for w, c in cols]
    s0 = [pl.BlockSpec((m, w), lambda i: (0, 0)) for w, _ in cols]
    rowv = [pl.BlockSpec((1, w), lambda i: (0, 0)) for w, _ in cols]
    full = lambda shape: pl.BlockSpec(shape, lambda i: (0,) * len(shape))
    out_spec = pl.BlockSpec((1, m, B_W), lambda i: (0, 0, 0))
    return pl.pallas_call(
        _prep_tok_kernel,
        grid=(1,),
        in_specs=cur + s0 + rowv + [full((1, B_W)), full((1, B_W)),
                                    full((LORA_W, B_W)), full((LORA_W, B_W))],
        out_specs=[out_spec] * 5,
        out_shape=[jax.ShapeDtypeStruct((1, m, B_W), F32)] * 5,
        compiler_params=_cparams("arbitrary"),
        name="rwkv_prep_tok",
    )(proj, proj, proj, proj, *shift0, *mu, w0, a0, w2p, a2p)


WKV_ROWS = 32
WKV_GROUP = 4


def _wkv_kernel(r_ref, w_ref, k_ref, v_ref, a_ref, s0_ref, kk_c, ka_c, rk_c, gg_c, gb_c,
                y_ref, s_ref, nkk_s, b_s, kp_s, ys_s, *, tt):
    step = pl.program_id(1)

    @pl.when(step == 0)
    def _():
        s_ref[...] = s0_ref[...]

    def prep(q, c):
        for d in range(WKV_GROUP):
            t = q * WKV_GROUP + d
            k = k_ref[t]
            a = a_ref[t]
            kk = k * kk_c[...]
            nrm = jnp.sqrt(jnp.sum(kk * kk, axis=0, keepdims=True))
            kk = kk / jnp.maximum(nrm, 1e-12)
            nkk_s[t] = -kk
            b_s[t] = kk * a
            kp_s[t] = k * (1.0 + (a - 1.0) * ka_c[...])
        return c

    lax.fori_loop(0, tt // WKV_GROUP, prep, 0)

    halves = [slice(h * WKV_ROWS, (h + 1) * WKV_ROWS) for h in range(HEAD_SIZE // WKV_ROWS)]

    def bcast(ref, t, j):
        return jnp.broadcast_to(ref[t, pl.ds(j, 1), :], (WKV_ROWS, V7X_LANES))

    def first_u(sl):
        acc = jnp.zeros((WKV_ROWS, V7X_LANES), F32)
        for j in range(HEAD_SIZE):
            acc = acc + s_ref[j, sl, :] * bcast(nkk_s, 0, j)
        return acc

    def token(t, u):
        t_next = jnp.minimum(t + 1, tt - 1)
        u_next = []
        for sl, uh in zip(halves, u):
            v = v_ref[t, sl, :]
            y_acc = jnp.zeros((WKV_ROWS, V7X_LANES), F32)
            u_acc = jnp.zeros((WKV_ROWS, V7X_LANES), F32)
            for j in range(HEAD_SIZE):
                sj = (s_ref[j, sl, :] * bcast(w_ref, t, j) + uh * bcast(b_s, t, j)) + v * bcast(kp_s, t, j)
                s_ref[j, sl, :] = sj
                y_acc = y_acc + sj * bcast(r_ref, t, j)
                u_acc = u_acc + sj * bcast(nkk_s, t_next, j)
            ys_s[t, sl, :] = y_acc
            u_next.append(u_acc)
        return tuple(u_next)

    lax.fori_loop(0, tt, token, tuple(first_u(sl) for sl in halves))

    def post(q, c):
        for d in range(WKV_GROUP):
            t = q * WKV_GROUP + d
            y = ys_s[t]
            mu = jnp.mean(y, axis=0, keepdims=True)
            yc = y - mu
            var = jnp.mean(yc * yc, axis=0, keepdims=True)
            yn = yc * lax.rsqrt(var + GN_EPS) * gg_c[...] + gb_c[...]
            bonus = jnp.sum(r_ref[t] * kp_s[t] * rk_c[...], axis=0, keepdims=True) * v_ref[t]
            y_ref[t] = yn + bonus
        return c

    lax.fori_loop(0, tt // WKV_GROUP, post, 0)


def _wkv(r, w, k, v, a, s0, consts, tt):
    t, n, lanes = r.shape
    seq = pl.BlockSpec((tt, n, V7X_LANES), lambda l, i: (i, 0, l))
    st = pl.BlockSpec((n, n, V7X_LANES), lambda l, i: (0, 0, l))
    cst = pl.BlockSpec((n, V7X_LANES), lambda l, i: (0, l))
    return pl.pallas_call(
        functools.partial(_wkv_kernel, tt=tt),
        grid=(lanes // V7X_LANES, t // tt),
        in_specs=[seq] * 5 + [st] + [cst] * 5,
        out_specs=[seq, st],
        out_shape=[jax.ShapeDtypeStruct((t, n, lanes), F32),
                   jax.ShapeDtypeStruct((n, n, lanes), F32)],
        scratch_shapes=[pltpu.VMEM((tt, n, V7X_LANES), F32)] * 4,
        compiler_params=_cparams("arbitrary", "arbitrary"),
        name="wkv",
    )(r, w, k, v, a, s0, *consts)


def _wkv_tok_kernel(r_ref, w_ref, k_ref, v_ref, a_ref, s0_ref, kk_c, ka_c, rk_c, gg_c, gb_c,
                    y_ref, s_ref, ys_s):
    k = k_ref[0]
    a = a_ref[0]
    r = r_ref[0]
    w = w_ref[0]
    v = v_ref[0]
    kk = k * kk_c[0]
    nrm = jnp.sqrt(jnp.sum(kk * kk, axis=0, keepdims=True))
    kk = kk / jnp.maximum(nrm, 1e-12)
    nkk = -kk
    bvec = kk * a
    kp = k * (1.0 + (a - 1.0) * ka_c[0])
    wr = w * r
    br = jnp.sum(bvec * r, axis=0, keepdims=True)
    kr = jnp.sum(kp * r, axis=0, keepdims=True)
    for i in range(HEAD_SIZE):
        s = s0_ref[0, i]
        u = jnp.sum(s * nkk, axis=0, keepdims=True)
        yw = jnp.sum(s * wr, axis=0, keepdims=True)
        vi = v[i:i + 1, :]
        s_ref[0, i] = (s * w + u * bvec) + vi * kp
        ys_s[i:i + 1, :] = (yw + u * br) + vi * kr
    y = ys_s[...]
    mu = jnp.mean(y, axis=0, keepdims=True)
    yc = y - mu
    var = jnp.mean(yc * yc, axis=0, keepdims=True)
    yn = yc * lax.rsqrt(var + GN_EPS) * gg_c[0] + gb_c[0]
    bonus = jnp.sum(r * kp * rk_c[0], axis=0, keepdims=True) * v
    y_ref[0] = yn + bonus


def _wkv_tok(r, w, k, v, a, s0, consts):
    nh, n, bsz = r.shape
    vec = pl.BlockSpec((1, n, bsz), lambda h: (h, 0, 0))
    st = pl.BlockSpec((1, n, n, bsz), lambda h: (h, 0, 0, 0))
    return pl.pallas_call(
        _wkv_tok_kernel,
        grid=(nh,),
        in_specs=[vec] * 5 + [st] + [vec] * 5,
        out_specs=[vec, st],
        out_shape=[jax.ShapeDtypeStruct((nh, n, bsz), F32),
                   jax.ShapeDtypeStruct(s0.shape, F32)],
        scratch_shapes=[pltpu.VMEM((n, bsz), F32)],
        compiler_params=_cparams("arbitrary"),
        name="wkv_tok",
    )(r, w, k, v, a, s0, *consts)


def _to_heads_on_lanes(x):
    t, w = x.shape
    return jnp.swapaxes(x.reshape(t, w // HEAD_SIZE, HEAD_SIZE), 1, 2)


def _from_heads_on_lanes(y):
    t, n, lanes = y.shape
    return jnp.swapaxes(y, 1, 2).reshape(t, lanes * n)


def _head_const(x, bsz):
    return jnp.tile(x.reshape(B_HEADS, HEAD_SIZE).T, (1, bsz))


def _gelu(x):
    return jax.nn.gelu(x)


def _layernorm(x, g, b):
    mu = jnp.mean(x, axis=-1, keepdims=True)
    xc = x - mu
    var = jnp.mean(xc * xc, axis=-1, keepdims=True)
    return xc * lax.rsqrt(var + NORM_EPS) * g + b


def _branch_a_seq_kernel(u_ref, v_ref, z_ref, g_ref, lg, lb, ws_ref, bias_ref, pa_ref, o_ref, *, tm):
    u = _gelu(u_ref[0])
    v = _layernorm(_gelu(v_ref[0]), lg[...], lb[...])
    vb = v.astype(BF16)
    row = lax.broadcasted_iota(jnp.int32, (CHUNK, CHUNK), 0)
    col = lax.broadcasted_iota(jnp.int32, (CHUNK, CHUNK), 1)
    rows = []
    for c in range(tm // CHUNK):
        blocks = []
        for g in range(A_GROUPS):
            wm = jnp.where(row >= col, ws_ref[g], 0.0).astype(BF16)
            blk = vb[c * CHUNK:(c + 1) * CHUNK, g * CHUNK:(g + 1) * CHUNK]
            blocks.append(jnp.dot(wm, blk, preferred_element_type=F32))
        rows.append(jnp.concatenate(blocks, axis=1) + bias_ref[...])
    mix = jnp.concatenate(rows, axis=0)
    ya = u * mix * jax.nn.silu(z_ref[0])
    o_ref[0] = jax.nn.sigmoid(g_ref[0]) * jnp.dot(ya.astype(BF16), pa_ref[...],
                                                  preferred_element_type=F32)


def _branch_a_tok_kernel(u_ref, v_ref, z_ref, g_ref, lg, lb, w00, b00, pa_ref, o_ref, vo_ref):
    u = _gelu(u_ref[0])
    v = _layernorm(_gelu(v_ref[0]), lg[...], lb[...])
    vo_ref[...] = v
    mix = v * w00[...] + b00[...]
    ya = u * mix * jax.nn.silu(z_ref[0])
    o_ref[0] = jax.nn.sigmoid(g_ref[0]) * jnp.dot(ya.astype(BF16), pa_ref[...],
                                                  preferred_element_type=F32)


def _branch_a_seq(proj, ln_g, ln_b, w_s, bias2d, p_a, tm=256):
    bsz, t, _ = proj.shape
    full = lambda shape: pl.BlockSpec(shape, lambda b, i: (0,) * len(shape))
    sec = lambda w, c: pl.BlockSpec((1, tm, w), lambda b, i: (b, i, c))
    return pl.pallas_call(
        functools.partial(_branch_a_seq_kernel, tm=tm),
        grid=(bsz, t // tm),
        in_specs=[sec(A_W, COL_UA), sec(A_W, COL_VA), sec(A_W, COL_ZA), sec(D_MODEL, COL_GA),
                  full((1, A_W)), full((1, A_W)), full((A_GROUPS, CHUNK, CHUNK)),
                  full((CHUNK, A_W)), full((A_W, D_MODEL))],
        out_specs=pl.BlockSpec((1, tm, D_MODEL), lambda b, i: (b, i, 0)),
        out_shape=jax.ShapeDtypeStruct((bsz, t, D_MODEL), F32),
        compiler_params=_cparams("arbitrary", "arbitrary"),
        name="branch_a_seq",
    )(proj, proj, proj, proj, ln_g, ln_b, w_s, bias2d, p_a)


def _branch_a_tok(proj, ln_g, ln_b, w00, b00, p_a):
    _, m, _ = proj.shape
    full = lambda shape: pl.BlockSpec(shape, lambda i: (0,) * len(shape))
    sec = lambda w, c: pl.BlockSpec((1, m, w), lambda i: (0, 0, c))
    return pl.pallas_call(
        _branch_a_tok_kernel,
        grid=(1,),
        in_specs=[sec(A_W, COL_UA), sec(A_W, COL_VA), sec(A_W, COL_ZA), sec(D_MODEL, COL_GA),
                  full((1, A_W)), full((1, A_W)), full((1, A_W)), full((1, A_W)),
                  full((A_W, D_MODEL))],
        out_specs=[pl.BlockSpec((1, m, D_MODEL), lambda i: (0, 0, 0)),
                   pl.BlockSpec((m, A_W), lambda i: (0, 0))],
        out_shape=[jax.ShapeDtypeStruct((1, m, D_MODEL), F32),
                   jax.ShapeDtypeStruct((m, A_W), F32)],
        compiler_params=_cparams("arbitrary"),
        name="branch_a_tok",
    )(proj, proj, proj, proj, ln_g, ln_b, w00, b00, p_a)


def _merge_kernel(yb_ref, zb_ref, gb_ref, oa_ref, x_ref, cg_ref, pb_ref, wo_ref, fg_ref, o_ref):
    yb = (yb_ref[...].reshape(zb_ref.shape[1:]) * jax.nn.silu(zb_ref[0])).astype(BF16)
    merged = oa_ref[0] + jax.nn.sigmoid(gb_ref[0]) * jnp.dot(yb, pb_ref[...],
                                                            preferred_element_type=F32)
    out = x_ref[0] + cg_ref[0] * jnp.dot(merged.astype(BF16), wo_ref[...],
                                         preferred_element_type=F32)
    ms = jnp.mean(out * out, axis=-1, keepdims=True)
    o_ref[0] = out * lax.rsqrt(ms + NORM_EPS) * fg_ref[...]


def _merge(yb, proj, oa, x, c_gate, p_b, w_out, final_g, tm):
    bsz, t, d = x.shape
    tg = c_gate.shape[1]
    row = lambda: pl.BlockSpec((1, tm, d), lambda b, i: (b, i, 0))
    sec = lambda c: pl.BlockSpec((1, tm, d), lambda b, i: (b, i, c))
    gate = (pl.BlockSpec((1, 1, d), lambda b, i: (b, 0, 0)) if tg == 1
            else pl.BlockSpec((1, tm, d), lambda b, i: (b, i, 0)))
    wspec = lambda: pl.BlockSpec((d, d), lambda b, i: (0, 0), pipeline_mode=pl.Buffered(1))
    yb_spec = row() if yb.ndim == 3 else pl.BlockSpec((tm, d), lambda b, i: (i, b))
    return pl.pallas_call(
        _merge_kernel,
        grid=(bsz, t // tm),
        in_specs=[yb_spec, sec(COL_ZB), sec(COL_GB), row(), row(), gate, wspec(), wspec(),
                  pl.BlockSpec((1, d), lambda b, i: (0, 0))],
        out_specs=row(),
        out_shape=jax.ShapeDtypeStruct((bsz, t, d), F32),
        compiler_params=_cparams("arbitrary", "arbitrary"),
        name="merge",
    )(yb, proj, proj, oa, x, c_gate, p_b, w_out, final_g.reshape(1, d))


def _split(t, sizes):
    out, o = [], 0
    for s in sizes:
        out.append(t[..., o:o + s])
        o += s
    return out


def _permute_shift_cols(x):
    r, wd, k, v, ad = _split(x, (B_W, LORA, B_W, B_W, LORA))
    pad = jnp.zeros(x.shape[:-1] + (LORA_W - 2 * LORA,), x.dtype)
    return r, k, v, jnp.concatenate([wd, ad, pad], axis=-1)


def _shift_row(proj_row):
    r = proj_row[..., 0:B_W]
    k = proj_row[..., B_W:2 * B_W]
    v = proj_row[..., 2 * B_W:3 * B_W]
    lo = proj_row[..., COL_LORA * LORA_W:]
    return jnp.concatenate([r, lo[..., :LORA], k, v, lo[..., LORA:2 * LORA]], axis=-1)


def kernel(x_prompt, x_sample, c_prompt, c_sample, state_wkv, state_shift, norm_g, w_c, b_c, w_in, ln_v_g, ln_v_b, w_s, b_s, mu_shift, w0, w2, a0, a2, k_k, k_a, r_k, gn_g, gn_b, p_a, p_b, w_out, final_g):
    assert norm_g.shape[0] == 1, "single-layer trunk"
    bp, t, d = x_prompt.shape
    bs = x_sample.shape[0]

    w_rows = jnp.swapaxes(w_in[0], 0, 1)
    bounds = np.cumsum([0, A_W, A_W, A_W, B_W, LORA, B_W, B_W, LORA, B_W, D_MODEL, D_MODEL])
    ua, va, za, wr, wwd, wk, wv, wad, zb, ga, gb = (w_rows[a:b] for a, b in zip(bounds[:-1], bounds[1:]))
    wpad = jnp.zeros((LORA_W - 2 * LORA, d), F32)
    w_p = jnp.concatenate([wr, wk, wv, zb, ga, gb, ua, va, za, wwd, wad, wpad], axis=0).astype(BF16)
    mu = [m.reshape(1, -1) for m in _permute_shift_cols(mu_shift[0])]
    zrow = jnp.zeros((LORA_W - LORA, B_W), F32)
    w2p = jnp.concatenate([w2[0], zrow], axis=0).astype(BF16)
    a2p = jnp.concatenate([jnp.zeros((LORA, B_W), F32), a2[0],
                           jnp.zeros((LORA_W - 2 * LORA, B_W), F32)], axis=0).astype(BF16)
    w0r, a0r = w0[0].reshape(1, B_W), a0[0].reshape(1, B_W)
    p_a_b, p_b_b, w_out_b = p_a[0].astype(BF16), p_b[0].astype(BF16), w_out[0].astype(BF16)
    ln_g, ln_b = ln_v_g[0].reshape(1, A_W), ln_v_b[0].reshape(1, A_W)
    bias2d = jnp.repeat(b_s[0].T, CHUNK, axis=1)
    w00 = jnp.repeat(w_s[0][:, 0, 0], CHUNK).reshape(1, A_W)
    b00 = jnp.repeat(b_s[0][:, 0], CHUNK).reshape(1, A_W)

    m_rows = bp + bs
    m_pad = -m_rows % V7X_SUBLANES
    c_all = jnp.concatenate([c_prompt, c_sample, jnp.zeros((m_pad, d), F32)], axis=0)
    mod = _modulation(c_all, w_c[0], b_c[0])
    shift_p, scale_p, gate_p = (mod[:bp, i * d:(i + 1) * d].reshape(bp, 1, d) for i in range(3))
    shift_s, scale_s, gate_s = (mod[bp:m_rows, i * d:(i + 1) * d].reshape(1, bs, d) for i in range(3))

    def wkv_consts(bsz):
        return [_head_const(c, bsz) for c in (k_k[0], k_a[0], r_k[0], gn_g[0], gn_b[0])]

    h_p = _norm_modulate(x_prompt, norm_g[0], scale_p, shift_p, tm=512)
    proj_p = _in_projection(h_p, w_p)
    zeros_shift = [jnp.zeros((bp, 1, w), F32) for w in (B_W, B_W, B_W, LORA_W)]
    feats = _rwkv_prep_seq(proj_p, zeros_shift, mu, w0r, a0r, w2p, a2p)
    feats = [_to_heads_on_lanes(f) for f in feats]
    s0_p = jnp.zeros((HEAD_SIZE, HEAD_SIZE, bp * B_HEADS), F32)
    y_hl, s_hl = _wkv(*feats, s0_p, wkv_consts(bp), tt=64)
    yb_p = _from_heads_on_lanes(y_hl)
    oa_p = _branch_a_seq(proj_p, ln_g, ln_b, w_s[0], bias2d, p_a_b)
    y_prompt = _merge(yb_p, proj_p, oa_p, x_prompt, gate_p, p_b_b, w_out_b, final_g, tm=256)
    wkv_prompt = jnp.transpose(s_hl.reshape(HEAD_SIZE, HEAD_SIZE, bp, B_HEADS), (2, 3, 1, 0))[None]
    shift_prompt = _shift_row(proj_p[:, t - 1, :])[None]

    xs = x_sample.reshape(1, bs, d)
    h_s = _norm_modulate(xs, norm_g[0], scale_s, shift_s, tm=bs)
    proj_s = _in_projection(h_s, w_p)
    feats = _rwkv_prep_tok(proj_s, _permute_shift_cols(state_shift[0]), mu, w0r, a0r, w2p, a2p)
    feats = [jnp.transpose(f.reshape(bs, B_HEADS, HEAD_SIZE), (1, 2, 0)) for f in feats]
    tok_consts = [jnp.broadcast_to(c.reshape(B_HEADS, HEAD_SIZE, 1), (B_HEADS, HEAD_SIZE, bs))
                  for c in (k_k[0], k_a[0], r_k[0], gn_g[0], gn_b[0])]
    yb_s, wkv_s = _wkv_tok(*feats, jnp.transpose(state_wkv[0], (1, 2, 3, 0)), tok_consts)
    yb_s = jnp.transpose(yb_s, (2, 0, 1)).reshape(1, bs, B_W)
    oa_s, v_s = _branch_a_tok(proj_s, ln_g, ln_b, w00, b00, p_a_b)
    y_s = _merge(yb_s, proj_s, oa_s, xs, gate_s, p_b_b, w_out_b, final_g, tm=bs)
    y_sample = y_s.reshape(bs, 1, d)
    wkv_sample = jnp.transpose(wkv_s, (3, 0, 1, 2))[None]
    shift_sample = _shift_row(proj_s[0])[None]
    chunk_v_sample = v_s.reshape(1, bs, 1, A_W)

    return (y_prompt, y_sample, wkv_prompt, shift_prompt, wkv_sample, shift_sample, chunk_v_sample)
```

```python
import functools

import jax
import jax.numpy as jnp
import numpy as np
from jax import lax
from jax.experimental import pallas as pl
from jax.experimental.pallas import tpu as pltpu

F32 = jnp.float32
BF16 = jnp.bfloat16

D_MODEL = 2048
A_W = 1024
A_GROUPS = 8
CHUNK = 128
B_HEADS = 32
HEAD_SIZE = 64
B_W = B_HEADS * HEAD_SIZE
LORA = 96
NORM_EPS = 1e-6
GN_EPS = HEAD_SIZE * 1e-5

LORA_W = 512
PROJ_W = 6 * B_W + 3 * A_W + LORA_W
COL_R, COL_K, COL_V, COL_ZB, COL_GA, COL_GB = 0, 1, 2, 3, 4, 5
COL_UA, COL_VA, COL_ZA = 12, 13, 14
COL_LORA = (6 * B_W + 3 * A_W) // LORA_W

V7X_LANES = 128
V7X_SUBLANES = 8
LANE_GROUP = B_HEADS
VMEM_LIMIT = 56 * 1024 * 1024


def _cparams(*sem):
    return pltpu.CompilerParams(dimension_semantics=sem, vmem_limit_bytes=VMEM_LIMIT)


def _mod_kernel(c_ref, w_ref, b_ref, o_ref):
    acc = jnp.dot(c_ref[...].astype(BF16), w_ref[...].astype(BF16), preferred_element_type=F32)
    o_ref[...] = acc + b_ref[...]


def _modulation(c_all, w_c, b_c):
    m, d = c_all.shape
    n = w_c.shape[1]
    tn = 768
    return pl.pallas_call(
        _mod_kernel,
        grid=(n // tn,),
        in_specs=[pl.BlockSpec((m, d), lambda j: (0, 0)),
                  pl.BlockSpec((d, tn), lambda j: (0, j)),
                  pl.BlockSpec((1, tn), lambda j: (0, j))],
        out_specs=pl.BlockSpec((m, tn), lambda j: (0, j)),
        out_shape=jax.ShapeDtypeStruct((m, n), F32),
        compiler_params=_cparams("arbitrary"),
        name="modulation",
    )(c_all, w_c, b_c.reshape(1, n))


def _normmod_kernel(x_ref, g_ref, sc_ref, sh_ref, o_ref):
    x = x_ref[0]
    ms = jnp.mean(x * x, axis=-1, keepdims=True)
    y = x * lax.rsqrt(ms + NORM_EPS) * g_ref[...]
    o_ref[0] = (y * (1.0 + sc_ref[0]) + sh_ref[0]).astype(BF16)


def _norm_modulate(x, g, scale, shift, tm):
    bsz, t, d = x.shape
    tm = min(tm, t)
    ts = scale.shape[1]
    sspec = (pl.BlockSpec((1, 1, d), lambda b, i: (b, 0, 0)) if ts == 1
             else pl.BlockSpec((1, tm, d), lambda b, i: (b, i, 0)))
    return pl.pallas_call(
        _normmod_kernel,
        grid=(bsz, t // tm),
        in_specs=[pl.BlockSpec((1, tm, d), lambda b, i: (b, i, 0)),
                  pl.BlockSpec((1, d), lambda b, i: (0, 0)),
                  sspec, sspec],
        out_specs=pl.BlockSpec((1, tm, d), lambda b, i: (b, i, 0)),
        out_shape=jax.ShapeDtypeStruct((bsz, t, d), BF16),
        compiler_params=_cparams("arbitrary", "arbitrary"),
        name="norm_modulate",
    )(x, g.reshape(1, d), scale, shift)


def _inproj_kernel(h_ref, w_ref, o_ref):
    o_ref[0] = lax.dot_general(h_ref[0], w_ref[...], (((1,), (1,)), ((), ())),
                               preferred_element_type=F32)


def _in_projection(h, w_t, tn=512):
    bsz, t, d = h.shape
    n = w_t.shape[0]
    return pl.pallas_call(
        _inproj_kernel,
        grid=(bsz, n // tn),
        in_specs=[pl.BlockSpec((1, t, d), lambda b, j: (b, 0, 0)),
                  pl.BlockSpec((tn, d), lambda b, j: (j, 0))],
        out_specs=pl.BlockSpec((1, t, tn), lambda b, j: (b, 0, j)),
        out_shape=jax.ShapeDtypeStruct((bsz, t, n), F32),
        compiler_params=_cparams("arbitrary", "arbitrary"),
        name="in_projection",
    )(h, w_t)


def _lerp(p, q, mu):
    return p + mu * (q - p)


def _lora_logits(lo, w0, a0, w2p, a2p):
    col = lax.broadcasted_iota(jnp.int32, lo.shape, 1)
    lt = jnp.where(col < LORA, jnp.tanh(lo), lo).astype(BF16)
    wl = w0 + jnp.dot(lt, w2p, preferred_element_type=F32)
    al = a0 + jnp.dot(lt, a2p, preferred_element_type=F32)
    return wl, al


def _decay(wl):
    w_log = -jax.nn.softplus(-wl) - 0.5
    return jnp.exp(-jnp.exp(w_log))


def _prep_core(p, q, mu, w0, a0, w2p, a2p):
    r, k, v, lo = (_lerp(pi, qi, mi) for pi, qi, mi in zip(p, q, mu))
    wl, al = _lora_logits(lo, w0, a0, w2p, a2p)
    return r, _decay(wl), k, v, jax.nn.sigmoid(al)


def _swap_lane_groups(x):
    lane = lax.broadcasted_iota(jnp.int32, x[0].shape, 1)
    low_half = lane < 2 * LANE_GROUP
    even_group = (lane // LANE_GROUP) % 2 == 0
    rot = lambda v, s: pltpu.roll(v, s, axis=1)
    y0 = jnp.where(low_half, x[0], rot(x[2], 2 * LANE_GROUP))
    y2 = jnp.where(low_half, rot(x[0], 2 * LANE_GROUP), x[2])
    y1 = jnp.where(low_half, x[1], rot(x[3], 2 * LANE_GROUP))
    y3 = jnp.where(low_half, rot(x[1], 2 * LANE_GROUP), x[3])
    return [jnp.where(even_group, y0, rot(y1, LANE_GROUP)),
            jnp.where(even_group, rot(y0, 3 * LANE_GROUP), y1),
            jnp.where(even_group, y2, rot(y3, LANE_GROUP)),
            jnp.where(even_group, rot(y2, 3 * LANE_GROUP), y3)]


def _prep_seq_kernel(pr, pk, pv, plo, tr, tk, tv, tlo, sr, sk, sv, slo,
                     mr, mk, mv, mlo, w0, a0, w2p, a2p,
                     o_r, o_w, o_k, o_v, o_a, wl_s, al_s, *, nb):
    i = pl.program_id(0)
    tm = pr.shape[1]

    def shifted(p_ref, tail_ref, s_ref, mu_ref, b, cols):
        p = p_ref[b, :, cols]
        last = tail_ref[b, V7X_SUBLANES - 1:V7X_SUBLANES, cols]
        first = jnp.where(i == 0, s_ref[b, :, cols], last)
        row = lax.broadcasted_iota(jnp.int32, p.shape, 0)
        prev = jnp.where(row == 0, first, pltpu.roll(p, 1, axis=0))
        return _lerp(p, prev, mu_ref[:, cols])

    for b in range(nb):
        lo = shifted(plo, tlo, slo, mlo, b, slice(None))
        wl_s[b], al_s[b] = _lora_logits(lo, w0[...], a0[...], w2p[...], a2p[...])

    for q in range(B_W // V7X_LANES):
        cols = slice(q * V7X_LANES, (q + 1) * V7X_LANES)
        feats = [[shifted(pr, tr, sr, mr, b, cols) for b in range(nb)],
                 [_decay(wl_s[b, :, cols]) for b in range(nb)],
                 [shifted(pk, tk, sk, mk, b, cols) for b in range(nb)],
                 [shifted(pv, tv, sv, mv, b, cols) for b in range(nb)],
                 [jax.nn.sigmoid(al_s[b, :, cols]) for b in range(nb)]]
        for o, per_seq in zip((o_r, o_w, o_k, o_v, o_a), feats):
            out = _swap_lane_groups(per_seq)
            for g in range(nb):
                n = nb * q + g
                if o is o_v:
                    o[pl.ds(n, tm, stride=HEAD_SIZE), :] = out[g]
                else:
                    o[n] = out[g]


def _prep_tok_kernel(pr, pk, pv, plo, sr, sk, sv, slo,
                     mr, mk, mv, mlo, w0, a0, w2p, a2p,
                     o_r, o_w, o_k, o_v, o_a):
    outs = _prep_core((pr[0], pk[0], pv[0], plo[0]), (sr[...], sk[...], sv[...], slo[...]),
                      (mr[...], mk[...], mv[...], mlo[...]), w0[...], a0[...], w2p[...], a2p[...])
    for o, val in zip((o_r, o_w, o_k, o_v, o_a), outs):
        o[0] = val


def _rwkv_prep_seq(proj, shift0, mu, w0, a0, w2p, a2p, tm=32):
    bsz, t, _ = proj.shape
    assert bsz * B_HEADS == V7X_LANES, "one lane per (sequence, head)"
    tail = tm // V7X_SUBLANES
    cols = ((B_W, COL_R), (B_W, COL_K), (B_W, COL_V), (LORA_W, COL_LORA))
    cur = [pl.BlockSpec((bsz, tm, w), functools.partial(lambda i, c: (0, i, c), c=c)) for w, c in cols]
    tails = [pl.BlockSpec((bsz, V7X_SUBLANES, w),
                          functools.partial(lambda i, c: (0, jnp.maximum(i * tail - 1, 0), c), c=c))
             for w, c in cols]
    s0 = [pl.BlockSpec((bsz, 1, w), lambda i: (0, 0, 0)) for w, _ in cols]
    rowv = [pl.BlockSpec((1, w), lambda i: (0, 0)) for w, _ in cols]
    once = lambda shape: pl.BlockSpec(shape, lambda i: (0,) * len(shape), pipeline_mode=pl.Buffered(1))
    by_n = pl.BlockSpec((HEAD_SIZE, tm, V7X_LANES), lambda i: (0, i, 0))
    by_t = pl.BlockSpec((tm * HEAD_SIZE, V7X_LANES), lambda i: (i, 0))
    sds_n = jax.ShapeDtypeStruct((HEAD_SIZE, t, V7X_LANES), F32)
    sds_t = jax.ShapeDtypeStruct((t * HEAD_SIZE, V7X_LANES), F32)
    return pl.pallas_call(
        functools.partial(_prep_seq_kernel, nb=bsz),
        grid=(t // tm,),
        in_specs=cur + tails + s0 + rowv + [once((1, B_W)), once((1, B_W)),
                                            once((LORA_W, B_W)), once((LORA_W, B_W))],
        out_specs=[by_n, by_n, by_n, by_t, by_n],
        out_shape=[sds_n, sds_n, sds_n, sds_t, sds_n],
        scratch_shapes=[pltpu.VMEM((bsz, tm, B_W), F32)] * 2,
        compiler_params=_cparams("arbitrary"),
        name="rwkv_prep_seq",
    )(proj, proj, proj, proj, proj, proj, proj, proj, *shift0, *mu, w0, a0, w2p, a2p)


def _rwkv_prep_tok(proj, shift0, mu, w0, a0, w2p, a2p):
    _, m, _ = proj.shape
    cols = ((B_W, COL_R), (B_W, COL_K), (B_W, COL_V), (LORA_W, COL_LORA))
    cur = [pl.BlockSpec((1, m, w), functools.partial(lambda i, c: (0, 0, c), c=c)) for w, c in cols]
    s0 = [pl.BlockSpec((m, w), lambda i: (0, 0)) for w, _ in cols]
    rowv = [pl.BlockSpec((1, w), lambda i: (0, 0)) for w, _ in cols]
    full = lambda shape: pl.BlockSpec(shape, lambda i: (0,) * len(shape))
    out_spec = pl.BlockSpec((1, m, B_W), lambda i: (0, 0, 0))
    return pl.pallas_call(
        _prep_tok_kernel,
        grid=(1,),
        in_specs=cur + s0 + rowv + [full((1, B_W)), full((1, B_W)),
                                    full((LORA_W, B_W)), full((LORA_W, B_W))],
        out_specs=[out_spec] * 5,
        out_shape=[jax.ShapeDtypeStruct((1, m, B_W), F32)] * 5,
        compiler_params=_cparams("arbitrary"),
        name="rwkv_prep_tok",
    )(proj, proj, proj, proj, *shift0, *mu, w0, a0, w2p, a2p)


WKV_ROWS = 32


def _wkv_kernel(r_ref, w_ref, k_ref, v_ref, a_ref, s0_ref, kk_c, ka_c, rk_c, gg_c, gb_c,
                y_ref, s_ref, nkk_s, b_s, kp_s, rks_s, ys_s, bon_s, *, tt, nb):
    step = pl.program_id(0)

    @pl.when(step == 0)
    def _():
        s_ref[...] = s0_ref[...]

    def prep(c, carry):
        ts = pl.ds(pl.multiple_of(c * V7X_SUBLANES, V7X_SUBLANES), V7X_SUBLANES)
        k = k_ref[:, ts, :]
        a = a_ref[:, ts, :]
        kk = k * kk_c[...]
        nrm = jnp.sqrt(jnp.sum(kk * kk, axis=0))
        kk = kk / jnp.maximum(nrm, 1e-12)[None]
        nkk_s[:, ts, :] = -kk
        b_s[:, ts, :] = kk * a
        kp = k * (1.0 + (a - 1.0) * ka_c[...])
        kp_s[:, ts, :] = kp
        rks_s[ts, :] = jnp.sum(r_ref[:, ts, :] * kp * rk_c[...], axis=0)
        return carry

    lax.fori_loop(0, tt // V7X_SUBLANES, prep, 0)

    halves = [h * WKV_ROWS for h in range(HEAD_SIZE // WKV_ROWS)]

    def bcast(ref, t, j):
        return jnp.broadcast_to(ref[j, pl.ds(t, 1), :], (WKV_ROWS, V7X_LANES))

    def first_u(i0):
        acc = jnp.zeros((WKV_ROWS, V7X_LANES), F32)
        for j in range(HEAD_SIZE):
            acc = acc + s_ref[j, i0:i0 + WKV_ROWS, :] * bcast(nkk_s, 0, j)
        return acc

    def token(t, u):
        t_next = jnp.minimum(t + 1, tt - 1)
        bonus_scale = jnp.broadcast_to(rks_s[pl.ds(t, 1), :], (WKV_ROWS, V7X_LANES))
        u_next = []
        for i0, uh in zip(halves, u):
            v = v_ref[pl.ds(pl.multiple_of(t * HEAD_SIZE + i0, WKV_ROWS), WKV_ROWS), :]
            y_acc = jnp.zeros((WKV_ROWS, V7X_LANES), F32)
            u_acc = jnp.zeros((WKV_ROWS, V7X_LANES), F32)
            for j in range(HEAD_SIZE):
                sj = ((s_ref[j, i0:i0 + WKV_ROWS, :] * bcast(w_ref, t, j) + uh * bcast(b_s, t, j))
                      + v * bcast(kp_s, t, j))
                s_ref[j, i0:i0 + WKV_ROWS, :] = sj
                y_acc = y_acc + sj * bcast(r_ref, t, j)
                u_acc = u_acc + sj * bcast(nkk_s, t_next, j)
            rows = pl.ds(i0 * tt + t, WKV_ROWS, stride=tt)
            ys_s[rows, :] = y_acc
            bon_s[rows, :] = bonus_scale * v
            u_next.append(u_acc)
        return tuple(u_next)

    lax.fori_loop(0, tt, token, tuple(first_u(i0) for i0 in halves))

    def post(c, carry):
        t0 = pl.multiple_of(c * V7X_SUBLANES, V7X_SUBLANES)
        tile = lambda ref, i: ref[pl.ds(i * tt + t0, V7X_SUBLANES), :]
        ys = [tile(ys_s, i) for i in range(HEAD_SIZE)]
        mu = sum(ys[1:], ys[0]) * (1.0 / HEAD_SIZE)
        yc = [y - mu for y in ys]
        sq = [y * y for y in yc]
        rstd = lax.rsqrt(sum(sq[1:], sq[0]) * (1.0 / HEAD_SIZE) + GN_EPS)
        out = [yc[i] * rstd * gg_c[i:i + 1, :] + gb_c[i:i + 1, :] + tile(bon_s, i)
               for i in range(HEAD_SIZE)]
        for q in range(HEAD_SIZE // nb):
            nat = _swap_lane_groups(out[nb * q:nb * (q + 1)])
            for b in range(nb):
                y_ref[b, pl.ds(t0, V7X_SUBLANES), q * V7X_LANES:(q + 1) * V7X_LANES] = nat[b]
        return carry

    lax.fori_loop(0, tt // V7X_SUBLANES, post, 0, unroll=2)


def _wkv(r, w, k, v, a, s0, consts, tt, nb):
    n, t, lanes = r.shape
    assert lanes == V7X_LANES and nb * B_HEADS == lanes
    by_n = pl.BlockSpec((n, tt, lanes), lambda i: (0, i, 0))
    by_t = pl.BlockSpec((tt * n, lanes), lambda i: (i, 0))
    st = pl.BlockSpec((n, n, lanes), lambda i: (0, 0, 0))
    c3 = pl.BlockSpec((n, 1, lanes), lambda i: (0, 0, 0))
    c2 = pl.BlockSpec((n, lanes), lambda i: (0, 0))
    kk_c, ka_c, rk_c, gg_c, gb_c = consts
    col = lambda c: c.reshape(n, 1, lanes)
    return pl.pallas_call(
        functools.partial(_wkv_kernel, tt=tt, nb=nb),
        grid=(t // tt,),
        in_specs=[by_n, by_n, by_n, by_t, by_n, st, c3, c3, c3, c2, c2],
        out_specs=[pl.BlockSpec((nb, tt, B_W), lambda i: (0, i, 0)), st],
        out_shape=[jax.ShapeDtypeStruct((nb, t, B_W), F32),
                   jax.ShapeDtypeStruct((n, n, lanes), F32)],
        scratch_shapes=[pltpu.VMEM((n, tt, lanes), F32)] * 3
        + [pltpu.VMEM((tt, lanes), F32)] + [pltpu.VMEM((n * tt, lanes), F32)] * 2,
        compiler_params=_cparams("arbitrary"),
        name="wkv",
    )(r, w, k, v, a, s0, col(kk_c), col(ka_c), col(rk_c), gg_c, gb_c)


def _wkv_tok_kernel(r_ref, w_ref, k_ref, v_ref, a_ref, s0_ref, kk_c, ka_c, rk_c, gg_c, gb_c,
                    y_ref, s_ref, ys_s):
    k = k_ref[0]
    a = a_ref[0]
    r = r_ref[0]
    w = w_ref[0]
    v = v_ref[0]
    kk = k * kk_c[0]
    nrm = jnp.sqrt(jnp.sum(kk * kk, axis=0, keepdims=True))
    kk = kk / jnp.maximum(nrm, 1e-12)
    nkk = -kk
    bvec = kk * a
    kp = k * (1.0 + (a - 1.0) * ka_c[0])
    wr = w * r
    br = jnp.sum(bvec * r, axis=0, keepdims=True)
    kr = jnp.sum(kp * r, axis=0, keepdims=True)
    for i in range(HEAD_SIZE):
        s = s0_ref[0, i]
        u = jnp.sum(s * nkk, axis=0, keepdims=True)
        yw = jnp.sum(s * wr, axis=0, keepdims=True)
        vi = v[i:i + 1, :]
        s_ref[0, i] = (s * w + u * bvec) + vi * kp
        ys_s[i:i + 1, :] = (yw + u * br) + vi * kr
    y = ys_s[...]
    mu = jnp.mean(y, axis=0, keepdims=True)
    yc = y - mu
    var = jnp.mean(yc * yc, axis=0, keepdims=True)
    yn = yc * lax.rsqrt(var + GN_EPS) * gg_c[0] + gb_c[0]
    bonus = jnp.sum(r * kp * rk_c[0], axis=0, keepdims=True) * v
    y_ref[0] = yn + bonus


def _wkv_tok(r, w, k, v, a, s0, consts):
    nh, n, bsz = r.shape
    vec = pl.BlockSpec((1, n, bsz), lambda h: (h, 0, 0))
    st = pl.BlockSpec((1, n, n, bsz), lambda h: (h, 0, 0, 0))
    return pl.pallas_call(
        _wkv_tok_kernel,
        grid=(nh,),
        in_specs=[vec] * 5 + [st] + [vec] * 5,
        out_specs=[vec, st],
        out_shape=[jax.ShapeDtypeStruct((nh, n, bsz), F32),
                   jax.ShapeDtypeStruct(s0.shape, F32)],
        scratch_shapes=[pltpu.VMEM((n, bsz), F32)],
        compiler_params=_cparams("arbitrary"),
        name="wkv_tok",
    )(r, w, k, v, a, s0, *consts)


def _head_const(x, bsz):
    return jnp.tile(x.reshape(B_HEADS, HEAD_SIZE).T, (1, bsz))


def _gelu(x):
    return jax.nn.gelu(x)


def _layernorm(x, g, b):
    mu = jnp.mean(x, axis=-1, keepdims=True)
    xc = x - mu
    var = jnp.mean(xc * xc, axis=-1, keepdims=True)
    return xc * lax.rsqrt(var + NORM_EPS) * g + b


def _branch_a_seq_kernel(u_ref, v_ref, z_ref, g_ref, lg, lb, ws_ref, bias_ref, pa_ref, o_ref, *, tm):
    u = _gelu(u_ref[0])
    v = _layernorm(_gelu(v_ref[0]), lg[...], lb[...])
    vb = v.astype(BF16)
    row = lax.broadcasted_iota(jnp.int32, (CHUNK, CHUNK), 0)
    col = lax.broadcasted_iota(jnp.int32, (CHUNK, CHUNK), 1)
    rows = []
    for c in range(tm // CHUNK):
        blocks = []
        for g in range(A_GROUPS):
            wm = jnp.where(row >= col, ws_ref[g], 0.0).astype(BF16)
            blk = vb[c * CHUNK:(c + 1) * CHUNK, g * CHUNK:(g + 1) * CHUNK]
            blocks.append(jnp.dot(wm, blk, preferred_element_type=F32))
        rows.append(jnp.concatenate(blocks, axis=1) + bias_ref[...])
    mix = jnp.concatenate(rows, axis=0)
    ya = u * mix * jax.nn.silu(z_ref[0])
    o_ref[0] = jax.nn.sigmoid(g_ref[0]) * jnp.dot(ya.astype(BF16), pa_ref[...],
                                                  preferred_element_type=F32)


def _branch_a_tok_kernel(u_ref, v_ref, z_ref, g_ref, lg, lb, w00, b00, pa_ref, o_ref, vo_ref):
    u = _gelu(u_ref[0])
    v = _layernorm(_gelu(v_ref[0]), lg[...], lb[...])
    vo_ref[...] = v
    mix = v * w00[...] + b00[...]
    ya = u * mix * jax.nn.silu(z_ref[0])
    o_ref[0] = jax.nn.sigmoid(g_ref[0]) * jnp.dot(ya.astype(BF16), pa_ref[...],
                                                  preferred_element_type=F32)


def _branch_a_seq(proj, ln_g, ln_b, w_s, bias2d, p_a, tm=256):
    bsz, t, _ = proj.shape
    tm = min(tm, t)
    full = lambda shape: pl.BlockSpec(shape, lambda b, i: (0,) * len(shape))
    sec = lambda w, c: pl.BlockSpec((1, tm, w), lambda b, i: (b, i, c))
    return pl.pallas_call(
        functools.partial(_branch_a_seq_kernel, tm=tm),
        grid=(bsz, t // tm),
        in_specs=[sec(A_W, COL_UA), sec(A_W, COL_VA), sec(A_W, COL_ZA), sec(D_MODEL, COL_GA),
                  full((1, A_W)), full((1, A_W)), full((A_GROUPS, CHUNK, CHUNK)),
                  full((CHUNK, A_W)), full((A_W, D_MODEL))],
        out_specs=pl.BlockSpec((1, tm, D_MODEL), lambda b, i: (b, i, 0)),
        out_shape=jax.ShapeDtypeStruct((bsz, t, D_MODEL), F32),
        compiler_params=_cparams("arbitrary", "arbitrary"),
        name="branch_a_seq",
    )(proj, proj, proj, proj, ln_g, ln_b, w_s, bias2d, p_a)


def _branch_a_tok(proj, ln_g, ln_b, w00, b00, p_a):
    _, m, _ = proj.shape
    full = lambda shape: pl.BlockSpec(shape, lambda i: (0,) * len(shape))
    sec = lambda w, c: pl.BlockSpec((1, m, w), lambda i: (0, 0, c))
    return pl.pallas_call(
        _branch_a_tok_kernel,
        grid=(1,),
        in_specs=[sec(A_W, COL_UA), sec(A_W, COL_VA), sec(A_W, COL_ZA), sec(D_MODEL, COL_GA),
                  full((1, A_W)), full((1, A_W)), full((1, A_W)), full((1, A_W)),
                  full((A_W, D_MODEL))],
        out_specs=[pl.BlockSpec((1, m, D_MODEL), lambda i: (0, 0, 0)),
                   pl.BlockSpec((m, A_W), lambda i: (0, 0))],
        out_shape=[jax.ShapeDtypeStruct((1, m, D_MODEL), F32),
                   jax.ShapeDtypeStruct((m, A_W), F32)],
        compiler_params=_cparams("arbitrary"),
        name="branch_a_tok",
    )(proj, proj, proj, proj, ln_g, ln_b, w00, b00, p_a)


def _merge_kernel(yb_ref, zb_ref, gb_ref, oa_ref, x_ref, cg_ref, pb_ref, wo_ref, fg_ref, o_ref):
    yb = (yb_ref[0] * jax.nn.silu(zb_ref[0])).astype(BF16)
    merged = oa_ref[0] + jax.nn.sigmoid(gb_ref[0]) * jnp.dot(yb, pb_ref[...],
                                                            preferred_element_type=F32)
    out = x_ref[0] + cg_ref[0] * jnp.dot(merged.astype(BF16), wo_ref[...],
                                         preferred_element_type=F32)
    ms = jnp.mean(out * out, axis=-1, keepdims=True)
    o_ref[0] = out * lax.rsqrt(ms + NORM_EPS) * fg_ref[...]


def _merge(yb, proj, oa, x, c_gate, p_b, w_out, final_g, tm):
    bsz, t, d = x.shape
    tm = min(tm, t)
    tg = c_gate.shape[1]
    row = lambda: pl.BlockSpec((1, tm, d), lambda b, i: (b, i, 0))
    sec = lambda c: pl.BlockSpec((1, tm, d), lambda b, i: (b, i, c))
    gate = (pl.BlockSpec((1, 1, d), lambda b, i: (b, 0, 0)) if tg == 1
            else pl.BlockSpec((1, tm, d), lambda b, i: (b, i, 0)))
    wspec = lambda: pl.BlockSpec((d, d), lambda b, i: (0, 0), pipeline_mode=pl.Buffered(1))
    return pl.pallas_call(
        _merge_kernel,
        grid=(bsz, t // tm),
        in_specs=[row(), sec(COL_ZB), sec(COL_GB), row(), row(), gate, wspec(), wspec(),
                  pl.BlockSpec((1, d), lambda b, i: (0, 0))],
        out_specs=row(),
        out_shape=jax.ShapeDtypeStruct((bsz, t, d), F32),
        compiler_params=_cparams("arbitrary", "arbitrary"),
        name="merge",
    )(yb, proj, proj, oa, x, c_gate, p_b, w_out, final_g.reshape(1, d))


def _split(t, sizes):
    out, o = [], 0
    for s in sizes:
        out.append(t[..., o:o + s])
        o += s
    return out


def _permute_shift_cols(x):
    r, wd, k, v, ad = _split(x, (B_W, LORA, B_W, B_W, LORA))
    pad = jnp.zeros(x.shape[:-1] + (LORA_W - 2 * LORA,), x.dtype)
    return r, k, v, jnp.concatenate([wd, ad, pad], axis=-1)


def _nh_order(x, axis=-1):
    axis = axis % x.ndim
    shape = x.shape
    x = x.reshape(shape[:axis] + (B_HEADS, HEAD_SIZE) + shape[axis + 1:])
    return jnp.swapaxes(x, axis, axis + 1).reshape(shape)


def _hn_order(x, axis=-1):
    axis = axis % x.ndim
    shape = x.shape
    x = x.reshape(shape[:axis] + (HEAD_SIZE, B_HEADS) + shape[axis + 1:])
    return jnp.swapaxes(x, axis, axis + 1).reshape(shape)


def _shift_row(proj_row):
    r = _hn_order(proj_row[..., 0:B_W])
    k = _hn_order(proj_row[..., B_W:2 * B_W])
    v = _hn_order(proj_row[..., 2 * B_W:3 * B_W])
    lo = proj_row[..., COL_LORA * LORA_W:]
    return jnp.concatenate([r, lo[..., :LORA], k, v, lo[..., LORA:2 * LORA]], axis=-1)


def kernel(x_prompt, x_sample, c_prompt, c_sample, state_wkv, state_shift, norm_g, w_c, b_c, w_in, ln_v_g, ln_v_b, w_s, b_s, mu_shift, w0, w2, a0, a2, k_k, k_a, r_k, gn_g, gn_b, p_a, p_b, w_out, final_g):
    assert norm_g.shape[0] == 1, "single-layer trunk"
    bp, t, d = x_prompt.shape
    bs = x_sample.shape[0]

    w_rows = jnp.swapaxes(w_in[0], 0, 1)
    bounds = np.cumsum([0, A_W, A_W, A_W, B_W, LORA, B_W, B_W, LORA, B_W, D_MODEL, D_MODEL])
    ua, va, za, wr, wwd, wk, wv, wad, zb, ga, gb = (w_rows[a:b] for a, b in zip(bounds[:-1], bounds[1:]))
    wpad = jnp.zeros((LORA_W - 2 * LORA, d), F32)
    nh0 = lambda x: _nh_order(x, axis=0)
    w_p = jnp.concatenate([nh0(wr), nh0(wk), nh0(wv), nh0(zb), ga, gb, ua, va, za, wwd, wad, wpad],
                          axis=0).astype(BF16)

    def shift_cols(x):
        xr, xk, xv, xlo = _permute_shift_cols(x)
        return _nh_order(xr), _nh_order(xk), _nh_order(xv), xlo

    mu = [m.reshape(1, -1) for m in shift_cols(mu_shift[0])]
    zrow = jnp.zeros((LORA_W - LORA, B_W), F32)
    w2p = jnp.concatenate([_nh_order(w2[0]), zrow], axis=0).astype(BF16)
    a2p = jnp.concatenate([jnp.zeros((LORA, B_W), F32), _nh_order(a2[0]),
                           jnp.zeros((LORA_W - 2 * LORA, B_W), F32)], axis=0).astype(BF16)
    w0r, a0r = _nh_order(w0[0]).reshape(1, B_W), _nh_order(a0[0]).reshape(1, B_W)
    p_a_b, w_out_b = p_a[0].astype(BF16), w_out[0].astype(BF16)
    p_b_b = nh0(p_b[0]).astype(BF16)
    ln_g, ln_b = ln_v_g[0].reshape(1, A_W), ln_v_b[0].reshape(1, A_W)
    bias2d = jnp.repeat(b_s[0].T, CHUNK, axis=1)
    w00 = jnp.repeat(w_s[0][:, 0, 0], CHUNK).reshape(1, A_W)
    b00 = jnp.repeat(b_s[0][:, 0], CHUNK).reshape(1, A_W)

    m_rows = bp + bs
    m_pad = -m_rows % V7X_SUBLANES
    c_all = jnp.concatenate([c_prompt, c_sample, jnp.zeros((m_pad, d), F32)], axis=0)
    mod = _modulation(c_all, w_c[0], b_c[0])
    shift_p, scale_p, gate_p = (mod[:bp, i * d:(i + 1) * d].reshape(bp, 1, d) for i in range(3))
    shift_s, scale_s, gate_s = (mod[bp:m_rows, i * d:(i + 1) * d].reshape(1, bs, d) for i in range(3))

    def wkv_consts(bsz):
        return [_head_const(c, bsz) for c in (k_k[0], k_a[0], r_k[0], gn_g[0], gn_b[0])]

    h_p = _norm_modulate(x_prompt, norm_g[0], scale_p, shift_p, tm=512)
    proj_p = _in_projection(h_p, w_p)
    zeros_shift = [jnp.zeros((bp, 1, w), F32) for w in (B_W, B_W, B_W, LORA_W)]
    feats = _rwkv_prep_seq(proj_p, zeros_shift, mu, w0r, a0r, w2p, a2p)
    s0_p = jnp.zeros((HEAD_SIZE, HEAD_SIZE, bp * B_HEADS), F32)
    yb_p, s_hl = _wkv(*feats, s0_p, wkv_consts(bp), tt=64, nb=bp)
    oa_p = _branch_a_seq(proj_p, ln_g, ln_b, w_s[0], bias2d, p_a_b)
    y_prompt = _merge(yb_p, proj_p, oa_p, x_prompt, gate_p, p_b_b, w_out_b, final_g, tm=256)
    wkv_prompt = jnp.transpose(s_hl.reshape(HEAD_SIZE, HEAD_SIZE, bp, B_HEADS), (2, 3, 1, 0))[None]
    shift_prompt = _shift_row(proj_p[:, t - 1, :])[None]

    xs = x_sample.reshape(1, bs, d)
    h_s = _norm_modulate(xs, norm_g[0], scale_s, shift_s, tm=bs)
    proj_s = _in_projection(h_s, w_p)
    feats = _rwkv_prep_tok(proj_s, shift_cols(state_shift[0]), mu, w0r, a0r, w2p, a2p)
    feats = [jnp.transpose(f.reshape(bs, HEAD_SIZE, B_HEADS), (2, 1, 0)) for f in feats]
    tok_consts = [jnp.broadcast_to(c.reshape(B_HEADS, HEAD_SIZE, 1), (B_HEADS, HEAD_SIZE, bs))
                  for c in (k_k[0], k_a[0], r_k[0], gn_g[0], gn_b[0])]
    yb_s, wkv_s = _wkv_tok(*feats, jnp.transpose(state_wkv[0], (1, 2, 3, 0)), tok_consts)
    yb_s = jnp.transpose(yb_s, (2, 1, 0)).reshape(1, bs, B_W)
    oa_s, v_s = _branch_a_tok(proj_s, ln_g, ln_b, w00, b00, p_a_b)
    y_s = _merge(yb_s, proj_s, oa_s, xs, gate_s, p_b_b, w_out_b, final_g, tm=bs)
    y_sample = y_s.reshape(bs, 1, d)
    wkv_sample = jnp.transpose(wkv_s, (3, 0, 1, 2))[None]
    shift_sample = _shift_row(proj_s[0])[None]
    chunk_v_sample = v_s.reshape(1, bs, 1, A_W)

    return (y_prompt, y_sample, wkv_prompt, shift_prompt, wkv_sample, shift_sample, chunk_v_sample)
```

```python
import functools

import jax
import jax.numpy as jnp
import numpy as np
from jax import lax
from jax.experimental import pallas as pl
from jax.experimental.pallas import tpu as pltpu

F32 = jnp.float32
BF16 = jnp.bfloat16

D_MODEL = 2048
A_W = 1024
A_GROUPS = 8
CHUNK = 128
B_HEADS = 32
HEAD_SIZE = 64
B_W = B_HEADS * HEAD_SIZE
LORA = 96
NORM_EPS = 1e-6
GN_EPS = HEAD_SIZE * 1e-5

LORA_W = 512
PROJ_W = 6 * B_W + 3 * A_W + LORA_W
COL_R, COL_K, COL_V, COL_ZB, COL_GA, COL_GB = 0, 1, 2, 3, 4, 5
COL_UA, COL_VA, COL_ZA = 12, 13, 14
COL_LORA = (6 * B_W + 3 * A_W) // LORA_W

V7X_LANES = 128
V7X_SUBLANES = 8
LANE_GROUP = B_HEADS
VMEM_LIMIT = 56 * 1024 * 1024


def _cparams(*sem):
    return pltpu.CompilerParams(dimension_semantics=sem, vmem_limit_bytes=VMEM_LIMIT)


def _mod_kernel(c_ref, w_ref, b_ref, o_ref):
    acc = jnp.dot(c_ref[...].astype(BF16), w_ref[...].astype(BF16), preferred_element_type=F32)
    o_ref[...] = acc + b_ref[...]


def _modulation(c_all, w_c, b_c):
    m, d = c_all.shape
    n = w_c.shape[1]
    tn = 768
    return pl.pallas_call(
        _mod_kernel,
        grid=(n // tn,),
        in_specs=[pl.BlockSpec((m, d), lambda j: (0, 0)),
                  pl.BlockSpec((d, tn), lambda j: (0, j)),
                  pl.BlockSpec((1, tn), lambda j: (0, j))],
        out_specs=pl.BlockSpec((m, tn), lambda j: (0, j)),
        out_shape=jax.ShapeDtypeStruct((m, n), F32),
        compiler_params=_cparams("arbitrary"),
        name="modulation",
    )(c_all, w_c, b_c.reshape(1, n))


def _normmod_kernel(x_ref, g_ref, sc_ref, sh_ref, o_ref):
    x = x_ref[0]
    ms = jnp.mean(x * x, axis=-1, keepdims=True)
    y = x * lax.rsqrt(ms + NORM_EPS) * g_ref[...]
    o_ref[0] = (y * (1.0 + sc_ref[0]) + sh_ref[0]).astype(BF16)


def _norm_modulate(x, g, scale, shift, tm):
    bsz, t, d = x.shape
    tm = min(tm, t)
    ts = scale.shape[1]
    sspec = (pl.BlockSpec((1, 1, d), lambda b, i: (b, 0, 0)) if ts == 1
             else pl.BlockSpec((1, tm, d), lambda b, i: (b, i, 0)))
    return pl.pallas_call(
        _normmod_kernel,
        grid=(bsz, t // tm),
        in_specs=[pl.BlockSpec((1, tm, d), lambda b, i: (b, i, 0)),
                  pl.BlockSpec((1, d), lambda b, i: (0, 0)),
                  sspec, sspec],
        out_specs=pl.BlockSpec((1, tm, d), lambda b, i: (b, i, 0)),
        out_shape=jax.ShapeDtypeStruct((bsz, t, d), BF16),
        compiler_params=_cparams("arbitrary", "arbitrary"),
        name="norm_modulate",
    )(x, g.reshape(1, d), scale, shift)


def _inproj_kernel(h_ref, w_ref, o_ref):
    o_ref[0] = lax.dot_general(h_ref[0], w_ref[...], (((1,), (1,)), ((), ())),
                               preferred_element_type=F32)


def _in_projection(h, w_t, tn=512):
    bsz, t, d = h.shape
    n = w_t.shape[0]
    return pl.pallas_call(
        _inproj_kernel,
        grid=(bsz, n // tn),
        in_specs=[pl.BlockSpec((1, t, d), lambda b, j: (b, 0, 0)),
                  pl.BlockSpec((tn, d), lambda b, j: (j, 0))],
        out_specs=pl.BlockSpec((1, t, tn), lambda b, j: (b, 0, j)),
        out_shape=jax.ShapeDtypeStruct((bsz, t, n), F32),
        compiler_params=_cparams("arbitrary", "arbitrary"),
        name="in_projection",
    )(h, w_t)


def _lerp(p, q, mu):
    return p + mu * (q - p)


def _lora_logits(lo, w0, a0, w2p, a2p):
    col = lax.broadcasted_iota(jnp.int32, lo.shape, 1)
    lt = jnp.where(col < LORA, jnp.tanh(lo), lo).astype(BF16)
    wl = w0 + jnp.dot(lt, w2p, preferred_element_type=F32)
    al = a0 + jnp.dot(lt, a2p, preferred_element_type=F32)
    return wl, al


def _decay(wl):
    w_log = -jax.nn.softplus(-wl) - 0.5
    return jnp.exp(-jnp.exp(w_log))


def _prep_core(p, q, mu, w0, a0, w2p, a2p):
    r, k, v, lo = (_lerp(pi, qi, mi) for pi, qi, mi in zip(p, q, mu))
    wl, al = _lora_logits(lo, w0, a0, w2p, a2p)
    return r, _decay(wl), k, v, jax.nn.sigmoid(al)


def _move_lane_group(x, src_g, dst_g):
    shift = ((dst_g - src_g) % (V7X_LANES // LANE_GROUP)) * LANE_GROUP
    return x if shift == 0 else pltpu.roll(x, shift, axis=1)


def _swap_lane_groups(x):
    lane = lax.broadcasted_iota(jnp.int32, x[0].shape, 1)
    low_half = lane < 2 * LANE_GROUP
    even_group = (lane // LANE_GROUP) % 2 == 0
    rot = lambda v, s: pltpu.roll(v, s, axis=1)
    y0 = jnp.where(low_half, x[0], rot(x[2], 2 * LANE_GROUP))
    y2 = jnp.where(low_half, rot(x[0], 2 * LANE_GROUP), x[2])
    y1 = jnp.where(low_half, x[1], rot(x[3], 2 * LANE_GROUP))
    y3 = jnp.where(low_half, rot(x[1], 2 * LANE_GROUP), x[3])
    return [jnp.where(even_group, y0, rot(y1, LANE_GROUP)),
            jnp.where(even_group, rot(y0, 3 * LANE_GROUP), y1),
            jnp.where(even_group, y2, rot(y3, LANE_GROUP)),
            jnp.where(even_group, rot(y2, 3 * LANE_GROUP), y3)]


def _prep_seq_kernel(pr, pk, pv, plo, tr, tk, tv, tlo, sr, sk, sv, slo,
                     mr, mk, mv, mlo, w0, a0, w2p, a2p,
                     o_r, o_w, o_k, o_v, o_a):
    i = pl.program_id(1)

    def prev(p_ref, tail_ref, s_ref):
        p = p_ref[0]
        last = tail_ref[0][V7X_SUBLANES - 1:V7X_SUBLANES, :]
        first = jnp.where(i == 0, s_ref[0], last)
        rolled = pltpu.roll(p, 1, axis=0)
        row = lax.broadcasted_iota(jnp.int32, p.shape, 0)
        return p, jnp.where(row == 0, first, rolled)

    pq = [prev(x, y, z) for x, y, z in ((pr, tr, sr), (pk, tk, sk), (pv, tv, sv), (plo, tlo, slo))]
    outs = _prep_core([x[0] for x in pq], [x[1] for x in pq],
                      (mr[...], mk[...], mv[...], mlo[...]), w0[...], a0[...], w2p[...], a2p[...])
    for o, val in zip((o_r, o_w, o_k, o_v, o_a), outs):
        for q in range(B_W // V7X_LANES):
            o[0, q] = val[:, q * V7X_LANES:(q + 1) * V7X_LANES]


def _prep_tok_kernel(pr, pk, pv, plo, sr, sk, sv, slo,
                     mr, mk, mv, mlo, w0, a0, w2p, a2p,
                     o_r, o_w, o_k, o_v, o_a):
    outs = _prep_core((pr[0], pk[0], pv[0], plo[0]), (sr[...], sk[...], sv[...], slo[...]),
                      (mr[...], mk[...], mv[...], mlo[...]), w0[...], a0[...], w2p[...], a2p[...])
    for o, val in zip((o_r, o_w, o_k, o_v, o_a), outs):
        o[0] = val


def _rwkv_prep_seq(proj, shift0, mu, w0, a0, w2p, a2p, tm=128):
    bsz, t, _ = proj.shape
    tail = tm // V7X_SUBLANES
    nq = B_W // V7X_LANES
    cols = ((B_W, COL_R), (B_W, COL_K), (B_W, COL_V), (LORA_W, COL_LORA))
    cur = [pl.BlockSpec((1, tm, w), functools.partial(lambda b, i, c: (b, i, c), c=c)) for w, c in cols]
    tails = [pl.BlockSpec((1, V7X_SUBLANES, w),
                          functools.partial(lambda b, i, c: (b, jnp.maximum(i * tail - 1, 0), c), c=c))
             for w, c in cols]
    s0 = [pl.BlockSpec((1, 1, w), lambda b, i: (b, 0, 0)) for w, _ in cols]
    rowv = [pl.BlockSpec((1, w), lambda b, i: (0, 0)) for w, _ in cols]
    full = lambda shape: pl.BlockSpec(shape, lambda b, i: (0,) * len(shape))
    out_spec = pl.BlockSpec((1, nq, tm, V7X_LANES), lambda b, i: (b, 0, i, 0))
    return pl.pallas_call(
        _prep_seq_kernel,
        grid=(bsz, t // tm),
        in_specs=cur + tails + s0 + rowv + [full((1, B_W)), full((1, B_W)),
                                            full((LORA_W, B_W)), full((LORA_W, B_W))],
        out_specs=[out_spec] * 5,
        out_shape=[jax.ShapeDtypeStruct((bsz, nq, t, V7X_LANES), F32)] * 5,
        compiler_params=_cparams("arbitrary", "arbitrary"),
        name="rwkv_prep_seq",
    )(proj, proj, proj, proj, proj, proj, proj, proj, *shift0, *mu, w0, a0, w2p, a2p)


def _rwkv_prep_tok(proj, shift0, mu, w0, a0, w2p, a2p):
    _, m, _ = proj.shape
    cols = ((B_W, COL_R), (B_W, COL_K), (B_W, COL_V), (LORA_W, COL_LORA))
    cur = [pl.BlockSpec((1, m, w), functools.partial(lambda i, c: (0, 0, c), c=c)) for w, c in cols]
    s0 = [pl.BlockSpec((m, w), lambda i: (0, 0)) for w, _ in cols]
    rowv = [pl.BlockSpec((1, w), lambda i: (0, 0)) for w, _ in cols]
    full = lambda shape: pl.BlockSpec(shape, lambda i: (0,) * len(shape))
    out_spec = pl.BlockSpec((1, m, B_W), lambda i: (0, 0, 0))
    return pl.pallas_call(
        _prep_tok_kernel,
        grid=(1,),
        in_specs=cur + s0 + rowv + [full((1, B_W)), full((1, B_W)),
                                    full((LORA_W, B_W)), full((LORA_W, B_W))],
        out_specs=[out_spec] * 5,
        out_shape=[jax.ShapeDtypeStruct((1, m, B_W), F32)] * 5,
        compiler_params=_cparams("arbitrary"),
        name="rwkv_prep_tok",
    )(proj, proj, proj, proj, *shift0, *mu, w0, a0, w2p, a2p)


WKV_ROWS = 32


AHEAD = 2


def _wkv_kernel(xr, xw, xk, xv, xa, hr, hw, hk, hv, ha, s0_ref, kk_c, ka_c, rk_c, gg_c, gb_c,
                y_ref, s_ref, r_s, w_s, k_s, a_s, vn_s, vt_s, nkk_s, b_s, kp_s, rks_s, ys_s,
                *, tt, nb):
    step = pl.program_id(0)
    ng = tt // V7X_SUBLANES
    pitch = tt + V7X_SUBLANES
    vpitch = HEAD_SIZE + V7X_SUBLANES

    @pl.when(step == 0)
    def _():
        s_ref[...] = s0_ref[...]

    dsts = (r_s, w_s, k_s, a_s, vn_s)

    def exchange(srcs, src_rg, dst_rg, q):
        src_rows = pl.ds(pl.multiple_of(src_rg * V7X_SUBLANES, V7X_SUBLANES), V7X_SUBLANES)
        dst_rows = pl.ds(pl.multiple_of(dst_rg * V7X_SUBLANES, V7X_SUBLANES), V7X_SUBLANES)
        for x, dst in zip(srcs, dsts):
            for b in range(nb):
                tile = x[b, q, src_rows, :]
                lanes = slice(b * LANE_GROUP, (b + 1) * LANE_GROUP)
                for g in range(nb):
                    dst[nb * q + g, dst_rows, lanes] = _move_lane_group(tile, g, b)[:, lanes]

    def v_by_token(rg, q):
        rows = pl.ds(pl.multiple_of(rg * V7X_SUBLANES, V7X_SUBLANES), V7X_SUBLANES)
        for g in range(nb):
            n = nb * q + g
            vt_s[pl.ds(rg * V7X_SUBLANES * vpitch + n, V7X_SUBLANES, stride=vpitch), :] = vn_s[n, rows, :]

    def prep(m):
        ts = pl.ds(pl.multiple_of(m * V7X_SUBLANES, V7X_SUBLANES), V7X_SUBLANES)
        k = k_s[:, ts, :]
        a = a_s[:, ts, :]
        kk = k * kk_c[...]
        nrm = jnp.sqrt(jnp.sum(kk * kk, axis=0))
        kk = kk / jnp.maximum(nrm, 1e-12)[None]
        nkk_s[:, ts, :] = -kk
        b_s[:, ts, :] = kk * a
        kp = k * (1.0 + (a - 1.0) * ka_c[...])
        kp_s[:, ts, :] = kp
        rks_s[ts, :] = jnp.sum(r_s[:, ts, :] * kp * rk_c[...], axis=0)

    cur = (xr, xw, xk, xa, xv)
    nxt = (hr, hw, hk, ha, hv)

    @pl.when(step == 0)
    def _():
        for rg in range(AHEAD):
            for q in range(B_W // V7X_LANES):
                exchange(cur, rg, rg, q)
                v_by_token(rg, q)

    @pl.when(step > 0)
    def _():
        head = AHEAD * V7X_SUBLANES
        for dst in dsts:
            dst[:, 0:head, :] = dst[:, tt:tt + head, :]
        vt_s[0:head * vpitch, :] = vt_s[tt * vpitch:(tt + head) * vpitch, :]

    prep(0)

    halves = [h * WKV_ROWS for h in range(HEAD_SIZE // WKV_ROWS)]

    def bcast(ref, t, j):
        return jnp.broadcast_to(ref[j, pl.ds(t, 1), :], (WKV_ROWS, V7X_LANES))

    def first_u(i0):
        acc = jnp.zeros((WKV_ROWS, V7X_LANES), F32)
        for j in range(HEAD_SIZE):
            acc = acc + s_ref[j, i0:i0 + WKV_ROWS, :] * bcast(nkk_s, 0, j)
        return acc

    def group(m, u, srcs, src_rg):
        @pl.when(m + 1 < ng)
        def _():
            prep(m + 1)

        def token(tau, u):
            t = m * V7X_SUBLANES + tau
            t_next = jnp.minimum(t + 1, tt - 1)
            u_next = []
            for i0, uh in zip(halves, u):
                v = vt_s[pl.ds(pl.multiple_of(t * vpitch + i0, V7X_SUBLANES), WKV_ROWS), :]
                y_acc = jnp.zeros((WKV_ROWS, V7X_LANES), F32)
                u_acc = jnp.zeros((WKV_ROWS, V7X_LANES), F32)
                for j in range(HEAD_SIZE):
                    sj = ((s_ref[j, i0:i0 + WKV_ROWS, :] * bcast(w_s, t, j) + uh * bcast(b_s, t, j))
                          + v * bcast(kp_s, t, j))
                    s_ref[j, i0:i0 + WKV_ROWS, :] = sj
                    y_acc = y_acc + sj * bcast(r_s, t, j)
                    u_acc = u_acc + sj * bcast(nkk_s, t_next, j)
                ys_s[pl.ds(i0 * pitch + t, WKV_ROWS, stride=pitch), :] = y_acc
                u_next.append(u_acc)
            prev = t + V7X_SUBLANES - 1
            prev_rg = lax.shift_right_logical(prev, 3) - 1 + AHEAD
            prev_q = 2 * (prev & (V7X_SUBLANES - 1))
            for d in range(2):
                v_by_token(prev_rg, prev_q + d)
                exchange(srcs, src_rg, m + AHEAD, 2 * tau + d)
            return tuple(u_next)

        return lax.fori_loop(0, V7X_SUBLANES, token, u)

    u = tuple(first_u(i0) for i0 in halves)
    u = lax.fori_loop(0, ng - AHEAD, lambda m, u: group(m, u, cur, m + AHEAD), u)
    lax.fori_loop(ng - AHEAD, ng, lambda m, u: group(m, u, nxt, m + AHEAD - ng), u)
    for d in range(2):
        v_by_token(ng - 1 + AHEAD, B_W // V7X_LANES - 2 + d)

    def post(c, carry):
        t0 = pl.multiple_of(c * V7X_SUBLANES, V7X_SUBLANES)
        ys = [ys_s[pl.ds(i * pitch + t0, V7X_SUBLANES), :] for i in range(HEAD_SIZE)]
        mu = sum(ys[1:], ys[0]) * (1.0 / HEAD_SIZE)
        yc = [y - mu for y in ys]
        sq = [y * y for y in yc]
        rstd = lax.rsqrt(sum(sq[1:], sq[0]) * (1.0 / HEAD_SIZE) + GN_EPS)
        scale = rks_s[pl.ds(t0, V7X_SUBLANES), :]
        out = [yc[i] * rstd * gg_c[i:i + 1, :] + gb_c[i:i + 1, :]
               + scale * vn_s[i, pl.ds(t0, V7X_SUBLANES), :] for i in range(HEAD_SIZE)]
        for q in range(HEAD_SIZE // nb):
            nat = _swap_lane_groups(out[nb * q:nb * (q + 1)])
            for b in range(nb):
                y_ref[b, pl.ds(t0, V7X_SUBLANES), q * V7X_LANES:(q + 1) * V7X_LANES] = nat[b]
        return carry

    lax.fori_loop(0, ng, post, 0, unroll=2)


def _wkv(r, w, k, v, a, s0, consts, tt):
    nb, nq, t, lanes = r.shape
    n = HEAD_SIZE
    assert lanes == V7X_LANES and nb * B_HEADS == lanes and nq * lanes == B_W
    head = AHEAD * V7X_SUBLANES
    assert tt % head == 0 and tt // V7X_SUBLANES > AHEAD
    seq = pl.BlockSpec((nb, nq, tt, lanes), lambda i: (0, 0, i, 0))
    nxt = pl.BlockSpec((nb, nq, head, lanes),
                       lambda i: (0, 0, jnp.minimum((i + 1) * (tt // head), t // head - 1), 0))
    st = pl.BlockSpec((n, n, lanes), lambda i: (0, 0, 0))
    c3 = pl.BlockSpec((n, 1, lanes), lambda i: (0, 0, 0))
    c2 = pl.BlockSpec((n, lanes), lambda i: (0, 0))
    kk_c, ka_c, rk_c, gg_c, gb_c = consts
    col = lambda c: c.reshape(n, 1, lanes)
    pitch = tt + V7X_SUBLANES
    by_n = pltpu.VMEM((n, tt, lanes), F32)
    lead_n = pltpu.VMEM((n, tt + head, lanes), F32)
    return pl.pallas_call(
        functools.partial(_wkv_kernel, tt=tt, nb=nb),
        grid=(t // tt,),
        in_specs=[seq] * 5 + [nxt] * 5
        + [pl.BlockSpec((n, n, lanes), lambda i: (0, 0, 0), pipeline_mode=pl.Buffered(1)),
           c3, c3, c3, c2, c2],
        out_specs=[pl.BlockSpec((nb, tt, B_W), lambda i: (0, i, 0)), st],
        out_shape=[jax.ShapeDtypeStruct((nb, t, B_W), F32),
                   jax.ShapeDtypeStruct((n, n, lanes), F32)],
        scratch_shapes=[lead_n] * 5
        + [pltpu.VMEM(((tt + head) * (n + V7X_SUBLANES), lanes), F32)] + [by_n] * 3
        + [pltpu.VMEM((tt, lanes), F32), pltpu.VMEM((n * pitch, lanes), F32)],
        compiler_params=_cparams("arbitrary"),
        name="wkv",
    )(r, w, k, v, a, r, w, k, v, a, s0, col(kk_c), col(ka_c), col(rk_c), gg_c, gb_c)


def _wkv_tok_kernel(r_ref, w_ref, k_ref, v_ref, a_ref, s0_ref, kk_c, ka_c, rk_c, gg_c, gb_c,
                    y_ref, s_ref, ys_s):
    k = k_ref[0]
    a = a_ref[0]
    r = r_ref[0]
    w = w_ref[0]
    v = v_ref[0]
    kk = k * kk_c[0]
    nrm = jnp.sqrt(jnp.sum(kk * kk, axis=0, keepdims=True))
    kk = kk / jnp.maximum(nrm, 1e-12)
    nkk = -kk
    bvec = kk * a
    kp = k * (1.0 + (a - 1.0) * ka_c[0])
    wr = w * r
    br = jnp.sum(bvec * r, axis=0, keepdims=True)
    kr = jnp.sum(kp * r, axis=0, keepdims=True)
    for i in range(HEAD_SIZE):
        s = s0_ref[0, i]
        u = jnp.sum(s * nkk, axis=0, keepdims=True)
        yw = jnp.sum(s * wr, axis=0, keepdims=True)
        vi = v[i:i + 1, :]
        s_ref[0, i] = (s * w + u * bvec) + vi * kp
        ys_s[i:i + 1, :] = (yw + u * br) + vi * kr
    y = ys_s[...]
    mu = jnp.mean(y, axis=0, keepdims=True)
    yc = y - mu
    var = jnp.mean(yc * yc, axis=0, keepdims=True)
    yn = yc * lax.rsqrt(var + GN_EPS) * gg_c[0] + gb_c[0]
    bonus = jnp.sum(r * kp * rk_c[0], axis=0, keepdims=True) * v
    y_ref[0] = yn + bonus


def _wkv_tok(r, w, k, v, a, s0, consts):
    nh, n, bsz = r.shape
    vec = pl.BlockSpec((1, n, bsz), lambda h: (h, 0, 0))
    st = pl.BlockSpec((1, n, n, bsz), lambda h: (h, 0, 0, 0))
    return pl.pallas_call(
        _wkv_tok_kernel,
        grid=(nh,),
        in_specs=[vec] * 5 + [st] + [vec] * 5,
        out_specs=[vec, st],
        out_shape=[jax.ShapeDtypeStruct((nh, n, bsz), F32),
                   jax.ShapeDtypeStruct(s0.shape, F32)],
        scratch_shapes=[pltpu.VMEM((n, bsz), F32)],
        compiler_params=_cparams("arbitrary"),
        name="wkv_tok",
    )(r, w, k, v, a, s0, *consts)


def _head_const(x, bsz):
    return jnp.tile(x.reshape(B_HEADS, HEAD_SIZE).T, (1, bsz))


def _gelu(x):
    return jax.nn.gelu(x)


def _layernorm(x, g, b):
    mu = jnp.mean(x, axis=-1, keepdims=True)
    xc = x - mu
    var = jnp.mean(xc * xc, axis=-1, keepdims=True)
    return xc * lax.rsqrt(var + NORM_EPS) * g + b


def _branch_a_seq_kernel(u_ref, v_ref, z_ref, g_ref, lg, lb, ws_ref, bias_ref, pa_ref, o_ref, *, tm):
    u = _gelu(u_ref[0])
    v = _layernorm(_gelu(v_ref[0]), lg[...], lb[...])
    vb = v.astype(BF16)
    row = lax.broadcasted_iota(jnp.int32, (CHUNK, CHUNK), 0)
    col = lax.broadcasted_iota(jnp.int32, (CHUNK, CHUNK), 1)
    rows = []
    for c in range(tm // CHUNK):
        blocks = []
        for g in range(A_GROUPS):
            wm = jnp.where(row >= col, ws_ref[g], 0.0).astype(BF16)
            blk = vb[c * CHUNK:(c + 1) * CHUNK, g * CHUNK:(g + 1) * CHUNK]
            blocks.append(jnp.dot(wm, blk, preferred_element_type=F32))
        rows.append(jnp.concatenate(blocks, axis=1) + bias_ref[...])
    mix = jnp.concatenate(rows, axis=0)
    ya = u * mix * jax.nn.silu(z_ref[0])
    o_ref[0] = jax.nn.sigmoid(g_ref[0]) * jnp.dot(ya.astype(BF16), pa_ref[...],
                                                  preferred_element_type=F32)


def _branch_a_tok_kernel(u_ref, v_ref, z_ref, g_ref, lg, lb, w00, b00, pa_ref, o_ref, vo_ref):
    u = _gelu(u_ref[0])
    v = _layernorm(_gelu(v_ref[0]), lg[...], lb[...])
    vo_ref[...] = v
    mix = v * w00[...] + b00[...]
    ya = u * mix * jax.nn.silu(z_ref[0])
    o_ref[0] = jax.nn.sigmoid(g_ref[0]) * jnp.dot(ya.astype(BF16), pa_ref[...],
                                                  preferred_element_type=F32)


def _branch_a_seq(proj, ln_g, ln_b, w_s, bias2d, p_a, tm=256):
    bsz, t, _ = proj.shape
    tm = min(tm, t)
    full = lambda shape: pl.BlockSpec(shape, lambda b, i: (0,) * len(shape))
    sec = lambda w, c: pl.BlockSpec((1, tm, w), lambda b, i: (b, i, c))
    return pl.pallas_call(
        functools.partial(_branch_a_seq_kernel, tm=tm),
        grid=(bsz, t // tm),
        in_specs=[sec(A_W, COL_UA), sec(A_W, COL_VA), sec(A_W, COL_ZA), sec(D_MODEL, COL_GA),
                  full((1, A_W)), full((1, A_W)), full((A_GROUPS, CHUNK, CHUNK)),
                  full((CHUNK, A_W)), full((A_W, D_MODEL))],
        out_specs=pl.BlockSpec((1, tm, D_MODEL), lambda b, i: (b, i, 0)),
        out_shape=jax.ShapeDtypeStruct((bsz, t, D_MODEL), F32),
        compiler_params=_cparams("arbitrary", "arbitrary"),
        name="branch_a_seq",
    )(proj, proj, proj, proj, ln_g, ln_b, w_s, bias2d, p_a)


def _branch_a_tok(proj, ln_g, ln_b, w00, b00, p_a):
    _, m, _ = proj.shape
    full = lambda shape: pl.BlockSpec(shape, lambda i: (0,) * len(shape))
    sec = lambda w, c: pl.BlockSpec((1, m, w), lambda i: (0, 0, c))
    return pl.pallas_call(
        _branch_a_tok_kernel,
        grid=(1,),
        in_specs=[sec(A_W, COL_UA), sec(A_W, COL_VA), sec(A_W, COL_ZA), sec(D_MODEL, COL_GA),
                  full((1, A_W)), full((1, A_W)), full((1, A_W)), full((1, A_W)),
                  full((A_W, D_MODEL))],
        out_specs=[pl.BlockSpec((1, m, D_MODEL), lambda i: (0, 0, 0)),
                   pl.BlockSpec((m, A_W), lambda i: (0, 0))],
        out_shape=[jax.ShapeDtypeStruct((1, m, D_MODEL), F32),
                   jax.ShapeDtypeStruct((m, A_W), F32)],
        compiler_params=_cparams("arbitrary"),
        name="branch_a_tok",
    )(proj, proj, proj, proj, ln_g, ln_b, w00, b00, p_a)


def _merge_kernel(yb_ref, zb_ref, gb_ref, oa_ref, x_ref, cg_ref, pb_ref, wo_ref, fg_ref, o_ref):
    yb = (yb_ref[0] * jax.nn.silu(zb_ref[0])).astype(BF16)
    merged = oa_ref[0] + jax.nn.sigmoid(gb_ref[0]) * jnp.dot(yb, pb_ref[...],
                                                            preferred_element_type=F32)
    out = x_ref[0] + cg_ref[0] * jnp.dot(merged.astype(BF16), wo_ref[...],
                                         preferred_element_type=F32)
    ms = jnp.mean(out * out, axis=-1, keepdims=True)
    o_ref[0] = out * lax.rsqrt(ms + NORM_EPS) * fg_ref[...]


def _merge(yb, proj, oa, x, c_gate, p_b, w_out, final_g, tm):
    bsz, t, d = x.shape
    tm = min(tm, t)
    tg = c_gate.shape[1]
    row = lambda: pl.BlockSpec((1, tm, d), lambda b, i: (b, i, 0))
    sec = lambda c: pl.BlockSpec((1, tm, d), lambda b, i: (b, i, c))
    gate = (pl.BlockSpec((1, 1, d), lambda b, i: (b, 0, 0)) if tg == 1
            else pl.BlockSpec((1, tm, d), lambda b, i: (b, i, 0)))
    wspec = lambda: pl.BlockSpec((d, d), lambda b, i: (0, 0), pipeline_mode=pl.Buffered(1))
    return pl.pallas_call(
        _merge_kernel,
        grid=(bsz, t // tm),
        in_specs=[row(), sec(COL_ZB), sec(COL_GB), row(), row(), gate, wspec(), wspec(),
                  pl.BlockSpec((1, d), lambda b, i: (0, 0))],
        out_specs=row(),
        out_shape=jax.ShapeDtypeStruct((bsz, t, d), F32),
        compiler_params=_cparams("arbitrary", "arbitrary"),
        name="merge",
    )(yb, proj, proj, oa, x, c_gate, p_b, w_out, final_g.reshape(1, d))


def _split(t, sizes):
    out, o = [], 0
    for s in sizes:
        out.append(t[..., o:o + s])
        o += s
    return out


def _permute_shift_cols(x):
    r, wd, k, v, ad = _split(x, (B_W, LORA, B_W, B_W, LORA))
    pad = jnp.zeros(x.shape[:-1] + (LORA_W - 2 * LORA,), x.dtype)
    return r, k, v, jnp.concatenate([wd, ad, pad], axis=-1)


def _nh_order(x, axis=-1):
    axis = axis % x.ndim
    shape = x.shape
    x = x.reshape(shape[:axis] + (B_HEADS, HEAD_SIZE) + shape[axis + 1:])
    return jnp.swapaxes(x, axis, axis + 1).reshape(shape)


def _hn_order(x, axis=-1):
    axis = axis % x.ndim
    shape = x.shape
    x = x.reshape(shape[:axis] + (HEAD_SIZE, B_HEADS) + shape[axis + 1:])
    return jnp.swapaxes(x, axis, axis + 1).reshape(shape)


def _shift_row(proj_row):
    r = _hn_order(proj_row[..., 0:B_W])
    k = _hn_order(proj_row[..., B_W:2 * B_W])
    v = _hn_order(proj_row[..., 2 * B_W:3 * B_W])
    lo = proj_row[..., COL_LORA * LORA_W:]
    return jnp.concatenate([r, lo[..., :LORA], k, v, lo[..., LORA:2 * LORA]], axis=-1)


def kernel(x_prompt, x_sample, c_prompt, c_sample, state_wkv, state_shift, norm_g, w_c, b_c, w_in, ln_v_g, ln_v_b, w_s, b_s, mu_shift, w0, w2, a0, a2, k_k, k_a, r_k, gn_g, gn_b, p_a, p_b, w_out, final_g):
    assert norm_g.shape[0] == 1, "single-layer trunk"
    bp, t, d = x_prompt.shape
    bs = x_sample.shape[0]

    w_rows = jnp.swapaxes(w_in[0], 0, 1)
    bounds = np.cumsum([0, A_W, A_W, A_W, B_W, LORA, B_W, B_W, LORA, B_W, D_MODEL, D_MODEL])
    ua, va, za, wr, wwd, wk, wv, wad, zb, ga, gb = (w_rows[a:b] for a, b in zip(bounds[:-1], bounds[1:]))
    wpad = jnp.zeros((LORA_W - 2 * LORA, d), F32)
    nh0 = lambda x: _nh_order(x, axis=0)
    w_p = jnp.concatenate([nh0(wr), nh0(wk), nh0(wv), nh0(zb), ga, gb, ua, va, za, wwd, wad, wpad],
                          axis=0).astype(BF16)

    def shift_cols(x):
        xr, xk, xv, xlo = _permute_shift_cols(x)
        return _nh_order(xr), _nh_order(xk), _nh_order(xv), xlo

    mu = [m.reshape(1, -1) for m in shift_cols(mu_shift[0])]
    zrow = jnp.zeros((LORA_W - LORA, B_W), F32)
    w2p = jnp.concatenate([_nh_order(w2[0]), zrow], axis=0).astype(BF16)
    a2p = jnp.concatenate([jnp.zeros((LORA, B_W), F32), _nh_order(a2[0]),
                           jnp.zeros((LORA_W - 2 * LORA, B_W), F32)], axis=0).astype(BF16)
    w0r, a0r = _nh_order(w0[0]).reshape(1, B_W), _nh_order(a0[0]).reshape(1, B_W)
    p_a_b, w_out_b = p_a[0].astype(BF16), w_out[0].astype(BF16)
    p_b_b = nh0(p_b[0]).astype(BF16)
    ln_g, ln_b = ln_v_g[0].reshape(1, A_W), ln_v_b[0].reshape(1, A_W)
    bias2d = jnp.repeat(b_s[0].T, CHUNK, axis=1)
    w00 = jnp.repeat(w_s[0][:, 0, 0], CHUNK).reshape(1, A_W)
    b00 = jnp.repeat(b_s[0][:, 0], CHUNK).reshape(1, A_W)

    m_rows = bp + bs
    m_pad = -m_rows % V7X_SUBLANES
    c_all = jnp.concatenate([c_prompt, c_sample, jnp.zeros((m_pad, d), F32)], axis=0)
    mod = _modulation(c_all, w_c[0], b_c[0])
    shift_p, scale_p, gate_p = (mod[:bp, i * d:(i + 1) * d].reshape(bp, 1, d) for i in range(3))
    shift_s, scale_s, gate_s = (mod[bp:m_rows, i * d:(i + 1) * d].reshape(1, bs, d) for i in range(3))

    def wkv_consts(bsz):
        return [_head_const(c, bsz) for c in (k_k[0], k_a[0], r_k[0], gn_g[0], gn_b[0])]

    h_p = _norm_modulate(x_prompt, norm_g[0], scale_p, shift_p, tm=512)
    proj_p = _in_projection(h_p, w_p)
    zeros_shift = [jnp.zeros((bp, 1, w), F32) for w in (B_W, B_W, B_W, LORA_W)]
    feats = _rwkv_prep_seq(proj_p, zeros_shift, mu, w0r, a0r, w2p, a2p)
    s0_p = jnp.zeros((HEAD_SIZE, HEAD_SIZE, bp * B_HEADS), F32)
    yb_p, s_hl = _wkv(*feats, s0_p, wkv_consts(bp), tt=32)
    oa_p = _branch_a_seq(proj_p, ln_g, ln_b, w_s[0], bias2d, p_a_b)
    y_prompt = _merge(yb_p, proj_p, oa_p, x_prompt, gate_p, p_b_b, w_out_b, final_g, tm=256)
    wkv_prompt = jnp.transpose(s_hl.reshape(HEAD_SIZE, HEAD_SIZE, bp, B_HEADS), (2, 3, 1, 0))[None]
    shift_prompt = _shift_row(proj_p[:, t - 1, :])[None]

    xs = x_sample.reshape(1, bs, d)
    h_s = _norm_modulate(xs, norm_g[0], scale_s, shift_s, tm=bs)
    proj_s = _in_projection(h_s, w_p)
    feats = _rwkv_prep_tok(proj_s, shift_cols(state_shift[0]), mu, w0r, a0r, w2p, a2p)
    feats = [jnp.transpose(f.reshape(bs, HEAD_SIZE, B_HEADS), (2, 1, 0)) for f in feats]
    tok_consts = [jnp.broadcast_to(c.reshape(B_HEADS, HEAD_SIZE, 1), (B_HEADS, HEAD_SIZE, bs))
                  for c in (k_k[0], k_a[0], r_k[0], gn_g[0], gn_b[0])]
    yb_s, wkv_s = _wkv_tok(*feats, jnp.transpose(state_wkv[0], (1, 2, 3, 0)), tok_consts)
    yb_s = jnp.transpose(yb_s, (2, 1, 0)).reshape(1, bs, B_W)
    oa_s, v_s = _branch_a_tok(proj_s, ln_g, ln_b, w00, b00, p_a_b)
    y_s = _merge(yb_s, proj_s, oa_s, xs, gate_s, p_b_b, w_out_b, final_g, tm=bs)
    y_sample = y_s.reshape(bs, 1, d)
    wkv_sample = jnp.transpose(wkv_s, (3, 0, 1, 2))[None]
    shift_sample = _shift_row(proj_s[0])[None]
    chunk_v_sample = v_s.reshape(1, bs, 1, A_W)

    return (y_prompt, y_sample, wkv_prompt, shift_prompt, wkv_sample, shift_sample, chunk_v_sample)
```

```python
import functools
import math

import jax
import jax.numpy as jnp
import numpy as np
from jax import lax
from jax.experimental import pallas as pl
from jax.experimental.pallas import tpu as pltpu

F32 = jnp.float32
BF16 = jnp.bfloat16

D_MODEL = 2048
A_W = 1024
A_GROUPS = 8
CHUNK = 128
B_HEADS = 32
HEAD_SIZE = 64
B_W = B_HEADS * HEAD_SIZE
LORA = 96
NORM_EPS = 1e-6
GN_EPS = HEAD_SIZE * 1e-5
DECAY_SCALE = math.exp(-0.5)

LORA_W = 512
PROJ_W = 6 * B_W + 3 * A_W + LORA_W
COL_R, COL_K, COL_V, COL_ZB, COL_GA, COL_GB = 0, 1, 2, 3, 4, 5
COL_UA, COL_VA, COL_ZA = 12, 13, 14
COL_LORA = (6 * B_W + 3 * A_W) // LORA_W

V7X_LANES = 128
V7X_SUBLANES = 8
LANE_GROUP = B_HEADS
VMEM_LIMIT = 56 * 1024 * 1024


def _cparams(*sem):
    return pltpu.CompilerParams(dimension_semantics=sem, vmem_limit_bytes=VMEM_LIMIT)


def _mod_kernel(c_ref, w_ref, b_ref, o_ref):
    acc = jnp.dot(c_ref[...].astype(BF16), w_ref[...].astype(BF16), preferred_element_type=F32)
    o_ref[...] = acc + b_ref[...]


def _modulation(c_all, w_c, b_c):
    m, d = c_all.shape
    n = w_c.shape[1]
    tn = 768
    return pl.pallas_call(
        _mod_kernel,
        grid=(n // tn,),
        in_specs=[pl.BlockSpec((m, d), lambda j: (0, 0)),
                  pl.BlockSpec((d, tn), lambda j: (0, j)),
                  pl.BlockSpec((1, tn), lambda j: (0, j))],
        out_specs=pl.BlockSpec((m, tn), lambda j: (0, j)),
        out_shape=jax.ShapeDtypeStruct((m, n), F32),
        compiler_params=_cparams("arbitrary"),
        name="modulation",
    )(c_all, w_c, b_c.reshape(1, n))


def _normmod_kernel(x_ref, g_ref, sc_ref, sh_ref, o_ref):
    x = x_ref[0]
    ms = jnp.mean(x * x, axis=-1, keepdims=True)
    y = x * lax.rsqrt(ms + NORM_EPS) * g_ref[...]
    o_ref[0] = (y * (1.0 + sc_ref[0]) + sh_ref[0]).astype(BF16)


def _norm_modulate(x, g, scale, shift, tm):
    bsz, t, d = x.shape
    tm = min(tm, t)
    ts = scale.shape[1]
    sspec = (pl.BlockSpec((1, 1, d), lambda b, i: (b, 0, 0)) if ts == 1
             else pl.BlockSpec((1, tm, d), lambda b, i: (b, i, 0)))
    return pl.pallas_call(
        _normmod_kernel,
        grid=(bsz, t // tm),
        in_specs=[pl.BlockSpec((1, tm, d), lambda b, i: (b, i, 0)),
                  pl.BlockSpec((1, d), lambda b, i: (0, 0)),
                  sspec, sspec],
        out_specs=pl.BlockSpec((1, tm, d), lambda b, i: (b, i, 0)),
        out_shape=jax.ShapeDtypeStruct((bsz, t, d), BF16),
        compiler_params=_cparams("arbitrary", "arbitrary"),
        name="norm_modulate",
    )(x, g.reshape(1, d), scale, shift)


def _inproj_kernel(h_ref, hs_ref, w_ref, o_ref, os_ref):
    nt = (((1,), (1,)), ((), ()))
    o_ref[0] = lax.dot_general(h_ref[0], w_ref[...], nt, preferred_element_type=F32)

    @pl.when(pl.program_id(0) == 0)
    def _():
        os_ref[...] = lax.dot_general(hs_ref[...], w_ref[...], nt, preferred_element_type=F32)


def _in_projection(h, h_tok, w_t, tn=512):
    bsz, t, d = h.shape
    m = h_tok.shape[0]
    n = w_t.shape[0]
    nj = n // tn
    tok_out = pl.BlockSpec((m, tn), lambda b, j: (0, jnp.where(b == 0, j, nj - 1)))
    return pl.pallas_call(
        _inproj_kernel,
        grid=(bsz, nj),
        in_specs=[pl.BlockSpec((1, t, d), lambda b, j: (b, 0, 0)),
                  pl.BlockSpec((m, d), lambda b, j: (0, 0)),
                  pl.BlockSpec((tn, d), lambda b, j: (j, 0))],
        out_specs=[pl.BlockSpec((1, t, tn), lambda b, j: (b, 0, j)), tok_out],
        out_shape=[jax.ShapeDtypeStruct((bsz, t, n), F32), jax.ShapeDtypeStruct((m, n), F32)],
        compiler_params=_cparams("arbitrary", "arbitrary"),
        name="in_projection",
    )(h, h_tok, w_t)


def _lerp(p, q, mu):
    return p + mu * (q - p)


def _lora_logits(lo, w0, a0, w2p, a2p):
    col = lax.broadcasted_iota(jnp.int32, lo.shape, 1)
    lt = jnp.where(col < LORA, jnp.tanh(lo), lo).astype(BF16)
    wl = w0 + jnp.dot(lt, w2p, preferred_element_type=F32)
    al = a0 + jnp.dot(lt, a2p, preferred_element_type=F32)
    return wl, al


def _decay(wl):
    return jnp.exp(-DECAY_SCALE * jax.nn.sigmoid(wl))


def _prep_core(p, q, mu, w0, a0, w2p, a2p):
    r, k, v, lo = (_lerp(pi, qi, mi) for pi, qi, mi in zip(p, q, mu))
    wl, al = _lora_logits(lo, w0, a0, w2p, a2p)
    return r, _decay(wl), k, v, jax.nn.sigmoid(al)


def _move_lane_group(x, src_g, dst_g):
    shift = ((dst_g - src_g) % (V7X_LANES // LANE_GROUP)) * LANE_GROUP
    return x if shift == 0 else pltpu.roll(x, shift, axis=1)


def _swap_lane_groups(x):
    lane = lax.broadcasted_iota(jnp.int32, x[0].shape, 1)
    low_half = lane < 2 * LANE_GROUP
    even_group = (lane // LANE_GROUP) % 2 == 0
    rot = lambda v, s: pltpu.roll(v, s, axis=1)
    y0 = jnp.where(low_half, x[0], rot(x[2], 2 * LANE_GROUP))
    y2 = jnp.where(low_half, rot(x[0], 2 * LANE_GROUP), x[2])
    y1 = jnp.where(low_half, x[1], rot(x[3], 2 * LANE_GROUP))
    y3 = jnp.where(low_half, rot(x[1], 2 * LANE_GROUP), x[3])
    return [jnp.where(even_group, y0, rot(y1, LANE_GROUP)),
            jnp.where(even_group, rot(y0, 3 * LANE_GROUP), y1),
            jnp.where(even_group, y2, rot(y3, LANE_GROUP)),
            jnp.where(even_group, rot(y2, 3 * LANE_GROUP), y3)]


def _prep_seq_kernel(pr, pk, pv, plo, tr, tk, tv, tlo, sr, sk, sv, slo,
                     mr, mk, mv, mlo, w0, a0, w2p, a2p,
                     o_r, o_w, o_k, o_v, o_a):
    i = pl.program_id(1)

    def prev(p_ref, tail_ref, s_ref):
        p = p_ref[0]
        last = tail_ref[0][V7X_SUBLANES - 1:V7X_SUBLANES, :]
        first = jnp.where(i == 0, s_ref[0], last)
        rolled = pltpu.roll(p, 1, axis=0)
        row = lax.broadcasted_iota(jnp.int32, p.shape, 0)
        return p, jnp.where(row == 0, first, rolled)

    pq = [prev(x, y, z) for x, y, z in ((pr, tr, sr), (pk, tk, sk), (pv, tv, sv), (plo, tlo, slo))]
    outs = _prep_core([x[0] for x in pq], [x[1] for x in pq],
                      (mr[...], mk[...], mv[...], mlo[...]), w0[...], a0[...], w2p[...], a2p[...])
    for o, val in zip((o_r, o_w, o_k, o_v, o_a), outs):
        for q in range(B_W // V7X_LANES):
            o[0, q] = val[:, q * V7X_LANES:(q + 1) * V7X_LANES]


def _prep_tok_kernel(pr, pk, pv, plo, sr, sk, sv, slo,
                     mr, mk, mv, mlo, w0, a0, w2p, a2p,
                     o_r, o_w, o_k, o_v, o_a):
    outs = _prep_core((pr[0], pk[0], pv[0], plo[0]), (sr[...], sk[...], sv[...], slo[...]),
                      (mr[...], mk[...], mv[...], mlo[...]), w0[...], a0[...], w2p[...], a2p[...])
    for o, val in zip((o_r, o_w, o_k, o_v, o_a), outs):
        o[0] = val


def _rwkv_prep_seq(proj, shift0, mu, w0, a0, w2p, a2p, tm=128):
    bsz, t, _ = proj.shape
    tail = tm // V7X_SUBLANES
    nq = B_W // V7X_LANES
    cols = ((B_W, COL_R), (B_W, COL_K), (B_W, COL_V), (LORA_W, COL_LORA))
    cur = [pl.BlockSpec((1, tm, w), functools.partial(lambda b, i, c: (b, i, c), c=c)) for w, c in cols]
    tails = [pl.BlockSpec((1, V7X_SUBLANES, w),
                          functools.partial(lambda b, i, c: (b, jnp.maximum(i * tail - 1, 0), c), c=c))
             for w, c in cols]
    s0 = [pl.BlockSpec((1, 1, w), lambda b, i: (b, 0, 0)) for w, _ in cols]
    rowv = [pl.BlockSpec((1, w), lambda b, i: (0, 0)) for w, _ in cols]
    full = lambda shape: pl.BlockSpec(shape, lambda b, i: (0,) * len(shape))
    out_spec = pl.BlockSpec((1, nq, tm, V7X_LANES), lambda b, i: (b, 0, i, 0))
    return pl.pallas_call(
        _prep_seq_kernel,
        grid=(bsz, t // tm),
        in_specs=cur + tails + s0 + rowv + [full((1, B_W)), full((1, B_W)),
                                            full((LORA_W, B_W)), full((LORA_W, B_W))],
        out_specs=[out_spec] * 5,
        out_shape=[jax.ShapeDtypeStruct((bsz, nq, t, V7X_LANES), F32)] * 5,
        compiler_params=_cparams("arbitrary", "arbitrary"),
        name="rwkv_prep_seq",
    )(proj, proj, proj, proj, proj, proj, proj, proj, *shift0, *mu, w0, a0, w2p, a2p)


def _rwkv_prep_tok(proj, shift0, mu, w0, a0, w2p, a2p):
    _, m, _ = proj.shape
    cols = ((B_W, COL_R), (B_W, COL_K), (B_W, COL_V), (LORA_W, COL_LORA))
    cur = [pl.BlockSpec((1, m, w), functools.partial(lambda i, c: (0, 0, c), c=c)) for w, c in cols]
    s0 = [pl.BlockSpec((m, w), lambda i: (0, 0)) for w, _ in cols]
    rowv = [pl.BlockSpec((1, w), lambda i: (0, 0)) for w, _ in cols]
    full = lambda shape: pl.BlockSpec(shape, lambda i: (0,) * len(shape))
    out_spec = pl.BlockSpec((1, m, B_W), lambda i: (0, 0, 0))
    return pl.pallas_call(
        _prep_tok_kernel,
        grid=(1,),
        in_specs=cur + s0 + rowv + [full((1, B_W)), full((1, B_W)),
                                    full((LORA_W, B_W)), full((LORA_W, B_W))],
        out_specs=[out_spec] * 5,
        out_shape=[jax.ShapeDtypeStruct((1, m, B_W), F32)] * 5,
        compiler_params=_cparams("arbitrary"),
        name="rwkv_prep_tok",
    )(proj, proj, proj, proj, *shift0, *mu, w0, a0, w2p, a2p)


WKV_ROWS = 32


AHEAD = 2


def _wkv_kernel(xr, xw, xk, xv, xa, hr, hw, hk, hv, ha, s0_ref, kk_c, ka_c, rk_c, gg_c, gb_c,
                y_ref, s_ref, r_s, w_s, k_s, a_s, vn_s, vt_s, nkk_s, b_s, kp_s, rks_s, ys_s,
                *, tt, nb):
    step = pl.program_id(0)
    ng = tt // V7X_SUBLANES
    pitch = tt + V7X_SUBLANES
    vpitch = HEAD_SIZE + V7X_SUBLANES

    @pl.when(step == 0)
    def _():
        s_ref[...] = s0_ref[...]

    dsts = (r_s, w_s, k_s, a_s, vn_s)

    def group_rows(rg):
        start = rg * V7X_SUBLANES
        if not isinstance(rg, int):
            start = pl.multiple_of(start, V7X_SUBLANES)
        return pl.ds(start, V7X_SUBLANES)

    def exchange(srcs, src_rg, dst_rg, q):
        src_rows = group_rows(src_rg)
        dst_rows = group_rows(dst_rg)
        for x, dst in zip(srcs, dsts):
            for b in range(nb):
                tile = x[b, q, src_rows, :]
                lanes = slice(b * LANE_GROUP, (b + 1) * LANE_GROUP)
                for g in range(nb):
                    dst[nb * q + g, dst_rows, lanes] = _move_lane_group(tile, g, b)[:, lanes]

    def v_by_token(rg, q):
        rows = group_rows(rg)
        for g in range(nb):
            n = nb * q + g
            vt_s[pl.ds(rg * V7X_SUBLANES * vpitch + n, V7X_SUBLANES, stride=vpitch), :] = vn_s[n, rows, :]

    def prep(m):
        ts = group_rows(m)
        k = k_s[:, ts, :]
        a = a_s[:, ts, :]
        kk = k * kk_c[...]
        nrm = jnp.sqrt(jnp.sum(kk * kk, axis=0))
        kk = kk / jnp.maximum(nrm, 1e-12)[None]
        nkk_s[:, ts, :] = -kk
        b_s[:, ts, :] = kk * a
        kp = k * (1.0 + (a - 1.0) * ka_c[...])
        kp_s[:, ts, :] = kp
        rks_s[ts, :] = jnp.sum(r_s[:, ts, :] * kp * rk_c[...], axis=0)

    cur = (xr, xw, xk, xa, xv)
    nxt = (hr, hw, hk, ha, hv)

    @pl.when(step == 0)
    def _():
        for rg in range(AHEAD):
            for q in range(B_W // V7X_LANES):
                exchange(cur, rg, rg, q)
                v_by_token(rg, q)

    @pl.when(step > 0)
    def _():
        head = AHEAD * V7X_SUBLANES
        for dst in dsts:
            dst[:, 0:head, :] = dst[:, tt:tt + head, :]
        vt_s[0:head * vpitch, :] = vt_s[tt * vpitch:(tt + head) * vpitch, :]

    prep(0)

    halves = [h * WKV_ROWS for h in range(HEAD_SIZE // WKV_ROWS)]

    def bcast(ref, t, j):
        return jnp.broadcast_to(ref[j, pl.ds(t, 1), :], (WKV_ROWS, V7X_LANES))

    def first_u(i0):
        acc = jnp.zeros((WKV_ROWS, V7X_LANES), F32)
        for j in range(HEAD_SIZE):
            acc = acc + s_ref[j, i0:i0 + WKV_ROWS, :] * bcast(nkk_s, 0, j)
        return acc

    def group(m, u, srcs, src_rg):
        if m + 1 < ng:
            prep(m + 1)

        def token(tau, u):
            t = m * V7X_SUBLANES + tau
            t_next = jnp.minimum(t + 1, tt - 1)
            u_next = []
            for i0, uh in zip(halves, u):
                v = vt_s[pl.ds(pl.multiple_of(t * vpitch + i0, V7X_SUBLANES), WKV_ROWS), :]
                y_acc = jnp.zeros((WKV_ROWS, V7X_LANES), F32)
                u_acc = jnp.zeros((WKV_ROWS, V7X_LANES), F32)
                for j in range(HEAD_SIZE):
                    sj = ((s_ref[j, i0:i0 + WKV_ROWS, :] * bcast(w_s, t, j) + uh * bcast(b_s, t, j))
                          + v * bcast(kp_s, t, j))
                    s_ref[j, i0:i0 + WKV_ROWS, :] = sj
                    y_acc = y_acc + sj * bcast(r_s, t, j)
                    u_acc = u_acc + sj * bcast(nkk_s, t_next, j)
                ys_s[pl.ds(i0 * pitch + t, WKV_ROWS, stride=pitch), :] = y_acc
                u_next.append(u_acc)
            prev = t + V7X_SUBLANES - 1
            prev_rg = lax.shift_right_logical(prev, 3) - 1 + AHEAD
            prev_q = 2 * (prev & (V7X_SUBLANES - 1))
            for d in range(2):
                v_by_token(prev_rg, prev_q + d)
                exchange(srcs, src_rg, m + AHEAD, 2 * tau + d)
            return tuple(u_next)

        return lax.fori_loop(0, V7X_SUBLANES, token, u)

    u = tuple(first_u(i0) for i0 in halves)
    for m in range(ng):
        u = group(m, u, cur, m + AHEAD) if m + AHEAD < ng else group(m, u, nxt, m + AHEAD - ng)
    for d in range(2):
        v_by_token(ng - 1 + AHEAD, B_W // V7X_LANES - 2 + d)

    def post(c, carry):
        t0 = pl.multiple_of(c * V7X_SUBLANES, V7X_SUBLANES)
        ys = [ys_s[pl.ds(i * pitch + t0, V7X_SUBLANES), :] for i in range(HEAD_SIZE)]
        mu = sum(ys[1:], ys[0]) * (1.0 / HEAD_SIZE)
        yc = [y - mu for y in ys]
        sq = [y * y for y in yc]
        rstd = lax.rsqrt(sum(sq[1:], sq[0]) * (1.0 / HEAD_SIZE) + GN_EPS)
        scale = rks_s[pl.ds(t0, V7X_SUBLANES), :]
        out = [yc[i] * rstd * gg_c[i:i + 1, :] + gb_c[i:i + 1, :]
               + scale * vn_s[i, pl.ds(t0, V7X_SUBLANES), :] for i in range(HEAD_SIZE)]
        for q in range(HEAD_SIZE // nb):
            nat = _swap_lane_groups(out[nb * q:nb * (q + 1)])
            for b in range(nb):
                y_ref[b, pl.ds(t0, V7X_SUBLANES), q * V7X_LANES:(q + 1) * V7X_LANES] = nat[b]
        return carry

    lax.fori_loop(0, ng, post, 0, unroll=2)


def _wkv(r, w, k, v, a, s0, consts, tt):
    nb, nq, t, lanes = r.shape
    n = HEAD_SIZE
    assert lanes == V7X_LANES and nb * B_HEADS == lanes and nq * lanes == B_W
    head = AHEAD * V7X_SUBLANES
    assert tt % head == 0 and tt // V7X_SUBLANES > AHEAD
    seq = pl.BlockSpec((nb, nq, tt, lanes), lambda i: (0, 0, i, 0))
    nxt = pl.BlockSpec((nb, nq, head, lanes),
                       lambda i: (0, 0, jnp.minimum((i + 1) * (tt // head), t // head - 1), 0))
    st = pl.BlockSpec((n, n, lanes), lambda i: (0, 0, 0))
    c3 = pl.BlockSpec((n, 1, lanes), lambda i: (0, 0, 0))
    c2 = pl.BlockSpec((n, lanes), lambda i: (0, 0))
    kk_c, ka_c, rk_c, gg_c, gb_c = consts
    col = lambda c: c.reshape(n, 1, lanes)
    pitch = tt + V7X_SUBLANES
    by_n = pltpu.VMEM((n, tt, lanes), F32)
    lead_n = pltpu.VMEM((n, tt + head, lanes), F32)
    return pl.pallas_call(
        functools.partial(_wkv_kernel, tt=tt, nb=nb),
        grid=(t // tt,),
        in_specs=[seq] * 5 + [nxt] * 5
        + [pl.BlockSpec((n, n, lanes), lambda i: (0, 0, 0), pipeline_mode=pl.Buffered(1)),
           c3, c3, c3, c2, c2],
        out_specs=[pl.BlockSpec((nb, tt, B_W), lambda i: (0, i, 0)), st],
        out_shape=[jax.ShapeDtypeStruct((nb, t, B_W), F32),
                   jax.ShapeDtypeStruct((n, n, lanes), F32)],
        scratch_shapes=[lead_n] * 5
        + [pltpu.VMEM(((tt + head) * (n + V7X_SUBLANES), lanes), F32)] + [by_n] * 3
        + [pltpu.VMEM((tt, lanes), F32), pltpu.VMEM((n * pitch, lanes), F32)],
        compiler_params=_cparams("arbitrary"),
        name="wkv",
    )(r, w, k, v, a, r, w, k, v, a, s0, col(kk_c), col(ka_c), col(rk_c), gg_c, gb_c)


def _wkv_tok_kernel(r_ref, w_ref, k_ref, v_ref, a_ref, s0_ref, kk_c, ka_c, rk_c, gg_c, gb_c,
                    y_ref, s_ref, ys_s):
    k = k_ref[0]
    a = a_ref[0]
    r = r_ref[0]
    w = w_ref[0]
    v = v_ref[0]
    kk = k * kk_c[0]
    nrm = jnp.sqrt(jnp.sum(kk * kk, axis=0, keepdims=True))
    kk = kk / jnp.maximum(nrm, 1e-12)
    nkk = -kk
    bvec = kk * a
    kp = k * (1.0 + (a - 1.0) * ka_c[0])
    wr = w * r
    br = jnp.sum(bvec * r, axis=0, keepdims=True)
    kr = jnp.sum(kp * r, axis=0, keepdims=True)
    for i in range(HEAD_SIZE):
        s = s0_ref[0, i]
        u = jnp.sum(s * nkk, axis=0, keepdims=True)
        yw = jnp.sum(s * wr, axis=0, keepdims=True)
        vi = v[i:i + 1, :]
        s_ref[0, i] = (s * w + u * bvec) + vi * kp
        ys_s[i:i + 1, :] = (yw + u * br) + vi * kr
    y = ys_s[...]
    mu = jnp.mean(y, axis=0, keepdims=True)
    yc = y - mu
    var = jnp.mean(yc * yc, axis=0, keepdims=True)
    yn = yc * lax.rsqrt(var + GN_EPS) * gg_c[0] + gb_c[0]
    bonus = jnp.sum(r * kp * rk_c[0], axis=0, keepdims=True) * v
    y_ref[0] = yn + bonus


def _wkv_tok(r, w, k, v, a, s0, consts):
    nh, n, bsz = r.shape
    vec = pl.BlockSpec((1, n, bsz), lambda h: (h, 0, 0))
    st = pl.BlockSpec((1, n, n, bsz), lambda h: (h, 0, 0, 0))
    return pl.pallas_call(
        _wkv_tok_kernel,
        grid=(nh,),
        in_specs=[vec] * 5 + [st] + [vec] * 5,
        out_specs=[vec, st],
        out_shape=[jax.ShapeDtypeStruct((nh, n, bsz), F32),
                   jax.ShapeDtypeStruct(s0.shape, F32)],
        scratch_shapes=[pltpu.VMEM((n, bsz), F32)],
        compiler_params=_cparams("arbitrary"),
        name="wkv_tok",
    )(r, w, k, v, a, s0, *consts)


def _head_const(x, bsz):
    return jnp.tile(x.reshape(B_HEADS, HEAD_SIZE).T, (1, bsz))


def _gelu(x):
    return jax.nn.gelu(x)


def _layernorm(x, g, b):
    mu = jnp.mean(x, axis=-1, keepdims=True)
    xc = x - mu
    var = jnp.mean(xc * xc, axis=-1, keepdims=True)
    return xc * lax.rsqrt(var + NORM_EPS) * g + b


def _branch_a_seq_kernel(u_ref, v_ref, z_ref, g_ref, lg, lb, ws_ref, bias_ref, pa_ref, o_ref, *, tm):
    u = _gelu(u_ref[0])
    v = _layernorm(_gelu(v_ref[0]), lg[...], lb[...])
    vb = v.astype(BF16)
    row = lax.broadcasted_iota(jnp.int32, (CHUNK, CHUNK), 0)
    col = lax.broadcasted_iota(jnp.int32, (CHUNK, CHUNK), 1)
    rows = []
    for c in range(tm // CHUNK):
        blocks = []
        for g in range(A_GROUPS):
            wm = jnp.where(row >= col, ws_ref[g], 0.0).astype(BF16)
            blk = vb[c * CHUNK:(c + 1) * CHUNK, g * CHUNK:(g + 1) * CHUNK]
            blocks.append(jnp.dot(wm, blk, preferred_element_type=F32))
        rows.append(jnp.concatenate(blocks, axis=1) + bias_ref[...])
    mix = jnp.concatenate(rows, axis=0)
    ya = u * mix * jax.nn.silu(z_ref[0])
    o_ref[0] = jax.nn.sigmoid(g_ref[0]) * jnp.dot(ya.astype(BF16), pa_ref[...],
                                                  preferred_element_type=F32)


def _branch_a_tok_kernel(u_ref, v_ref, z_ref, g_ref, lg, lb, w00, b00, pa_ref, o_ref, vo_ref):
    u = _gelu(u_ref[0])
    v = _layernorm(_gelu(v_ref[0]), lg[...], lb[...])
    vo_ref[...] = v
    mix = v * w00[...] + b00[...]
    ya = u * mix * jax.nn.silu(z_ref[0])
    o_ref[0] = jax.nn.sigmoid(g_ref[0]) * jnp.dot(ya.astype(BF16), pa_ref[...],
                                                  preferred_element_type=F32)


def _branch_a_seq(proj, ln_g, ln_b, w_s, bias2d, p_a, tm=256):
    bsz, t, _ = proj.shape
    tm = min(tm, t)
    full = lambda shape: pl.BlockSpec(shape, lambda b, i: (0,) * len(shape))
    sec = lambda w, c: pl.BlockSpec((1, tm, w), lambda b, i: (b, i, c))
    return pl.pallas_call(
        functools.partial(_branch_a_seq_kernel, tm=tm),
        grid=(bsz, t // tm),
        in_specs=[sec(A_W, COL_UA), sec(A_W, COL_VA), sec(A_W, COL_ZA), sec(D_MODEL, COL_GA),
                  full((1, A_W)), full((1, A_W)), full((A_GROUPS, CHUNK, CHUNK)),
                  full((CHUNK, A_W)), full((A_W, D_MODEL))],
        out_specs=pl.BlockSpec((1, tm, D_MODEL), lambda b, i: (b, i, 0)),
        out_shape=jax.ShapeDtypeStruct((bsz, t, D_MODEL), F32),
        compiler_params=_cparams("arbitrary", "arbitrary"),
        name="branch_a_seq",
    )(proj, proj, proj, proj, ln_g, ln_b, w_s, bias2d, p_a)


def _branch_a_tok(proj, ln_g, ln_b, w00, b00, p_a):
    _, m, _ = proj.shape
    full = lambda shape: pl.BlockSpec(shape, lambda i: (0,) * len(shape))
    sec = lambda w, c: pl.BlockSpec((1, m, w), lambda i: (0, 0, c))
    return pl.pallas_call(
        _branch_a_tok_kernel,
        grid=(1,),
        in_specs=[sec(A_W, COL_UA), sec(A_W, COL_VA), sec(A_W, COL_ZA), sec(D_MODEL, COL_GA),
                  full((1, A_W)), full((1, A_W)), full((1, A_W)), full((1, A_W)),
                  full((A_W, D_MODEL))],
        out_specs=[pl.BlockSpec((1, m, D_MODEL), lambda i: (0, 0, 0)),
                   pl.BlockSpec((m, A_W), lambda i: (0, 0))],
        out_shape=[jax.ShapeDtypeStruct((1, m, D_MODEL), F32),
                   jax.ShapeDtypeStruct((m, A_W), F32)],
        compiler_params=_cparams("arbitrary"),
        name="branch_a_tok",
    )(proj, proj, proj, proj, ln_g, ln_b, w00, b00, p_a)


def _merge_kernel(yb_ref, zb_ref, gb_ref, oa_ref, x_ref, cg_ref, pb_ref, wo_ref, fg_ref, o_ref):
    yb = (yb_ref[0] * jax.nn.silu(zb_ref[0])).astype(BF16)
    merged = oa_ref[0] + jax.nn.sigmoid(gb_ref[0]) * jnp.dot(yb, pb_ref[...],
                                                            preferred_element_type=F32)
    out = x_ref[0] + cg_ref[0] * jnp.dot(merged.astype(BF16), wo_ref[...],
                                         preferred_element_type=F32)
    ms = jnp.mean(out * out, axis=-1, keepdims=True)
    o_ref[0] = out * lax.rsqrt(ms + NORM_EPS) * fg_ref[...]


def _merge(yb, proj, oa, x, c_gate, p_b, w_out, final_g, tm):
    bsz, t, d = x.shape
    tm = min(tm, t)
    tg = c_gate.shape[1]
    row = lambda: pl.BlockSpec((1, tm, d), lambda b, i: (b, i, 0))
    sec = lambda c: pl.BlockSpec((1, tm, d), lambda b, i: (b, i, c))
    gate = (pl.BlockSpec((1, 1, d), lambda b, i: (b, 0, 0)) if tg == 1
            else pl.BlockSpec((1, tm, d), lambda b, i: (b, i, 0)))
    wspec = lambda: pl.BlockSpec((d, d), lambda b, i: (0, 0), pipeline_mode=pl.Buffered(1))
    return pl.pallas_call(
        _merge_kernel,
        grid=(bsz, t // tm),
        in_specs=[row(), sec(COL_ZB), sec(COL_GB), row(), row(), gate, wspec(), wspec(),
                  pl.BlockSpec((1, d), lambda b, i: (0, 0))],
        out_specs=row(),
        out_shape=jax.ShapeDtypeStruct((bsz, t, d), F32),
        compiler_params=_cparams("arbitrary", "arbitrary"),
        name="merge",
    )(yb, proj, proj, oa, x, c_gate, p_b, w_out, final_g.reshape(1, d))


def _split(t, sizes):
    out, o = [], 0
    for s in sizes:
        out.append(t[..., o:o + s])
        o += s
    return out


def _permute_shift_cols(x):
    r, wd, k, v, ad = _split(x, (B_W, LORA, B_W, B_W, LORA))
    pad = jnp.zeros(x.shape[:-1] + (LORA_W - 2 * LORA,), x.dtype)
    return r, k, v, jnp.concatenate([wd, ad, pad], axis=-1)


def _nh_order(x, axis=-1):
    axis = axis % x.ndim
    shape = x.shape
    x = x.reshape(shape[:axis] + (B_HEADS, HEAD_SIZE) + shape[axis + 1:])
    return jnp.swapaxes(x, axis, axis + 1).reshape(shape)


def _hn_order(x, axis=-1):
    axis = axis % x.ndim
    shape = x.shape
    x = x.reshape(shape[:axis] + (HEAD_SIZE, B_HEADS) + shape[axis + 1:])
    return jnp.swapaxes(x, axis, axis + 1).reshape(shape)


def _shift_row(proj_row):
    r = _hn_order(proj_row[..., 0:B_W])
    k = _hn_order(proj_row[..., B_W:2 * B_W])
    v = _hn_order(proj_row[..., 2 * B_W:3 * B_W])
    lo = proj_row[..., COL_LORA * LORA_W:]
    return jnp.concatenate([r, lo[..., :LORA], k, v, lo[..., LORA:2 * LORA]], axis=-1)


def kernel(x_prompt, x_sample, c_prompt, c_sample, state_wkv, state_shift, norm_g, w_c, b_c, w_in, ln_v_g, ln_v_b, w_s, b_s, mu_shift, w0, w2, a0, a2, k_k, k_a, r_k, gn_g, gn_b, p_a, p_b, w_out, final_g):
    assert norm_g.shape[0] == 1, "single-layer trunk"
    bp, t, d = x_prompt.shape
    bs = x_sample.shape[0]

    w_rows = jnp.swapaxes(w_in[0], 0, 1)
    bounds = np.cumsum([0, A_W, A_W, A_W, B_W, LORA, B_W, B_W, LORA, B_W, D_MODEL, D_MODEL])
    ua, va, za, wr, wwd, wk, wv, wad, zb, ga, gb = (w_rows[a:b] for a, b in zip(bounds[:-1], bounds[1:]))
    wpad = jnp.zeros((LORA_W - 2 * LORA, d), F32)
    nh0 = lambda x: _nh_order(x, axis=0)
    w_p = jnp.concatenate([nh0(wr), nh0(wk), nh0(wv), nh0(zb), ga, gb, ua, va, za, wwd, wad, wpad],
                          axis=0).astype(BF16)

    def shift_cols(x):
        xr, xk, xv, xlo = _permute_shift_cols(x)
        return _nh_order(xr), _nh_order(xk), _nh_order(xv), xlo

    mu = [m.reshape(1, -1) for m in shift_cols(mu_shift[0])]
    zrow = jnp.zeros((LORA_W - LORA, B_W), F32)
    w2p = jnp.concatenate([_nh_order(w2[0]), zrow], axis=0).astype(BF16)
    a2p = jnp.concatenate([jnp.zeros((LORA, B_W), F32), _nh_order(a2[0]),
                           jnp.zeros((LORA_W - 2 * LORA, B_W), F32)], axis=0).astype(BF16)
    w0r, a0r = _nh_order(w0[0]).reshape(1, B_W), _nh_order(a0[0]).reshape(1, B_W)
    p_a_b, w_out_b = p_a[0].astype(BF16), w_out[0].astype(BF16)
    p_b_b = nh0(p_b[0]).astype(BF16)
    ln_g, ln_b = ln_v_g[0].reshape(1, A_W), ln_v_b[0].reshape(1, A_W)
    bias2d = jnp.repeat(b_s[0].T, CHUNK, axis=1)
    w00 = jnp.repeat(w_s[0][:, 0, 0], CHUNK).reshape(1, A_W)
    b00 = jnp.repeat(b_s[0][:, 0], CHUNK).reshape(1, A_W)

    m_rows = bp + bs
    m_pad = -m_rows % V7X_SUBLANES
    c_all = jnp.concatenate([c_prompt, c_sample, jnp.zeros((m_pad, d), F32)], axis=0)
    mod = _modulation(c_all, w_c[0], b_c[0])
    shift_p, scale_p, gate_p = (mod[:bp, i * d:(i + 1) * d].reshape(bp, 1, d) for i in range(3))
    shift_s, scale_s, gate_s = (mod[bp:m_rows, i * d:(i + 1) * d].reshape(1, bs, d) for i in range(3))

    def wkv_consts(bsz):
        return [_head_const(c, bsz) for c in (k_k[0], k_a[0], r_k[0], gn_g[0], gn_b[0])]

    xs = x_sample.reshape(1, bs, d)
    h_p = _norm_modulate(x_prompt, norm_g[0], scale_p, shift_p, tm=512)
    h_s = _norm_modulate(xs, norm_g[0], scale_s, shift_s, tm=bs)
    proj_p, proj_s = _in_projection(h_p, h_s[0], w_p)
    proj_s = proj_s[None]

    zeros_shift = [jnp.zeros((bp, 1, w), F32) for w in (B_W, B_W, B_W, LORA_W)]
    feats = _rwkv_prep_seq(proj_p, zeros_shift, mu, w0r, a0r, w2p, a2p)
    s0_p = jnp.zeros((HEAD_SIZE, HEAD_SIZE, bp * B_HEADS), F32)
    yb_p, s_hl = _wkv(*feats, s0_p, wkv_consts(bp), tt=32)
    oa_p = _branch_a_seq(proj_p, ln_g, ln_b, w_s[0], bias2d, p_a_b)
    y_prompt = _merge(yb_p, proj_p, oa_p, x_prompt, gate_p, p_b_b, w_out_b, final_g, tm=256)
    wkv_prompt = jnp.transpose(s_hl.reshape(HEAD_SIZE, HEAD_SIZE, bp, B_HEADS), (2, 3, 1, 0))[None]
    shift_prompt = _shift_row(proj_p[:, t - 1, :])[None]

    feats = _rwkv_prep_tok(proj_s, shift_cols(state_shift[0]), mu, w0r, a0r, w2p, a2p)
    feats = [jnp.transpose(f.reshape(bs, HEAD_SIZE, B_HEADS), (2, 1, 0)) for f in feats]
    tok_consts = [jnp.broadcast_to(c.reshape(B_HEADS, HEAD_SIZE, 1), (B_HEADS, HEAD_SIZE, bs))
                  for c in (k_k[0], k_a[0], r_k[0], gn_g[0], gn_b[0])]
    yb_s, wkv_s = _wkv_tok(*feats, jnp.transpose(state_wkv[0], (1, 2, 3, 0)), tok_consts)
    yb_s = jnp.transpose(yb_s, (2, 1, 0)).reshape(1, bs, B_W)
    oa_s, v_s = _branch_a_tok(proj_s, ln_g, ln_b, w00, b00, p_a_b)
    y_s = _merge(yb_s, proj_s, oa_s, xs, gate_s, p_b_b, w_out_b, final_g, tm=bs)
    y_sample = y_s.reshape(bs, 1, d)
    wkv_sample = jnp.transpose(wkv_s, (3, 0, 1, 2))[None]
    shift_sample = _shift_row(proj_s[0])[None]
    chunk_v_sample = v_s.reshape(1, bs, 1, A_W)

    return (y_prompt, y_sample, wkv_prompt, shift_prompt, wkv_sample, shift_sample, chunk_v_sample)
```

```python
import functools
import math

import jax
import jax.numpy as jnp
import numpy as np
from jax import lax
from jax.experimental import pallas as pl
from jax.experimental.pallas import tpu as pltpu

F32 = jnp.float32
BF16 = jnp.bfloat16

D_MODEL = 2048
A_W = 1024
A_GROUPS = 8
CHUNK = 128
B_HEADS = 32
HEAD_SIZE = 64
B_W = B_HEADS * HEAD_SIZE
LORA = 96
NORM_EPS = 1e-6
GN_EPS = HEAD_SIZE * 1e-5
DECAY_SCALE = math.exp(-0.5)

LORA_W = 512
PROJ_W = 6 * B_W + 3 * A_W + LORA_W
COL_R, COL_K, COL_V, COL_ZB, COL_GA, COL_GB = 0, 1, 2, 3, 4, 5
COL_UA, COL_VA, COL_ZA = 12, 13, 14
COL_LORA = (6 * B_W + 3 * A_W) // LORA_W

V7X_LANES = 128
V7X_SUBLANES = 8
LANE_GROUP = B_HEADS
VMEM_LIMIT = 56 * 1024 * 1024


def _cparams(*sem):
    return pltpu.CompilerParams(dimension_semantics=sem, vmem_limit_bytes=VMEM_LIMIT)


def _mod_kernel(c_ref, w_ref, b_ref, o_ref):
    acc = jnp.dot(c_ref[...].astype(BF16), w_ref[...].astype(BF16), preferred_element_type=F32)
    o_ref[...] = acc + b_ref[...]


def _modulation(c_all, w_c, b_c):
    m, d = c_all.shape
    n = w_c.shape[1]
    tn = 768
    return pl.pallas_call(
        _mod_kernel,
        grid=(n // tn,),
        in_specs=[pl.BlockSpec((m, d), lambda j: (0, 0)),
                  pl.BlockSpec((d, tn), lambda j: (0, j)),
                  pl.BlockSpec((1, tn), lambda j: (0, j))],
        out_specs=pl.BlockSpec((m, tn), lambda j: (0, j)),
        out_shape=jax.ShapeDtypeStruct((m, n), F32),
        compiler_params=_cparams("arbitrary"),
        name="modulation",
    )(c_all, w_c, b_c.reshape(1, n))


def _normmod_kernel(x_ref, g_ref, sc_ref, sh_ref, o_ref):
    x = x_ref[0]
    ms = jnp.mean(x * x, axis=-1, keepdims=True)
    y = x * lax.rsqrt(ms + NORM_EPS) * g_ref[...]
    o_ref[0] = (y * (1.0 + sc_ref[0]) + sh_ref[0]).astype(BF16)


def _norm_modulate(x, g, scale, shift, tm):
    bsz, t, d = x.shape
    tm = min(tm, t)
    ts = scale.shape[1]
    sspec = (pl.BlockSpec((1, 1, d), lambda b, i: (b, 0, 0)) if ts == 1
             else pl.BlockSpec((1, tm, d), lambda b, i: (b, i, 0)))
    return pl.pallas_call(
        _normmod_kernel,
        grid=(bsz, t // tm),
        in_specs=[pl.BlockSpec((1, tm, d), lambda b, i: (b, i, 0)),
                  pl.BlockSpec((1, d), lambda b, i: (0, 0)),
                  sspec, sspec],
        out_specs=pl.BlockSpec((1, tm, d), lambda b, i: (b, i, 0)),
        out_shape=jax.ShapeDtypeStruct((bsz, t, d), BF16),
        compiler_params=_cparams("arbitrary", "arbitrary"),
        name="norm_modulate",
    )(x, g.reshape(1, d), scale, shift)


def _inproj_kernel(h_ref, hs_ref, w_ref, o_ref, os_ref):
    nt = (((1,), (1,)), ((), ()))
    o_ref[0] = lax.dot_general(h_ref[0], w_ref[...], nt, preferred_element_type=F32)

    @pl.when(pl.program_id(0) == 0)
    def _():
        os_ref[...] = lax.dot_general(hs_ref[...], w_ref[...], nt, preferred_element_type=F32)


def _in_projection(h, h_tok, w_t, tn=512):
    bsz, t, d = h.shape
    m = h_tok.shape[0]
    n = w_t.shape[0]
    nj = n // tn
    tok_out = pl.BlockSpec((m, tn), lambda b, j: (0, jnp.where(b == 0, j, nj - 1)))
    return pl.pallas_call(
        _inproj_kernel,
        grid=(bsz, nj),
        in_specs=[pl.BlockSpec((1, t, d), lambda b, j: (b, 0, 0)),
                  pl.BlockSpec((m, d), lambda b, j: (0, 0)),
                  pl.BlockSpec((tn, d), lambda b, j: (j, 0))],
        out_specs=[pl.BlockSpec((1, t, tn), lambda b, j: (b, 0, j)), tok_out],
        out_shape=[jax.ShapeDtypeStruct((bsz, t, n), F32), jax.ShapeDtypeStruct((m, n), F32)],
        compiler_params=_cparams("arbitrary", "arbitrary"),
        name="in_projection",
    )(h, h_tok, w_t)


def _lerp(p, q, mu):
    return p + mu * (q - p)


def _lora_logits(lo, w0, a0, w2p, a2p):
    col = lax.broadcasted_iota(jnp.int32, lo.shape, 1)
    lt = jnp.where(col < LORA, jnp.tanh(lo), lo).astype(BF16)
    wl = w0 + jnp.dot(lt, w2p, preferred_element_type=F32)
    al = a0 + jnp.dot(lt, a2p, preferred_element_type=F32)
    return wl, al


def _decay(wl):
    return jnp.exp(-DECAY_SCALE * jax.nn.sigmoid(wl))


def _prep_core(p, q, mu, w0, a0, w2p, a2p):
    r, k, v, lo = (_lerp(pi, qi, mi) for pi, qi, mi in zip(p, q, mu))
    wl, al = _lora_logits(lo, w0, a0, w2p, a2p)
    return r, _decay(wl), k, v, jax.nn.sigmoid(al)


def _move_lane_group(x, src_g, dst_g):
    shift = ((dst_g - src_g) % (V7X_LANES // LANE_GROUP)) * LANE_GROUP
    return x if shift == 0 else pltpu.roll(x, shift, axis=1)


def _swap_lane_groups(x):
    lane = lax.broadcasted_iota(jnp.int32, x[0].shape, 1)
    low_half = lane < 2 * LANE_GROUP
    even_group = (lane // LANE_GROUP) % 2 == 0
    rot = lambda v, s: pltpu.roll(v, s, axis=1)
    y0 = jnp.where(low_half, x[0], rot(x[2], 2 * LANE_GROUP))
    y2 = jnp.where(low_half, rot(x[0], 2 * LANE_GROUP), x[2])
    y1 = jnp.where(low_half, x[1], rot(x[3], 2 * LANE_GROUP))
    y3 = jnp.where(low_half, rot(x[1], 2 * LANE_GROUP), x[3])
    return [jnp.where(even_group, y0, rot(y1, LANE_GROUP)),
            jnp.where(even_group, rot(y0, 3 * LANE_GROUP), y1),
            jnp.where(even_group, y2, rot(y3, LANE_GROUP)),
            jnp.where(even_group, rot(y2, 3 * LANE_GROUP), y3)]


def _prep_seq_kernel(pr, pk, pv, plo, tr, tk, tv, tlo, sr, sk, sv, slo,
                     mr, mk, mv, mlo, w0, a0, w2p, a2p,
                     o_r, o_w, o_k, o_v, o_a):
    i = pl.program_id(1)

    def prev(p_ref, tail_ref, s_ref):
        p = p_ref[0]
        last = tail_ref[0][V7X_SUBLANES - 1:V7X_SUBLANES, :]
        first = jnp.where(i == 0, s_ref[0], last)
        rolled = pltpu.roll(p, 1, axis=0)
        row = lax.broadcasted_iota(jnp.int32, p.shape, 0)
        return p, jnp.where(row == 0, first, rolled)

    pq = [prev(x, y, z) for x, y, z in ((pr, tr, sr), (pk, tk, sk), (pv, tv, sv), (plo, tlo, slo))]
    outs = _prep_core([x[0] for x in pq], [x[1] for x in pq],
                      (mr[...], mk[...], mv[...], mlo[...]), w0[...], a0[...], w2p[...], a2p[...])
    for o, val in zip((o_r, o_w, o_k, o_v, o_a), outs):
        for q in range(B_W // V7X_LANES):
            o[0, q] = val[:, q * V7X_LANES:(q + 1) * V7X_LANES]


def _prep_tok_kernel(pr, pk, pv, plo, sr, sk, sv, slo,
                     mr, mk, mv, mlo, w0, a0, w2p, a2p,
                     o_r, o_w, o_k, o_v, o_a):
    outs = _prep_core((pr[0], pk[0], pv[0], plo[0]), (sr[...], sk[...], sv[...], slo[...]),
                      (mr[...], mk[...], mv[...], mlo[...]), w0[...], a0[...], w2p[...], a2p[...])
    for o, val in zip((o_r, o_w, o_k, o_v, o_a), outs):
        o[0] = val


def _rwkv_prep_seq(proj, shift0, mu, w0, a0, w2p, a2p, tm=128):
    bsz, t, _ = proj.shape
    tail = tm // V7X_SUBLANES
    nq = B_W // V7X_LANES
    cols = ((B_W, COL_R), (B_W, COL_K), (B_W, COL_V), (LORA_W, COL_LORA))
    cur = [pl.BlockSpec((1, tm, w), functools.partial(lambda b, i, c: (b, i, c), c=c)) for w, c in cols]
    tails = [pl.BlockSpec((1, V7X_SUBLANES, w),
                          functools.partial(lambda b, i, c: (b, jnp.maximum(i * tail - 1, 0), c), c=c))
             for w, c in cols]
    s0 = [pl.BlockSpec((1, 1, w), lambda b, i: (b, 0, 0)) for w, _ in cols]
    rowv = [pl.BlockSpec((1, w), lambda b, i: (0, 0)) for w, _ in cols]
    full = lambda shape: pl.BlockSpec(shape, lambda b, i: (0,) * len(shape))
    out_spec = pl.BlockSpec((1, nq, tm, V7X_LANES), lambda b, i: (b, 0, i, 0))
    return pl.pallas_call(
        _prep_seq_kernel,
        grid=(bsz, t // tm),
        in_specs=cur + tails + s0 + rowv + [full((1, B_W)), full((1, B_W)),
                                            full((LORA_W, B_W)), full((LORA_W, B_W))],
        out_specs=[out_spec] * 5,
        out_shape=[jax.ShapeDtypeStruct((bsz, nq, t, V7X_LANES), F32)] * 5,
        compiler_params=_cparams("arbitrary", "arbitrary"),
        name="rwkv_prep_seq",
    )(proj, proj, proj, proj, proj, proj, proj, proj, *shift0, *mu, w0, a0, w2p, a2p)


def _rwkv_prep_tok(proj, shift0, mu, w0, a0, w2p, a2p):
    _, m, _ = proj.shape
    cols = ((B_W, COL_R), (B_W, COL_K), (B_W, COL_V), (LORA_W, COL_LORA))
    cur = [pl.BlockSpec((1, m, w), functools.partial(lambda i, c: (0, 0, c), c=c)) for w, c in cols]
    s0 = [pl.BlockSpec((m, w), lambda i: (0, 0)) for w, _ in cols]
    rowv = [pl.BlockSpec((1, w), lambda i: (0, 0)) for w, _ in cols]
    full = lambda shape: pl.BlockSpec(shape, lambda i: (0,) * len(shape))
    out_spec = pl.BlockSpec((1, m, B_W), lambda i: (0, 0, 0))
    return pl.pallas_call(
        _prep_tok_kernel,
        grid=(1,),
        in_specs=cur + s0 + rowv + [full((1, B_W)), full((1, B_W)),
                                    full((LORA_W, B_W)), full((LORA_W, B_W))],
        out_specs=[out_spec] * 5,
        out_shape=[jax.ShapeDtypeStruct((1, m, B_W), F32)] * 5,
        compiler_params=_cparams("arbitrary"),
        name="rwkv_prep_tok",
    )(proj, proj, proj, proj, *shift0, *mu, w0, a0, w2p, a2p)


WKV_ROWS = 32


AHEAD = 2


def _wkv_kernel(xr, xw, xk, xv, xa, hr, hw, hk, hv, ha, s0_ref, kk_c, ka_c, rk_c, gg_c, gb_c,
                y_ref, s_ref, r_s, w_s, k_s, a_s, vn_s, vt_s, nkk_s, b_s, kp_s,
                p1_s, p2_s, p3_s, p4_s, p5_s, rks_s, cba_s, cka_s, cbr_s, ckr_s, ys_s,
                *, tt, nb):
    step = pl.program_id(0)
    ng = tt // V7X_SUBLANES
    pitch = tt + V7X_SUBLANES
    vpitch = HEAD_SIZE + V7X_SUBLANES

    @pl.when(step == 0)
    def _():
        s_ref[...] = s0_ref[...]

    dsts = (r_s, w_s, k_s, a_s, vn_s)

    def group_rows(rg):
        start = rg * V7X_SUBLANES
        if not isinstance(rg, int):
            start = pl.multiple_of(start, V7X_SUBLANES)
        return pl.ds(start, V7X_SUBLANES)

    def exchange(srcs, src_rg, dst_rg, q):
        src_rows = group_rows(src_rg)
        dst_rows = group_rows(dst_rg)
        for x, dst in zip(srcs, dsts):
            for b in range(nb):
                tile = x[b, q, src_rows, :]
                lanes = slice(b * LANE_GROUP, (b + 1) * LANE_GROUP)
                for g in range(nb):
                    dst[nb * q + g, dst_rows, lanes] = _move_lane_group(tile, g, b)[:, lanes]

    def v_by_token(rg, q):
        rows = group_rows(rg)
        for g in range(nb):
            n = nb * q + g
            vt_s[pl.ds(rg * V7X_SUBLANES * vpitch + n, V7X_SUBLANES, stride=vpitch), :] = vn_s[n, rows, :]

    def prep(m):
        ts = group_rows(m)
        sq = None
        for n in range(HEAD_SIZE):
            kk = k_s[n, ts, :] * kk_c[n]
            sq = kk * kk if sq is None else sq + kk * kk
        den = jnp.maximum(jnp.sqrt(sq), 1e-12)
        nxt_tok = lambda x: pltpu.roll(x, V7X_SUBLANES - 1, axis=0)
        acc = [None] * 5
        for n in range(HEAD_SIZE):
            k = k_s[n, ts, :]
            a = a_s[n, ts, :]
            r = r_s[n, ts, :]
            w = w_s[n, ts, :]
            kk = (k * kk_c[n]) / den
            nkk = -kk
            b = kk * a
            kp = k * (1.0 + (a - 1.0) * ka_c[n])
            nkk_n = nxt_tok(nkk)
            w_n = nxt_tok(w)
            nkk_s[n, ts, :] = nkk
            b_s[n, ts, :] = b
            kp_s[n, ts, :] = kp
            p1_s[n, ts, :] = w * r
            p2_s[n, ts, :] = w * w_n
            p3_s[n, ts, :] = b * w_n
            p4_s[n, ts, :] = kp * w_n
            p5_s[n, ts, :] = w * nkk_n
            kpr = kp * r
            terms = (b * nkk_n, kp * nkk_n, b * r, kpr, kpr * rk_c[n])
            acc = [x if y is None else y + x for x, y in zip(terms, acc)]
        for ref, val in zip((cba_s, cka_s, cbr_s, ckr_s, rks_s), acc):
            ref[ts, :] = val

    cur = (xr, xw, xk, xa, xv)
    nxt = (hr, hw, hk, ha, hv)

    @pl.when(step == 0)
    def _():
        for rg in range(AHEAD):
            for q in range(B_W // V7X_LANES):
                exchange(cur, rg, rg, q)
                v_by_token(rg, q)

    @pl.when(step > 0)
    def _():
        head = AHEAD * V7X_SUBLANES
        for dst in dsts:
            dst[:, 0:head, :] = dst[:, tt:tt + head, :]
        vt_s[0:head * vpitch, :] = vt_s[tt * vpitch:(tt + head) * vpitch, :]

    prep(0)

    parts = [h * WKV_ROWS for h in range(HEAD_SIZE // WKV_ROWS)]
    zeros = lambda: jnp.zeros((WKV_ROWS, V7X_LANES), F32)

    def bcast(ref, t, j):
        return jnp.broadcast_to(ref[j, pl.ds(t, 1), :], (WKV_ROWS, V7X_LANES))

    def lane_row(ref, t):
        return jnp.broadcast_to(ref[pl.ds(t, 1), :], (WKV_ROWS, V7X_LANES))

    def first_carry(i0):
        ua, xb = zeros(), zeros()
        for j in range(HEAD_SIZE):
            s = s_ref[j, i0:i0 + WKV_ROWS, :]
            ua = ua + s * bcast(nkk_s, 0, j)
            xb = xb + s * bcast(p5_s, 0, j)
        return ua, xb

    def group(m, carry, srcs, src_rg):
        if m + 1 < ng:
            prep(m + 1)
        pairs = V7X_SUBLANES // 2

        def pair(p, carry):
            ta = m * V7X_SUBLANES + 2 * p
            tb = ta + 1
            tc = jnp.minimum(ta + 2, tt - 2)
            out = []
            for i0, (ua, xb) in zip(parts, carry):
                va = vt_s[pl.ds(pl.multiple_of(ta * vpitch + i0, V7X_SUBLANES), WKV_ROWS), :]
                vb = vt_s[pl.ds(pl.multiple_of(tb * vpitch + i0, V7X_SUBLANES), WKV_ROWS), :]
                ub = (xb + ua * lane_row(cba_s, ta)) + va * lane_row(cka_s, ta)
                ya, yb, uc, xd = zeros(), zeros(), zeros(), zeros()
                for j in range(HEAD_SIZE):
                    s = s_ref[j, i0:i0 + WKV_ROWS, :]
                    ya = ya + s * bcast(p1_s, ta, j)
                    sn = ((((s * bcast(p2_s, ta, j) + ua * bcast(p3_s, ta, j)) + va * bcast(p4_s, ta, j))
                           + ub * bcast(b_s, tb, j)) + vb * bcast(kp_s, tb, j))
                    s_ref[j, i0:i0 + WKV_ROWS, :] = sn
                    yb = yb + sn * bcast(r_s, tb, j)
                    uc = uc + sn * bcast(nkk_s, tc, j)
                    xd = xd + sn * bcast(p5_s, tc, j)
                ya = (ya + ua * lane_row(cbr_s, ta)) + va * lane_row(ckr_s, ta)
                ys_s[pl.ds(i0 * pitch + ta, WKV_ROWS, stride=pitch), :] = ya
                ys_s[pl.ds(i0 * pitch + tb, WKV_ROWS, stride=pitch), :] = yb
                out.append((uc, xd))
            prev = m * pairs + p + pairs - 1
            prev_rg = lax.shift_right_logical(prev, 2) - 1 + AHEAD
            prev_q = 4 * (prev & (pairs - 1))
            for d in range(4):
                v_by_token(prev_rg, prev_q + d)
                exchange(srcs, src_rg, m + AHEAD, 4 * p + d)
            return tuple(out)

        return lax.fori_loop(0, pairs, pair, carry)

    carry = tuple(first_carry(i0) for i0 in parts)
    for m in range(ng):
        carry = (group(m, carry, cur, m + AHEAD) if m + AHEAD < ng
                 else group(m, carry, nxt, m + AHEAD - ng))
    for d in range(4):
        v_by_token(ng - 1 + AHEAD, B_W // V7X_LANES - 4 + d)

    def post(c, carry):
        t0 = pl.multiple_of(c * V7X_SUBLANES, V7X_SUBLANES)
        ys = [ys_s[pl.ds(i * pitch + t0, V7X_SUBLANES), :] for i in range(HEAD_SIZE)]
        mu = sum(ys[1:], ys[0]) * (1.0 / HEAD_SIZE)
        yc = [y - mu for y in ys]
        sq = [y * y for y in yc]
        rstd = lax.rsqrt(sum(sq[1:], sq[0]) * (1.0 / HEAD_SIZE) + GN_EPS)
        scale = rks_s[pl.ds(t0, V7X_SUBLANES), :]
        out = [yc[i] * rstd * gg_c[i:i + 1, :] + gb_c[i:i + 1, :]
               + scale * vn_s[i, pl.ds(t0, V7X_SUBLANES), :] for i in range(HEAD_SIZE)]
        for q in range(HEAD_SIZE // nb):
            nat = _swap_lane_groups(out[nb * q:nb * (q + 1)])
            for b in range(nb):
                y_ref[b, pl.ds(t0, V7X_SUBLANES), q * V7X_LANES:(q + 1) * V7X_LANES] = nat[b]
        return carry

    lax.fori_loop(0, ng, post, 0, unroll=2)


def _wkv(r, w, k, v, a, s0, consts, tt):
    nb, nq, t, lanes = r.shape
    n = HEAD_SIZE
    assert lanes == V7X_LANES and nb * B_HEADS == lanes and nq * lanes == B_W
    head = AHEAD * V7X_SUBLANES
    assert tt % head == 0 and tt // V7X_SUBLANES > AHEAD
    seq = pl.BlockSpec((nb, nq, tt, lanes), lambda i: (0, 0, i, 0))
    nxt = pl.BlockSpec((nb, nq, head, lanes),
                       lambda i: (0, 0, jnp.minimum((i + 1) * (tt // head), t // head - 1), 0))
    st = pl.BlockSpec((n, n, lanes), lambda i: (0, 0, 0))
    c3 = pl.BlockSpec((n, 1, lanes), lambda i: (0, 0, 0))
    c2 = pl.BlockSpec((n, lanes), lambda i: (0, 0))
    kk_c, ka_c, rk_c, gg_c, gb_c = consts
    col = lambda c: c.reshape(n, 1, lanes)
    pitch = tt + V7X_SUBLANES
    by_n = pltpu.VMEM((n, tt, lanes), F32)
    lead_n = pltpu.VMEM((n, tt + head, lanes), F32)
    return pl.pallas_call(
        functools.partial(_wkv_kernel, tt=tt, nb=nb),
        grid=(t // tt,),
        in_specs=[seq] * 5 + [nxt] * 5
        + [pl.BlockSpec((n, n, lanes), lambda i: (0, 0, 0), pipeline_mode=pl.Buffered(1)),
           c3, c3, c3, c2, c2],
        out_specs=[pl.BlockSpec((nb, tt, B_W), lambda i: (0, i, 0)), st],
        out_shape=[jax.ShapeDtypeStruct((nb, t, B_W), F32),
                   jax.ShapeDtypeStruct((n, n, lanes), F32)],
        scratch_shapes=[lead_n] * 5
        + [pltpu.VMEM(((tt + head) * (n + V7X_SUBLANES), lanes), F32)] + [by_n] * 8
        + [pltpu.VMEM((tt, lanes), F32)] * 5 + [pltpu.VMEM((n * pitch, lanes), F32)],
        compiler_params=_cparams("arbitrary"),
        name="wkv",
    )(r, w, k, v, a, r, w, k, v, a, s0, col(kk_c), col(ka_c), col(rk_c), gg_c, gb_c)


def _wkv_tok_kernel(r_ref, w_ref, k_ref, v_ref, a_ref, s0_ref, kk_c, ka_c, rk_c, gg_c, gb_c,
                    y_ref, s_ref, ys_s):
    k = k_ref[0]
    a = a_ref[0]
    r = r_ref[0]
    w = w_ref[0]
    v = v_ref[0]
    kk = k * kk_c[0]
    nrm = jnp.sqrt(jnp.sum(kk * kk, axis=0, keepdims=True))
    kk = kk / jnp.maximum(nrm, 1e-12)
    nkk = -kk
    bvec = kk * a
    kp = k * (1.0 + (a - 1.0) * ka_c[0])
    wr = w * r
    br = jnp.sum(bvec * r, axis=0, keepdims=True)
    kr = jnp.sum(kp * r, axis=0, keepdims=True)
    for i in range(HEAD_SIZE):
        s = s0_ref[0, i]
        u = jnp.sum(s * nkk, axis=0, keepdims=True)
        yw = jnp.sum(s * wr, axis=0, keepdims=True)
        vi = v[i:i + 1, :]
        s_ref[0, i] = (s * w + u * bvec) + vi * kp
        ys_s[i:i + 1, :] = (yw + u * br) + vi * kr
    y = ys_s[...]
    mu = jnp.mean(y, axis=0, keepdims=True)
    yc = y - mu
    var = jnp.mean(yc * yc, axis=0, keepdims=True)
    yn = yc * lax.rsqrt(var + GN_EPS) * gg_c[0] + gb_c[0]
    bonus = jnp.sum(r * kp * rk_c[0], axis=0, keepdims=True) * v
    y_ref[0] = yn + bonus


def _wkv_tok(r, w, k, v, a, s0, consts):
    nh, n, bsz = r.shape
    vec = pl.BlockSpec((1, n, bsz), lambda h: (h, 0, 0))
    st = pl.BlockSpec((1, n, n, bsz), lambda h: (h, 0, 0, 0))
    return pl.pallas_call(
        _wkv_tok_kernel,
        grid=(nh,),
        in_specs=[vec] * 5 + [st] + [vec] * 5,
        out_specs=[vec, st],
        out_shape=[jax.ShapeDtypeStruct((nh, n, bsz), F32),
                   jax.ShapeDtypeStruct(s0.shape, F32)],
        scratch_shapes=[pltpu.VMEM((n, bsz), F32)],
        compiler_params=_cparams("arbitrary"),
        name="wkv_tok",
    )(r, w, k, v, a, s0, *consts)


def _head_const(x, bsz):
    return jnp.tile(x.reshape(B_HEADS, HEAD_SIZE).T, (1, bsz))


def _gelu(x):
    return jax.nn.gelu(x)


def _layernorm(x, g, b):
    mu = jnp.mean(x, axis=-1, keepdims=True)
    xc = x - mu
    var = jnp.mean(xc * xc, axis=-1, keepdims=True)
    return xc * lax.rsqrt(var + NORM_EPS) * g + b


def _branch_a_seq_kernel(u_ref, v_ref, z_ref, g_ref, lg, lb, ws_ref, bias_ref, pa_ref, o_ref, *, tm):
    u = _gelu(u_ref[0])
    v = _layernorm(_gelu(v_ref[0]), lg[...], lb[...])
    vb = v.astype(BF16)
    row = lax.broadcasted_iota(jnp.int32, (CHUNK, CHUNK), 0)
    col = lax.broadcasted_iota(jnp.int32, (CHUNK, CHUNK), 1)
    rows = []
    for c in range(tm // CHUNK):
        blocks = []
        for g in range(A_GROUPS):
            wm = jnp.where(row >= col, ws_ref[g], 0.0).astype(BF16)
            blk = vb[c * CHUNK:(c + 1) * CHUNK, g * CHUNK:(g + 1) * CHUNK]
            blocks.append(jnp.dot(wm, blk, preferred_element_type=F32))
        rows.append(jnp.concatenate(blocks, axis=1) + bias_ref[...])
    mix = jnp.concatenate(rows, axis=0)
    ya = u * mix * jax.nn.silu(z_ref[0])
    o_ref[0] = jax.nn.sigmoid(g_ref[0]) * jnp.dot(ya.astype(BF16), pa_ref[...],
                                                  preferred_element_type=F32)


def _branch_a_tok_kernel(u_ref, v_ref, z_ref, g_ref, lg, lb, w00, b00, pa_ref, o_ref, vo_ref):
    u = _gelu(u_ref[0])
    v = _layernorm(_gelu(v_ref[0]), lg[...], lb[...])
    vo_ref[...] = v
    mix = v * w00[...] + b00[...]
    ya = u * mix * jax.nn.silu(z_ref[0])
    o_ref[0] = jax.nn.sigmoid(g_ref[0]) * jnp.dot(ya.astype(BF16), pa_ref[...],
                                                  preferred_element_type=F32)


def _branch_a_seq(proj, ln_g, ln_b, w_s, bias2d, p_a, tm=256):
    bsz, t, _ = proj.shape
    tm = min(tm, t)
    full = lambda shape: pl.BlockSpec(shape, lambda b, i: (0,) * len(shape))
    sec = lambda w, c: pl.BlockSpec((1, tm, w), lambda b, i: (b, i, c))
    return pl.pallas_call(
        functools.partial(_branch_a_seq_kernel, tm=tm),
        grid=(bsz, t // tm),
        in_specs=[sec(A_W, COL_UA), sec(A_W, COL_VA), sec(A_W, COL_ZA), sec(D_MODEL, COL_GA),
                  full((1, A_W)), full((1, A_W)), full((A_GROUPS, CHUNK, CHUNK)),
                  full((CHUNK, A_W)), full((A_W, D_MODEL))],
        out_specs=pl.BlockSpec((1, tm, D_MODEL), lambda b, i: (b, i, 0)),
        out_shape=jax.ShapeDtypeStruct((bsz, t, D_MODEL), F32),
        compiler_params=_cparams("arbitrary", "arbitrary"),
        name="branch_a_seq",
    )(proj, proj, proj, proj, ln_g, ln_b, w_s, bias2d, p_a)


def _branch_a_tok(proj, ln_g, ln_b, w00, b00, p_a):
    _, m, _ = proj.shape
    full = lambda shape: pl.BlockSpec(shape, lambda i: (0,) * len(shape))
    sec = lambda w, c: pl.BlockSpec((1, m, w), lambda i: (0, 0, c))
    return pl.pallas_call(
        _branch_a_tok_kernel,
        grid=(1,),
        in_specs=[sec(A_W, COL_UA), sec(A_W, COL_VA), sec(A_W, COL_ZA), sec(D_MODEL, COL_GA),
                  full((1, A_W)), full((1, A_W)), full((1, A_W)), full((1, A_W)),
                  full((A_W, D_MODEL))],
        out_specs=[pl.BlockSpec((1, m, D_MODEL), lambda i: (0, 0, 0)),
                   pl.BlockSpec((m, A_W), lambda i: (0, 0))],
        out_shape=[jax.ShapeDtypeStruct((1, m, D_MODEL), F32),
                   jax.ShapeDtypeStruct((m, A_W), F32)],
        compiler_params=_cparams("arbitrary"),
        name="branch_a_tok",
    )(proj, proj, proj, proj, ln_g, ln_b, w00, b00, p_a)


def _merge_kernel(yb_ref, zb_ref, gb_ref, oa_ref, x_ref, cg_ref, pb_ref, wo_ref, fg_ref, o_ref):
    yb = (yb_ref[0] * jax.nn.silu(zb_ref[0])).astype(BF16)
    merged = oa_ref[0] + jax.nn.sigmoid(gb_ref[0]) * jnp.dot(yb, pb_ref[...],
                                                            preferred_element_type=F32)
    out = x_ref[0] + cg_ref[0] * jnp.dot(merged.astype(BF16), wo_ref[...],
                                         preferred_element_type=F32)
    ms = jnp.mean(out * out, axis=-1, keepdims=True)
    o_ref[0] = out * lax.rsqrt(ms + NORM_EPS) * fg_ref[...]


def _merge(yb, proj, oa, x, c_gate, p_b, w_out, final_g, tm):
    bsz, t, d = x.shape
    tm = min(tm, t)
    tg = c_gate.shape[1]
    row = lambda: pl.BlockSpec((1, tm, d), lambda b, i: (b, i, 0))
    sec = lambda c: pl.BlockSpec((1, tm, d), lambda b, i: (b, i, c))
    gate = (pl.BlockSpec((1, 1, d), lambda b, i: (b, 0, 0)) if tg == 1
            else pl.BlockSpec((1, tm, d), lambda b, i: (b, i, 0)))
    wspec = lambda: pl.BlockSpec((d, d), lambda b, i: (0, 0), pipeline_mode=pl.Buffered(1))
    return pl.pallas_call(
        _merge_kernel,
        grid=(bsz, t // tm),
        in_specs=[row(), sec(COL_ZB), sec(COL_GB), row(), row(), gate, wspec(), wspec(),
                  pl.BlockSpec((1, d), lambda b, i: (0, 0))],
        out_specs=row(),
        out_shape=jax.ShapeDtypeStruct((bsz, t, d), F32),
        compiler_params=_cparams("arbitrary", "arbitrary"),
        name="merge",
    )(yb, proj, proj, oa, x, c_gate, p_b, w_out, final_g.reshape(1, d))


def _split(t, sizes):
    out, o = [], 0
    for s in sizes:
        out.append(t[..., o:o + s])
        o += s
    return out


def _permute_shift_cols(x):
    r, wd, k, v, ad = _split(x, (B_W, LORA, B_W, B_W, LORA))
    pad = jnp.zeros(x.shape[:-1] + (LORA_W - 2 * LORA,), x.dtype)
    return r, k, v, jnp.concatenate([wd, ad, pad], axis=-1)


def _nh_order(x, axis=-1):
    axis = axis % x.ndim
    shape = x.shape
    x = x.reshape(shape[:axis] + (B_HEADS, HEAD_SIZE) + shape[axis + 1:])
    return jnp.swapaxes(x, axis, axis + 1).reshape(shape)


def _hn_order(x, axis=-1):
    axis = axis % x.ndim
    shape = x.shape
    x = x.reshape(shape[:axis] + (HEAD_SIZE, B_HEADS) + shape[axis + 1:])
    return jnp.swapaxes(x, axis, axis + 1).reshape(shape)


def _shift_row(proj_row):
    r = _hn_order(proj_row[..., 0:B_W])
    k = _hn_order(proj_row[..., B_W:2 * B_W])
    v = _hn_order(proj_row[..., 2 * B_W:3 * B_W])
    lo = proj_row[..., COL_LORA * LORA_W:]
    return jnp.concatenate([r, lo[..., :LORA], k, v, lo[..., LORA:2 * LORA]], axis=-1)


def kernel(x_prompt, x_sample, c_prompt, c_sample, state_wkv, state_shift, norm_g, w_c, b_c, w_in, ln_v_g, ln_v_b, w_s, b_s, mu_shift, w0, w2, a0, a2, k_k, k_a, r_k, gn_g, gn_b, p_a, p_b, w_out, final_g):
    assert norm_g.shape[0] == 1, "single-layer trunk"
    bp, t, d = x_prompt.shape
    bs = x_sample.shape[0]

    w_rows = jnp.swapaxes(w_in[0], 0, 1)
    bounds = np.cumsum([0, A_W, A_W, A_W, B_W, LORA, B_W, B_W, LORA, B_W, D_MODEL, D_MODEL])
    ua, va, za, wr, wwd, wk, wv, wad, zb, ga, gb = (w_rows[a:b] for a, b in zip(bounds[:-1], bounds[1:]))
    wpad = jnp.zeros((LORA_W - 2 * LORA, d), F32)
    nh0 = lambda x: _nh_order(x, axis=0)
    w_p = jnp.concatenate([nh0(wr), nh0(wk), nh0(wv), nh0(zb), ga, gb, ua, va, za, wwd, wad, wpad],
                          axis=0).astype(BF16)

    def shift_cols(x):
        xr, xk, xv, xlo = _permute_shift_cols(x)
        return _nh_order(xr), _nh_order(xk), _nh_order(xv), xlo

    mu = [m.reshape(1, -1) for m in shift_cols(mu_shift[0])]
    zrow = jnp.zeros((LORA_W - LORA, B_W), F32)
    w2p = jnp.concatenate([_nh_order(w2[0]), zrow], axis=0).astype(BF16)
    a2p = jnp.concatenate([jnp.zeros((LORA, B_W), F32), _nh_order(a2[0]),
                           jnp.zeros((LORA_W - 2 * LORA, B_W), F32)], axis=0).astype(BF16)
    w0r, a0r = _nh_order(w0[0]).reshape(1, B_W), _nh_order(a0[0]).reshape(1, B_W)
    p_a_b, w_out_b = p_a[0].astype(BF16), w_out[0].astype(BF16)
    p_b_b = nh0(p_b[0]).astype(BF16)
    ln_g, ln_b = ln_v_g[0].reshape(1, A_W), ln_v_b[0].reshape(1, A_W)
    bias2d = jnp.repeat(b_s[0].T, CHUNK, axis=1)
    w00 = jnp.repeat(w_s[0][:, 0, 0], CHUNK).reshape(1, A_W)
    b00 = jnp.repeat(b_s[0][:, 0], CHUNK).reshape(1, A_W)

    m_rows = bp + bs
    m_pad = -m_rows % V7X_SUBLANES
    c_all = jnp.concatenate([c_prompt, c_sample, jnp.zeros((m_pad, d), F32)], axis=0)
    mod = _modulation(c_all, w_c[0], b_c[0])
    shift_p, scale_p, gate_p = (mod[:bp, i * d:(i + 1) * d].reshape(bp, 1, d) for i in range(3))
    shift_s, scale_s, gate_s = (mod[bp:m_rows, i * d:(i + 1) * d].reshape(1, bs, d) for i in range(3))

    def wkv_consts(bsz):
        return [_head_const(c, bsz) for c in (k_k[0], k_a[0], r_k[0], gn_g[0], gn_b[0])]

    xs = x_sample.reshape(1, bs, d)
    h_p = _norm_modulate(x_prompt, norm_g[0], scale_p, shift_p, tm=512)
    h_s = _norm_modulate(xs, norm_g[0], scale_s, shift_s, tm=bs)
    proj_p, proj_s = _in_projection(h_p, h_s[0], w_p)
    proj_s = proj_s[None]

    zeros_shift = [jnp.zeros((bp, 1, w), F32) for w in (B_W, B_W, B_W, LORA_W)]
    feats = _rwkv_prep_seq(proj_p, zeros_shift, mu, w0r, a0r, w2p, a2p)
    s0_p = jnp.zeros((HEAD_SIZE, HEAD_SIZE, bp * B_HEADS), F32)
    yb_p, s_hl = _wkv(*feats, s0_p, wkv_consts(bp), tt=32)
    oa_p = _branch_a_seq(proj_p, ln_g, ln_b, w_s[0], bias2d, p_a_b)
    y_prompt = _merge(yb_p, proj_p, oa_p, x_prompt, gate_p, p_b_b, w_out_b, final_g, tm=256)
    wkv_prompt = jnp.transpose(s_hl.reshape(HEAD_SIZE, HEAD_SIZE, bp, B_HEADS), (2, 3, 1, 0))[None]
    shift_prompt = _shift_row(proj_p[:, t - 1, :])[None]

    feats = _rwkv_prep_tok(proj_s, shift_cols(state_shift[0]), mu, w0r, a0r, w2p, a2p)
    feats = [jnp.transpose(f.reshape(bs, HEAD_SIZE, B_HEADS), (2, 1, 0)) for f in feats]
    tok_consts = [jnp.broadcast_to(c.reshape(B_HEADS, HEAD_SIZE, 1), (B_HEADS, HEAD_SIZE, bs))
                  for c in (k_k[0], k_a[0], r_k[0], gn_g[0], gn_b[0])]
    yb_s, wkv_s = _wkv_tok(*feats, jnp.transpose(state_wkv[0], (1, 2, 3, 0)), tok_consts)
    yb_s = jnp.transpose(yb_s, (2, 1, 0)).reshape(1, bs, B_W)
    oa_s, v_s = _branch_a_tok(proj_s, ln_g, ln_b, w00, b00, p_a_b)
    y_s = _merge(yb_s, proj_s, oa_s, xs, gate_s, p_b_b, w_out_b, final_g, tm=bs)
    y_sample = y_s.reshape(bs, 1, d)
    wkv_sample = jnp.transpose(wkv_s, (3, 0, 1, 2))[None]
    shift_sample = _shift_row(proj_s[0])[None]
    chunk_v_sample = v_s.reshape(1, bs, 1, A_W)

    return (y_prompt, y_sample, wkv_prompt, shift_prompt, wkv_sample, shift_sample, chunk_v_sample)
```

```python
import functools
import math

import jax
import jax.numpy as jnp
import numpy as np
from jax import lax
from jax.experimental import pallas as pl
from jax.experimental.pallas import tpu as pltpu

F32 = jnp.float32
BF16 = jnp.bfloat16

D_MODEL = 2048
A_W = 1024
A_GROUPS = 8
CHUNK = 128
B_HEADS = 32
HEAD_SIZE = 64
B_W = B_HEADS * HEAD_SIZE
LORA = 96
NORM_EPS = 1e-6
GN_EPS = HEAD_SIZE * 1e-5
DECAY_SCALE = math.exp(-0.5)

LORA_W = 512
PROJ_W = 6 * B_W + 3 * A_W + LORA_W
COL_R, COL_K, COL_V, COL_ZB, COL_GA, COL_GB = 0, 1, 2, 3, 4, 5
COL_UA, COL_VA, COL_ZA = 12, 13, 14
COL_LORA = (6 * B_W + 3 * A_W) // LORA_W

V7X_LANES = 128
V7X_SUBLANES = 8
LANE_GROUP = B_HEADS
VMEM_LIMIT = 56 * 1024 * 1024


def _cparams(*sem):
    return pltpu.CompilerParams(dimension_semantics=sem, vmem_limit_bytes=VMEM_LIMIT)


def _mod_kernel(c_ref, w_ref, b_ref, o_ref):
    acc = jnp.dot(c_ref[...].astype(BF16), w_ref[...].astype(BF16), preferred_element_type=F32)
    o_ref[...] = acc + b_ref[...]


def _modulation(c_all, w_c, b_c):
    m, d = c_all.shape
    n = w_c.shape[1]
    tn = 768
    return pl.pallas_call(
        _mod_kernel,
        grid=(n // tn,),
        in_specs=[pl.BlockSpec((m, d), lambda j: (0, 0)),
                  pl.BlockSpec((d, tn), lambda j: (0, j)),
                  pl.BlockSpec((1, tn), lambda j: (0, j))],
        out_specs=pl.BlockSpec((m, tn), lambda j: (0, j)),
        out_shape=jax.ShapeDtypeStruct((m, n), F32),
        compiler_params=_cparams("arbitrary"),
        name="modulation",
    )(c_all, w_c, b_c.reshape(1, n))


def _normmod_kernel(x_ref, g_ref, sc_ref, sh_ref, o_ref):
    x = x_ref[0]
    ms = jnp.mean(x * x, axis=-1, keepdims=True)
    y = x * lax.rsqrt(ms + NORM_EPS) * g_ref[...]
    o_ref[0] = (y * (1.0 + sc_ref[0]) + sh_ref[0]).astype(BF16)


def _norm_modulate(x, g, scale, shift, tm):
    bsz, t, d = x.shape
    tm = min(tm, t)
    ts = scale.shape[1]
    sspec = (pl.BlockSpec((1, 1, d), lambda b, i: (b, 0, 0)) if ts == 1
             else pl.BlockSpec((1, tm, d), lambda b, i: (b, i, 0)))
    return pl.pallas_call(
        _normmod_kernel,
        grid=(bsz, t // tm),
        in_specs=[pl.BlockSpec((1, tm, d), lambda b, i: (b, i, 0)),
                  pl.BlockSpec((1, d), lambda b, i: (0, 0)),
                  sspec, sspec],
        out_specs=pl.BlockSpec((1, tm, d), lambda b, i: (b, i, 0)),
        out_shape=jax.ShapeDtypeStruct((bsz, t, d), BF16),
        compiler_params=_cparams("arbitrary", "arbitrary"),
        name="norm_modulate",
    )(x, g.reshape(1, d), scale, shift)


def _inproj_kernel(h_ref, hs_ref, w_ref, o_ref, os_ref):
    nt = (((1,), (1,)), ((), ()))
    o_ref[0] = lax.dot_general(h_ref[0], w_ref[...], nt, preferred_element_type=F32)

    @pl.when(pl.program_id(0) == 0)
    def _():
        os_ref[...] = lax.dot_general(hs_ref[...], w_ref[...], nt, preferred_element_type=F32)


def _in_projection(h, h_tok, w_t, tn=512):
    bsz, t, d = h.shape
    m = h_tok.shape[0]
    n = w_t.shape[0]
    nj = n // tn
    tok_out = pl.BlockSpec((m, tn), lambda b, j: (0, jnp.where(b == 0, j, nj - 1)))
    return pl.pallas_call(
        _inproj_kernel,
        grid=(bsz, nj),
        in_specs=[pl.BlockSpec((1, t, d), lambda b, j: (b, 0, 0)),
                  pl.BlockSpec((m, d), lambda b, j: (0, 0)),
                  pl.BlockSpec((tn, d), lambda b, j: (j, 0))],
        out_specs=[pl.BlockSpec((1, t, tn), lambda b, j: (b, 0, j)), tok_out],
        out_shape=[jax.ShapeDtypeStruct((bsz, t, n), F32), jax.ShapeDtypeStruct((m, n), F32)],
        compiler_params=_cparams("arbitrary", "arbitrary"),
        name="in_projection",
    )(h, h_tok, w_t)


def _lerp(p, q, mu):
    return p + mu * (q - p)


def _lora_logits(lo, w0, a0, w2p, a2p):
    col = lax.broadcasted_iota(jnp.int32, lo.shape, 1)
    lt = jnp.where(col < LORA, jnp.tanh(lo), lo).astype(BF16)
    wl = w0 + jnp.dot(lt, w2p, preferred_element_type=F32)
    al = a0 + jnp.dot(lt, a2p, preferred_element_type=F32)
    return wl, al


def _decay(wl):
    return jnp.exp(-DECAY_SCALE * jax.nn.sigmoid(wl))


def _prep_core(p, q, mu, w0, a0, w2p, a2p):
    r, k, v, lo = (_lerp(pi, qi, mi) for pi, qi, mi in zip(p, q, mu))
    wl, al = _lora_logits(lo, w0, a0, w2p, a2p)
    return r, _decay(wl), k, v, jax.nn.sigmoid(al)


def _move_lane_group(x, src_g, dst_g):
    shift = ((dst_g - src_g) % (V7X_LANES // LANE_GROUP)) * LANE_GROUP
    return x if shift == 0 else pltpu.roll(x, shift, axis=1)


def _swap_lane_groups(x):
    lane = lax.broadcasted_iota(jnp.int32, x[0].shape, 1)
    low_half = lane < 2 * LANE_GROUP
    even_group = (lane // LANE_GROUP) % 2 == 0
    rot = lambda v, s: pltpu.roll(v, s, axis=1)
    y0 = jnp.where(low_half, x[0], rot(x[2], 2 * LANE_GROUP))
    y2 = jnp.where(low_half, rot(x[0], 2 * LANE_GROUP), x[2])
    y1 = jnp.where(low_half, x[1], rot(x[3], 2 * LANE_GROUP))
    y3 = jnp.where(low_half, rot(x[1], 2 * LANE_GROUP), x[3])
    return [jnp.where(even_group, y0, rot(y1, LANE_GROUP)),
            jnp.where(even_group, rot(y0, 3 * LANE_GROUP), y1),
            jnp.where(even_group, y2, rot(y3, LANE_GROUP)),
            jnp.where(even_group, rot(y2, 3 * LANE_GROUP), y3)]


def _prep_seq_kernel(pr, pk, pv, plo, tr, tk, tv, tlo, sr, sk, sv, slo,
                     mr, mk, mv, mlo, w0, a0, w2p, a2p,
                     o_r, o_w, o_k, o_v, o_a):
    i = pl.program_id(1)

    def prev(p_ref, tail_ref, s_ref):
        p = p_ref[0]
        last = tail_ref[0][V7X_SUBLANES - 1:V7X_SUBLANES, :]
        first = jnp.where(i == 0, s_ref[0], last)
        rolled = pltpu.roll(p, 1, axis=0)
        row = lax.broadcasted_iota(jnp.int32, p.shape, 0)
        return p, jnp.where(row == 0, first, rolled)

    pq = [prev(x, y, z) for x, y, z in ((pr, tr, sr), (pk, tk, sk), (pv, tv, sv), (plo, tlo, slo))]
    outs = _prep_core([x[0] for x in pq], [x[1] for x in pq],
                      (mr[...], mk[...], mv[...], mlo[...]), w0[...], a0[...], w2p[...], a2p[...])
    for o, val in zip((o_r, o_w, o_k, o_v, o_a), outs):
        for q in range(B_W // V7X_LANES):
            o[0, q] = val[:, q * V7X_LANES:(q + 1) * V7X_LANES]


def _prep_tok_kernel(pr, pk, pv, plo, sr, sk, sv, slo,
                     mr, mk, mv, mlo, w0, a0, w2p, a2p,
                     o_r, o_w, o_k, o_v, o_a):
    outs = _prep_core((pr[0], pk[0], pv[0], plo[0]), (sr[...], sk[...], sv[...], slo[...]),
                      (mr[...], mk[...], mv[...], mlo[...]), w0[...], a0[...], w2p[...], a2p[...])
    for o, val in zip((o_r, o_w, o_k, o_v, o_a), outs):
        o[0] = val


def _rwkv_prep_seq(proj, shift0, mu, w0, a0, w2p, a2p, tm=128):
    bsz, t, _ = proj.shape
    tail = tm // V7X_SUBLANES
    nq = B_W // V7X_LANES
    cols = ((B_W, COL_R), (B_W, COL_K), (B_W, COL_V), (LORA_W, COL_LORA))
    cur = [pl.BlockSpec((1, tm, w), functools.partial(lambda b, i, c: (b, i, c), c=c)) for w, c in cols]
    tails = [pl.BlockSpec((1, V7X_SUBLANES, w),
                          functools.partial(lambda b, i, c: (b, jnp.maximum(i * tail - 1, 0), c), c=c))
             for w, c in cols]
    s0 = [pl.BlockSpec((1, 1, w), lambda b, i: (b, 0, 0)) for w, _ in cols]
    rowv = [pl.BlockSpec((1, w), lambda b, i: (0, 0)) for w, _ in cols]
    full = lambda shape: pl.BlockSpec(shape, lambda b, i: (0,) * len(shape))
    out_spec = pl.BlockSpec((1, nq, tm, V7X_LANES), lambda b, i: (b, 0, i, 0))
    return pl.pallas_call(
        _prep_seq_kernel,
        grid=(bsz, t // tm),
        in_specs=cur + tails + s0 + rowv + [full((1, B_W)), full((1, B_W)),
                                            full((LORA_W, B_W)), full((LORA_W, B_W))],
        out_specs=[out_spec] * 5,
        out_shape=[jax.ShapeDtypeStruct((bsz, nq, t, V7X_LANES), F32)] * 5,
        compiler_params=_cparams("arbitrary", "arbitrary"),
        name="rwkv_prep_seq",
    )(proj, proj, proj, proj, proj, proj, proj, proj, *shift0, *mu, w0, a0, w2p, a2p)


def _rwkv_prep_tok(proj, shift0, mu, w0, a0, w2p, a2p):
    _, m, _ = proj.shape
    cols = ((B_W, COL_R), (B_W, COL_K), (B_W, COL_V), (LORA_W, COL_LORA))
    cur = [pl.BlockSpec((1, m, w), functools.partial(lambda i, c: (0, 0, c), c=c)) for w, c in cols]
    s0 = [pl.BlockSpec((m, w), lambda i: (0, 0)) for w, _ in cols]
    rowv = [pl.BlockSpec((1, w), lambda i: (0, 0)) for w, _ in cols]
    full = lambda shape: pl.BlockSpec(shape, lambda i: (0,) * len(shape))
    out_spec = pl.BlockSpec((1, m, B_W), lambda i: (0, 0, 0))
    return pl.pallas_call(
        _prep_tok_kernel,
        grid=(1,),
        in_specs=cur + s0 + rowv + [full((1, B_W)), full((1, B_W)),
                                    full((LORA_W, B_W)), full((LORA_W, B_W))],
        out_specs=[out_spec] * 5,
        out_shape=[jax.ShapeDtypeStruct((1, m, B_W), F32)] * 5,
        compiler_params=_cparams("arbitrary"),
        name="rwkv_prep_tok",
    )(proj, proj, proj, proj, *shift0, *mu, w0, a0, w2p, a2p)


WKV_ROWS = 16


AHEAD = 2


def _wkv_kernel(xr, xw, xk, xv, xa, hr, hw, hk, hv, ha, s0_ref, kk_c, ka_c, rk_c, gg_c, gb_c,
                y_ref, s_ref, r_s, w_s, k_s, a_s, vn_s, vt_s, nkk_s, b_s, kp_s,
                p1_s, p2_s, p3_s, p4_s, p5_s, rks_s, cba_s, cka_s, cbr_s, ckr_s, ys_s,
                *, tt, nb):
    step = pl.program_id(0)
    ng = tt // V7X_SUBLANES
    pitch = tt + V7X_SUBLANES
    vpitch = HEAD_SIZE + V7X_SUBLANES

    @pl.when(step == 0)
    def _():
        s_ref[...] = s0_ref[...]

    dsts = (r_s, w_s, k_s, a_s, vn_s)

    def group_rows(rg):
        start = rg * V7X_SUBLANES
        if not isinstance(rg, int):
            start = pl.multiple_of(start, V7X_SUBLANES)
        return pl.ds(start, V7X_SUBLANES)

    def exchange(srcs, src_rg, dst_rg, q):
        src_rows = group_rows(src_rg)
        dst_rows = group_rows(dst_rg)
        for x, dst in zip(srcs, dsts):
            for b in range(nb):
                tile = x[b, q, src_rows, :]
                lanes = slice(b * LANE_GROUP, (b + 1) * LANE_GROUP)
                for g in range(nb):
                    dst[nb * q + g, dst_rows, lanes] = _move_lane_group(tile, g, b)[:, lanes]

    def v_by_token(rg, q):
        rows = group_rows(rg)
        for g in range(nb):
            n = nb * q + g
            vt_s[pl.ds(rg * V7X_SUBLANES * vpitch + n, V7X_SUBLANES, stride=vpitch), :] = vn_s[n, rows, :]

    def prep(m):
        ts = group_rows(m)
        sq = None
        for n in range(HEAD_SIZE):
            kk = k_s[n, ts, :] * kk_c[n]
            sq = kk * kk if sq is None else sq + kk * kk
        den = jnp.maximum(jnp.sqrt(sq), 1e-12)
        nxt_tok = lambda x: pltpu.roll(x, V7X_SUBLANES - 1, axis=0)
        acc = [None] * 5
        for n in range(HEAD_SIZE):
            k = k_s[n, ts, :]
            a = a_s[n, ts, :]
            r = r_s[n, ts, :]
            w = w_s[n, ts, :]
            kk = (k * kk_c[n]) / den
            nkk = -kk
            b = kk * a
            kp = k * (1.0 + (a - 1.0) * ka_c[n])
            nkk_n = nxt_tok(nkk)
            w_n = nxt_tok(w)
            nkk_s[n, ts, :] = nkk
            b_s[n, ts, :] = b
            kp_s[n, ts, :] = kp
            p1_s[n, ts, :] = w * r
            p2_s[n, ts, :] = w * w_n
            p3_s[n, ts, :] = b * w_n
            p4_s[n, ts, :] = kp * w_n
            p5_s[n, ts, :] = w * nkk_n
            kpr = kp * r
            terms = (b * nkk_n, kp * nkk_n, b * r, kpr, kpr * rk_c[n])
            acc = [x if y is None else y + x for x, y in zip(terms, acc)]
        for ref, val in zip((cba_s, cka_s, cbr_s, ckr_s, rks_s), acc):
            ref[ts, :] = val

    cur = (xr, xw, xk, xa, xv)
    nxt = (hr, hw, hk, ha, hv)

    @pl.when(step == 0)
    def _():
        for rg in range(AHEAD):
            for q in range(B_W // V7X_LANES):
                exchange(cur, rg, rg, q)
                v_by_token(rg, q)

    @pl.when(step > 0)
    def _():
        head = AHEAD * V7X_SUBLANES
        for dst in dsts:
            dst[:, 0:head, :] = dst[:, tt:tt + head, :]
        vt_s[0:head * vpitch, :] = vt_s[tt * vpitch:(tt + head) * vpitch, :]

    prep(0)

    parts = [h * WKV_ROWS for h in range(HEAD_SIZE // WKV_ROWS)]
    zeros = lambda: jnp.zeros((WKV_ROWS, V7X_LANES), F32)

    def bcast(ref, t, j):
        return jnp.broadcast_to(ref[j, pl.ds(t, 1), :], (WKV_ROWS, V7X_LANES))

    def lane_row(ref, t):
        return jnp.broadcast_to(ref[pl.ds(t, 1), :], (WKV_ROWS, V7X_LANES))

    def first_carry(i0):
        ua, xb = zeros(), zeros()
        for j in range(HEAD_SIZE):
            s = s_ref[j, i0:i0 + WKV_ROWS, :]
            ua = ua + s * bcast(nkk_s, 0, j)
            xb = xb + s * bcast(p5_s, 0, j)
        return ua, xb

    def group(m, carry, srcs, src_rg):
        if m + 1 < ng:
            prep(m + 1)
        pairs = V7X_SUBLANES // 2

        def pair(p, carry):
            ta = m * V7X_SUBLANES + 2 * p
            tb = ta + 1
            tc = jnp.minimum(ta + 2, tt - 2)
            out = []
            for i0, (ua, xb) in zip(parts, carry):
                va = vt_s[pl.ds(pl.multiple_of(ta * vpitch + i0, V7X_SUBLANES), WKV_ROWS), :]
                vb = vt_s[pl.ds(pl.multiple_of(tb * vpitch + i0, V7X_SUBLANES), WKV_ROWS), :]
                ub = (xb + ua * lane_row(cba_s, ta)) + va * lane_row(cka_s, ta)
                ya, yb, uc, xd = zeros(), zeros(), zeros(), zeros()
                for j in range(HEAD_SIZE):
                    s = s_ref[j, i0:i0 + WKV_ROWS, :]
                    ya = ya + s * bcast(p1_s, ta, j)
                    sn = ((((s * bcast(p2_s, ta, j) + ua * bcast(p3_s, ta, j)) + va * bcast(p4_s, ta, j))
                           + ub * bcast(b_s, tb, j)) + vb * bcast(kp_s, tb, j))
                    s_ref[j, i0:i0 + WKV_ROWS, :] = sn
                    yb = yb + sn * bcast(r_s, tb, j)
                    uc = uc + sn * bcast(nkk_s, tc, j)
                    xd = xd + sn * bcast(p5_s, tc, j)
                ya = (ya + ua * lane_row(cbr_s, ta)) + va * lane_row(ckr_s, ta)
                ys_s[pl.ds(i0 * pitch + ta, WKV_ROWS, stride=pitch), :] = ya
                ys_s[pl.ds(i0 * pitch + tb, WKV_ROWS, stride=pitch), :] = yb
                out.append((uc, xd))
            prev = m * pairs + p + pairs - 1
            prev_rg = lax.shift_right_logical(prev, 2) - 1 + AHEAD
            prev_q = 4 * (prev & (pairs - 1))
            for d in range(4):
                v_by_token(prev_rg, prev_q + d)
                exchange(srcs, src_rg, m + AHEAD, 4 * p + d)
            return tuple(out)

        return lax.fori_loop(0, pairs, pair, carry)

    carry = tuple(first_carry(i0) for i0 in parts)
    for m in range(ng):
        carry = (group(m, carry, cur, m + AHEAD) if m + AHEAD < ng
                 else group(m, carry, nxt, m + AHEAD - ng))
    for d in range(4):
        v_by_token(ng - 1 + AHEAD, B_W // V7X_LANES - 4 + d)

    def post(c, carry):
        t0 = pl.multiple_of(c * V7X_SUBLANES, V7X_SUBLANES)
        ys = [ys_s[pl.ds(i * pitch + t0, V7X_SUBLANES), :] for i in range(HEAD_SIZE)]
        mu = sum(ys[1:], ys[0]) * (1.0 / HEAD_SIZE)
        yc = [y - mu for y in ys]
        sq = [y * y for y in yc]
        rstd = lax.rsqrt(sum(sq[1:], sq[0]) * (1.0 / HEAD_SIZE) + GN_EPS)
        scale = rks_s[pl.ds(t0, V7X_SUBLANES), :]
        out = [yc[i] * rstd * gg_c[i:i + 1, :] + gb_c[i:i + 1, :]
               + scale * vn_s[i, pl.ds(t0, V7X_SUBLANES), :] for i in range(HEAD_SIZE)]
        for q in range(HEAD_SIZE // nb):
            nat = _swap_lane_groups(out[nb * q:nb * (q + 1)])
            for b in range(nb):
                y_ref[b, pl.ds(t0, V7X_SUBLANES), q * V7X_LANES:(q + 1) * V7X_LANES] = nat[b]
        return carry

    lax.fori_loop(0, ng, post, 0, unroll=2)


def _wkv(r, w, k, v, a, s0, consts, tt):
    nb, nq, t, lanes = r.shape
    n = HEAD_SIZE
    assert lanes == V7X_LANES and nb * B_HEADS == lanes and nq * lanes == B_W
    head = AHEAD * V7X_SUBLANES
    assert tt % head == 0 and tt // V7X_SUBLANES > AHEAD
    seq = pl.BlockSpec((nb, nq, tt, lanes), lambda i: (0, 0, i, 0))
    nxt = pl.BlockSpec((nb, nq, head, lanes),
                       lambda i: (0, 0, jnp.minimum((i + 1) * (tt // head), t // head - 1), 0))
    st = pl.BlockSpec((n, n, lanes), lambda i: (0, 0, 0))
    c3 = pl.BlockSpec((n, 1, lanes), lambda i: (0, 0, 0))
    c2 = pl.BlockSpec((n, lanes), lambda i: (0, 0))
    kk_c, ka_c, rk_c, gg_c, gb_c = consts
    col = lambda c: c.reshape(n, 1, lanes)
    pitch = tt + V7X_SUBLANES
    by_n = pltpu.VMEM((n, tt, lanes), F32)
    lead_n = pltpu.VMEM((n, tt + head, lanes), F32)
    return pl.pallas_call(
        functools.partial(_wkv_kernel, tt=tt, nb=nb),
        grid=(t // tt,),
        in_specs=[seq] * 5 + [nxt] * 5
        + [pl.BlockSpec((n, n, lanes), lambda i: (0, 0, 0), pipeline_mode=pl.Buffered(1)),
           c3, c3, c3, c2, c2],
        out_specs=[pl.BlockSpec((nb, tt, B_W), lambda i: (0, i, 0)), st],
        out_shape=[jax.ShapeDtypeStruct((nb, t, B_W), F32),
                   jax.ShapeDtypeStruct((n, n, lanes), F32)],
        scratch_shapes=[lead_n] * 5
        + [pltpu.VMEM(((tt + head) * (n + V7X_SUBLANES), lanes), F32)] + [by_n] * 8
        + [pltpu.VMEM((tt, lanes), F32)] * 5 + [pltpu.VMEM((n * pitch, lanes), F32)],
        compiler_params=_cparams("arbitrary"),
        name="wkv",
    )(r, w, k, v, a, r, w, k, v, a, s0, col(kk_c), col(ka_c), col(rk_c), gg_c, gb_c)


def _wkv_tok_kernel(r_ref, w_ref, k_ref, v_ref, a_ref, s0_ref, kk_c, ka_c, rk_c, gg_c, gb_c,
                    y_ref, s_ref, ys_s):
    k = k_ref[0]
    a = a_ref[0]
    r = r_ref[0]
    w = w_ref[0]
    v = v_ref[0]
    kk = k * kk_c[0]
    nrm = jnp.sqrt(jnp.sum(kk * kk, axis=0, keepdims=True))
    kk = kk / jnp.maximum(nrm, 1e-12)
    nkk = -kk
    bvec = kk * a
    kp = k * (1.0 + (a - 1.0) * ka_c[0])
    wr = w * r
    br = jnp.sum(bvec * r, axis=0, keepdims=True)
    kr = jnp.sum(kp * r, axis=0, keepdims=True)
    for i in range(HEAD_SIZE):
        s = s0_ref[0, i]
        u = jnp.sum(s * nkk, axis=0, keepdims=True)
        yw = jnp.sum(s * wr, axis=0, keepdims=True)
        vi = v[i:i + 1, :]
        s_ref[0, i] = (s * w + u * bvec) + vi * kp
        ys_s[i:i + 1, :] = (yw + u * br) + vi * kr
    y = ys_s[...]
    mu = jnp.mean(y, axis=0, keepdims=True)
    yc = y - mu
    var = jnp.mean(yc * yc, axis=0, keepdims=True)
    yn = yc * lax.rsqrt(var + GN_EPS) * gg_c[0] + gb_c[0]
    bonus = jnp.sum(r * kp * rk_c[0], axis=0, keepdims=True) * v
    y_ref[0] = yn + bonus


def _wkv_tok(r, w, k, v, a, s0, consts):
    nh, n, bsz = r.shape
    vec = pl.BlockSpec((1, n, bsz), lambda h: (h, 0, 0))
    st = pl.BlockSpec((1, n, n, bsz), lambda h: (h, 0, 0, 0))
    return pl.pallas_call(
        _wkv_tok_kernel,
        grid=(nh,),
        in_specs=[vec] * 5 + [st] + [vec] * 5,
        out_specs=[vec, st],
        out_shape=[jax.ShapeDtypeStruct((nh, n, bsz), F32),
                   jax.ShapeDtypeStruct(s0.shape, F32)],
        scratch_shapes=[pltpu.VMEM((n, bsz), F32)],
        compiler_params=_cparams("arbitrary"),
        name="wkv_tok",
    )(r, w, k, v, a, s0, *consts)


def _head_const(x, bsz):
    return jnp.tile(x.reshape(B_HEADS, HEAD_SIZE).T, (1, bsz))


def _gelu(x):
    return jax.nn.gelu(x)


def _layernorm(x, g, b):
    mu = jnp.mean(x, axis=-1, keepdims=True)
    xc = x - mu
    var = jnp.mean(xc * xc, axis=-1, keepdims=True)
    return xc * lax.rsqrt(var + NORM_EPS) * g + b


def _branch_a_seq_kernel(u_ref, v_ref, z_ref, g_ref, lg, lb, ws_ref, bias_ref, pa_ref, o_ref, *, tm):
    u = _gelu(u_ref[0])
    v = _layernorm(_gelu(v_ref[0]), lg[...], lb[...])
    vb = v.astype(BF16)
    row = lax.broadcasted_iota(jnp.int32, (CHUNK, CHUNK), 0)
    col = lax.broadcasted_iota(jnp.int32, (CHUNK, CHUNK), 1)
    rows = []
    for c in range(tm // CHUNK):
        blocks = []
        for g in range(A_GROUPS):
            wm = jnp.where(row >= col, ws_ref[g], 0.0).astype(BF16)
            blk = vb[c * CHUNK:(c + 1) * CHUNK, g * CHUNK:(g + 1) * CHUNK]
            blocks.append(jnp.dot(wm, blk, preferred_element_type=F32))
        rows.append(jnp.concatenate(blocks, axis=1) + bias_ref[...])
    mix = jnp.concatenate(rows, axis=0)
    ya = u * mix * jax.nn.silu(z_ref[0])
    o_ref[0] = jax.nn.sigmoid(g_ref[0]) * jnp.dot(ya.astype(BF16), pa_ref[...],
                                                  preferred_element_type=F32)


def _branch_a_tok_kernel(u_ref, v_ref, z_ref, g_ref, lg, lb, w00, b00, pa_ref, o_ref, vo_ref):
    u = _gelu(u_ref[0])
    v = _layernorm(_gelu(v_ref[0]), lg[...], lb[...])
    vo_ref[...] = v
    mix = v * w00[...] + b00[...]
    ya = u * mix * jax.nn.silu(z_ref[0])
    o_ref[0] = jax.nn.sigmoid(g_ref[0]) * jnp.dot(ya.astype(BF16), pa_ref[...],
                                                  preferred_element_type=F32)


def _branch_a_seq(proj, ln_g, ln_b, w_s, bias2d, p_a, tm=256):
    bsz, t, _ = proj.shape
    tm = min(tm, t)
    full = lambda shape: pl.BlockSpec(shape, lambda b, i: (0,) * len(shape))
    sec = lambda w, c: pl.BlockSpec((1, tm, w), lambda b, i: (b, i, c))
    return pl.pallas_call(
        functools.partial(_branch_a_seq_kernel, tm=tm),
        grid=(bsz, t // tm),
        in_specs=[sec(A_W, COL_UA), sec(A_W, COL_VA), sec(A_W, COL_ZA), sec(D_MODEL, COL_GA),
                  full((1, A_W)), full((1, A_W)), full((A_GROUPS, CHUNK, CHUNK)),
                  full((CHUNK, A_W)), full((A_W, D_MODEL))],
        out_specs=pl.BlockSpec((1, tm, D_MODEL), lambda b, i: (b, i, 0)),
        out_shape=jax.ShapeDtypeStruct((bsz, t, D_MODEL), F32),
        compiler_params=_cparams("arbitrary", "arbitrary"),
        name="branch_a_seq",
    )(proj, proj, proj, proj, ln_g, ln_b, w_s, bias2d, p_a)


def _branch_a_tok(proj, ln_g, ln_b, w00, b00, p_a):
    _, m, _ = proj.shape
    full = lambda shape: pl.BlockSpec(shape, lambda i: (0,) * len(shape))
    sec = lambda w, c: pl.BlockSpec((1, m, w), lambda i: (0, 0, c))
    return pl.pallas_call(
        _branch_a_tok_kernel,
        grid=(1,),
        in_specs=[sec(A_W, COL_UA), sec(A_W, COL_VA), sec(A_W, COL_ZA), sec(D_MODEL, COL_GA),
                  full((1, A_W)), full((1, A_W)), full((1, A_W)), full((1, A_W)),
                  full((A_W, D_MODEL))],
        out_specs=[pl.BlockSpec((1, m, D_MODEL), lambda i: (0, 0, 0)),
                   pl.BlockSpec((m, A_W), lambda i: (0, 0))],
        out_shape=[jax.ShapeDtypeStruct((1, m, D_MODEL), F32),
                   jax.ShapeDtypeStruct((m, A_W), F32)],
        compiler_params=_cparams("arbitrary"),
        name="branch_a_tok",
    )(proj, proj, proj, proj, ln_g, ln_b, w00, b00, p_a)


def _merge_kernel(yb_ref, zb_ref, gb_ref, oa_ref, x_ref, cg_ref, pb_ref, wo_ref, fg_ref, o_ref):
    yb = (yb_ref[0] * jax.nn.silu(zb_ref[0])).astype(BF16)
    merged = oa_ref[0] + jax.nn.sigmoid(gb_ref[0]) * jnp.dot(yb, pb_ref[...],
                                                            preferred_element_type=F32)
    out = x_ref[0] + cg_ref[0] * jnp.dot(merged.astype(BF16), wo_ref[...],
                                         preferred_element_type=F32)
    ms = jnp.mean(out * out, axis=-1, keepdims=True)
    o_ref[0] = out * lax.rsqrt(ms + NORM_EPS) * fg_ref[...]


def _merge(yb, proj, oa, x, c_gate, p_b, w_out, final_g, tm):
    bsz, t, d = x.shape
    tm = min(tm, t)
    tg = c_gate.shape[1]
    row = lambda: pl.BlockSpec((1, tm, d), lambda b, i: (b, i, 0))
    sec = lambda c: pl.BlockSpec((1, tm, d), lambda b, i: (b, i, c))
    gate = (pl.BlockSpec((1, 1, d), lambda b, i: (b, 0, 0)) if tg == 1
            else pl.BlockSpec((1, tm, d), lambda b, i: (b, i, 0)))
    wspec = lambda: pl.BlockSpec((d, d), lambda b, i: (0, 0), pipeline_mode=pl.Buffered(1))
    return pl.pallas_call(
        _merge_kernel,
        grid=(bsz, t // tm),
        in_specs=[row(), sec(COL_ZB), sec(COL_GB), row(), row(), gate, wspec(), wspec(),
                  pl.BlockSpec((1, d), lambda b, i: (0, 0))],
        out_specs=row(),
        out_shape=jax.ShapeDtypeStruct((bsz, t, d), F32),
        compiler_params=_cparams("arbitrary", "arbitrary"),
        name="merge",
    )(yb, proj, proj, oa, x, c_gate, p_b, w_out, final_g.reshape(1, d))


def _split(t, sizes):
    out, o = [], 0
    for s in sizes:
        out.append(t[..., o:o + s])
        o += s
    return out


def _permute_shift_cols(x):
    r, wd, k, v, ad = _split(x, (B_W, LORA, B_W, B_W, LORA))
    pad = jnp.zeros(x.shape[:-1] + (LORA_W - 2 * LORA,), x.dtype)
    return r, k, v, jnp.concatenate([wd, ad, pad], axis=-1)


def _nh_order(x, axis=-1):
    axis = axis % x.ndim
    shape = x.shape
    x = x.reshape(shape[:axis] + (B_HEADS, HEAD_SIZE) + shape[axis + 1:])
    return jnp.swapaxes(x, axis, axis + 1).reshape(shape)


def _hn_order(x, axis=-1):
    axis = axis % x.ndim
    shape = x.shape
    x = x.reshape(shape[:axis] + (HEAD_SIZE, B_HEADS) + shape[axis + 1:])
    return jnp.swapaxes(x, axis, axis + 1).reshape(shape)


def _shift_row(proj_row):
    r = _hn_order(proj_row[..., 0:B_W])
    k = _hn_order(proj_row[..., B_W:2 * B_W])
    v = _hn_order(proj_row[..., 2 * B_W:3 * B_W])
    lo = proj_row[..., COL_LORA * LORA_W:]
    return jnp.concatenate([r, lo[..., :LORA], k, v, lo[..., LORA:2 * LORA]], axis=-1)


def kernel(x_prompt, x_sample, c_prompt, c_sample, state_wkv, state_shift, norm_g, w_c, b_c, w_in, ln_v_g, ln_v_b, w_s, b_s, mu_shift, w0, w2, a0, a2, k_k, k_a, r_k, gn_g, gn_b, p_a, p_b, w_out, final_g):
    assert norm_g.shape[0] == 1, "single-layer trunk"
    bp, t, d = x_prompt.shape
    bs = x_sample.shape[0]

    w_rows = jnp.swapaxes(w_in[0], 0, 1)
    bounds = np.cumsum([0, A_W, A_W, A_W, B_W, LORA, B_W, B_W, LORA, B_W, D_MODEL, D_MODEL])
    ua, va, za, wr, wwd, wk, wv, wad, zb, ga, gb = (w_rows[a:b] for a, b in zip(bounds[:-1], bounds[1:]))
    wpad = jnp.zeros((LORA_W - 2 * LORA, d), F32)
    nh0 = lambda x: _nh_order(x, axis=0)
    w_p = jnp.concatenate([nh0(wr), nh0(wk), nh0(wv), nh0(zb), ga, gb, ua, va, za, wwd, wad, wpad],
                          axis=0).astype(BF16)

    def shift_cols(x):
        xr, xk, xv, xlo = _permute_shift_cols(x)
        return _nh_order(xr), _nh_order(xk), _nh_order(xv), xlo

    mu = [m.reshape(1, -1) for m in shift_cols(mu_shift[0])]
    zrow = jnp.zeros((LORA_W - LORA, B_W), F32)
    w2p = jnp.concatenate([_nh_order(w2[0]), zrow], axis=0).astype(BF16)
    a2p = jnp.concatenate([jnp.zeros((LORA, B_W), F32), _nh_order(a2[0]),
                           jnp.zeros((LORA_W - 2 * LORA, B_W), F32)], axis=0).astype(BF16)
    w0r, a0r = _nh_order(w0[0]).reshape(1, B_W), _nh_order(a0[0]).reshape(1, B_W)
    p_a_b, w_out_b = p_a[0].astype(BF16), w_out[0].astype(BF16)
    p_b_b = nh0(p_b[0]).astype(BF16)
    ln_g, ln_b = ln_v_g[0].reshape(1, A_W), ln_v_b[0].reshape(1, A_W)
    bias2d = jnp.repeat(b_s[0].T, CHUNK, axis=1)
    w00 = jnp.repeat(w_s[0][:, 0, 0], CHUNK).reshape(1, A_W)
    b00 = jnp.repeat(b_s[0][:, 0], CHUNK).reshape(1, A_W)

    m_rows = bp + bs
    m_pad = -m_rows % V7X_SUBLANES
    c_all = jnp.concatenate([c_prompt, c_sample, jnp.zeros((m_pad, d), F32)], axis=0)
    mod = _modulation(c_all, w_c[0], b_c[0])
    shift_p, scale_p, gate_p = (mod[:bp, i * d:(i + 1) * d].reshape(bp, 1, d) for i in range(3))
    shift_s, scale_s, gate_s = (mod[bp:m_rows, i * d:(i + 1) * d].reshape(1, bs, d) for i in range(3))

    def wkv_consts(bsz):
        return [_head_const(c, bsz) for c in (k_k[0], k_a[0], r_k[0], gn_g[0], gn_b[0])]

    xs = x_sample.reshape(1, bs, d)
    h_p = _norm_modulate(x_prompt, norm_g[0], scale_p, shift_p, tm=1024)
    h_s = _norm_modulate(xs, norm_g[0], scale_s, shift_s, tm=bs)
    proj_p, proj_s = _in_projection(h_p, h_s[0], w_p)
    proj_s = proj_s[None]

    zeros_shift = [jnp.zeros((bp, 1, w), F32) for w in (B_W, B_W, B_W, LORA_W)]
    feats = _rwkv_prep_seq(proj_p, zeros_shift, mu, w0r, a0r, w2p, a2p)
    s0_p = jnp.zeros((HEAD_SIZE, HEAD_SIZE, bp * B_HEADS), F32)
    yb_p, s_hl = _wkv(*feats, s0_p, wkv_consts(bp), tt=32)
    oa_p = _branch_a_seq(proj_p, ln_g, ln_b, w_s[0], bias2d, p_a_b)
    y_prompt = _merge(yb_p, proj_p, oa_p, x_prompt, gate_p, p_b_b, w_out_b, final_g, tm=256)
    wkv_prompt = jnp.transpose(s_hl.reshape(HEAD_SIZE, HEAD_SIZE, bp, B_HEADS), (2, 3, 1, 0))[None]
    shift_prompt = _shift_row(proj_p[:, t - 1, :])[None]

    feats = _rwkv_prep_tok(proj_s, shift_cols(state_shift[0]), mu, w0r, a0r, w2p, a2p)
    feats = [jnp.transpose(f.reshape(bs, HEAD_SIZE, B_HEADS), (2, 1, 0)) for f in feats]
    tok_consts = [jnp.broadcast_to(c.reshape(B_HEADS, HEAD_SIZE, 1), (B_HEADS, HEAD_SIZE, bs))
                  for c in (k_k[0], k_a[0], r_k[0], gn_g[0], gn_b[0])]
    yb_s, wkv_s = _wkv_tok(*feats, jnp.transpose(state_wkv[0], (1, 2, 3, 0)), tok_consts)
    yb_s = jnp.transpose(yb_s, (2, 1, 0)).reshape(1, bs, B_W)
    oa_s, v_s = _branch_a_tok(proj_s, ln_g, ln_b, w00, b00, p_a_b)
    y_s = _merge(yb_s, proj_s, oa_s, xs, gate_s, p_b_b, w_out_b, final_g, tm=bs)
    y_sample = y_s.reshape(bs, 1, d)
    wkv_sample = jnp.transpose(wkv_s, (3, 0, 1, 2))[None]
    shift_sample = _shift_row(proj_s[0])[None]
    chunk_v_sample = v_s.reshape(1, bs, 1, A_W)

    return (y_prompt, y_sample, wkv_prompt, shift_prompt, wkv_sample, shift_sample, chunk_v_sample)
```

```python
import functools
import math

import jax
import jax.numpy as jnp
import numpy as np
from jax import lax
from jax.experimental import pallas as pl
from jax.experimental.pallas import tpu as pltpu

F32 = jnp.float32
BF16 = jnp.bfloat16

D_MODEL = 2048
A_W = 1024
A_GROUPS = 8
CHUNK = 128
B_HEADS = 32
HEAD_SIZE = 64
B_W = B_HEADS * HEAD_SIZE
LORA = 96
NORM_EPS = 1e-6
GN_EPS = HEAD_SIZE * 1e-5
DECAY_SCALE = math.exp(-0.5)

LORA_W = 512
PROJ_W = 6 * B_W + 3 * A_W + LORA_W
COL_R, COL_K, COL_V, COL_ZB, COL_GA, COL_GB = 0, 1, 2, 3, 4, 5
COL_UA, COL_VA, COL_ZA = 12, 13, 14
COL_LORA = (6 * B_W + 3 * A_W) // LORA_W

V7X_LANES = 128
V7X_SUBLANES = 8
LANE_GROUP = B_HEADS
VMEM_LIMIT = 56 * 1024 * 1024


def _cparams(*sem):
    return pltpu.CompilerParams(dimension_semantics=sem, vmem_limit_bytes=VMEM_LIMIT)


def _mod_kernel(c_ref, w_ref, b_ref, o_ref):
    acc = jnp.dot(c_ref[...].astype(BF16), w_ref[...].astype(BF16), preferred_element_type=F32)
    o_ref[...] = acc + b_ref[...]


def _modulation(c_all, w_c, b_c):
    m, d = c_all.shape
    n = w_c.shape[1]
    tn = 768
    return pl.pallas_call(
        _mod_kernel,
        grid=(n // tn,),
        in_specs=[pl.BlockSpec((m, d), lambda j: (0, 0)),
                  pl.BlockSpec((d, tn), lambda j: (0, j)),
                  pl.BlockSpec((1, tn), lambda j: (0, j))],
        out_specs=pl.BlockSpec((m, tn), lambda j: (0, j)),
        out_shape=jax.ShapeDtypeStruct((m, n), F32),
        compiler_params=_cparams("arbitrary"),
        name="modulation",
    )(c_all, w_c, b_c.reshape(1, n))


def _normmod_kernel(x_ref, g_ref, sc_ref, sh_ref, o_ref):
    x = x_ref[0]
    ms = jnp.mean(x * x, axis=-1, keepdims=True)
    y = x * lax.rsqrt(ms + NORM_EPS) * g_ref[...]
    o_ref[0] = (y * (1.0 + sc_ref[0]) + sh_ref[0]).astype(BF16)


def _norm_modulate(x, g, scale, shift, tm):
    bsz, t, d = x.shape
    tm = min(tm, t)
    ts = scale.shape[1]
    sspec = (pl.BlockSpec((1, 1, d), lambda b, i: (b, 0, 0)) if ts == 1
             else pl.BlockSpec((1, tm, d), lambda b, i: (b, i, 0)))
    return pl.pallas_call(
        _normmod_kernel,
        grid=(bsz, t // tm),
        in_specs=[pl.BlockSpec((1, tm, d), lambda b, i: (b, i, 0)),
                  pl.BlockSpec((1, d), lambda b, i: (0, 0)),
                  sspec, sspec],
        out_specs=pl.BlockSpec((1, tm, d), lambda b, i: (b, i, 0)),
        out_shape=jax.ShapeDtypeStruct((bsz, t, d), BF16),
        compiler_params=_cparams("arbitrary", "arbitrary"),
        name="norm_modulate",
    )(x, g.reshape(1, d), scale, shift)


def _inproj_kernel(h_ref, hs_ref, w_ref, o_ref, os_ref):
    nt = (((1,), (1,)), ((), ()))
    o_ref[0] = lax.dot_general(h_ref[0], w_ref[...], nt, preferred_element_type=F32)

    @pl.when(pl.program_id(0) == 0)
    def _():
        os_ref[...] = lax.dot_general(hs_ref[...], w_ref[...], nt, preferred_element_type=F32)


def _in_projection(h, h_tok, w_t, tn=512):
    bsz, t, d = h.shape
    m = h_tok.shape[0]
    n = w_t.shape[0]
    nj = n // tn
    tok_out = pl.BlockSpec((m, tn), lambda b, j: (0, jnp.where(b == 0, j, nj - 1)))
    return pl.pallas_call(
        _inproj_kernel,
        grid=(bsz, nj),
        in_specs=[pl.BlockSpec((1, t, d), lambda b, j: (b, 0, 0)),
                  pl.BlockSpec((m, d), lambda b, j: (0, 0)),
                  pl.BlockSpec((tn, d), lambda b, j: (j, 0))],
        out_specs=[pl.BlockSpec((1, t, tn), lambda b, j: (b, 0, j)), tok_out],
        out_shape=[jax.ShapeDtypeStruct((bsz, t, n), F32), jax.ShapeDtypeStruct((m, n), F32)],
        compiler_params=_cparams("arbitrary", "arbitrary"),
        name="in_projection",
    )(h, h_tok, w_t)


def _lerp(p, q, mu):
    return p + mu * (q - p)


def _lora_logits(lo, w0, a0, w2p, a2p):
    col = lax.broadcasted_iota(jnp.int32, lo.shape, 1)
    lt = jnp.where(col < LORA, jnp.tanh(lo), lo).astype(BF16)
    wl = w0 + jnp.dot(lt, w2p, preferred_element_type=F32)
    al = a0 + jnp.dot(lt, a2p, preferred_element_type=F32)
    return wl, al


def _decay(wl):
    return jnp.exp(-DECAY_SCALE * jax.nn.sigmoid(wl))


def _prep_core(p, q, mu, w0, a0, w2p, a2p):
    r, k, v, lo = (_lerp(pi, qi, mi) for pi, qi, mi in zip(p, q, mu))
    wl, al = _lora_logits(lo, w0, a0, w2p, a2p)
    return r, _decay(wl), k, v, jax.nn.sigmoid(al)


def _move_lane_group(x, src_g, dst_g):
    shift = ((dst_g - src_g) % (V7X_LANES // LANE_GROUP)) * LANE_GROUP
    return x if shift == 0 else pltpu.roll(x, shift, axis=1)


def _swap_lane_groups(x):
    lane = lax.broadcasted_iota(jnp.int32, x[0].shape, 1)
    low_half = lane < 2 * LANE_GROUP
    even_group = (lane // LANE_GROUP) % 2 == 0
    rot = lambda v, s: pltpu.roll(v, s, axis=1)
    y0 = jnp.where(low_half, x[0], rot(x[2], 2 * LANE_GROUP))
    y2 = jnp.where(low_half, rot(x[0], 2 * LANE_GROUP), x[2])
    y1 = jnp.where(low_half, x[1], rot(x[3], 2 * LANE_GROUP))
    y3 = jnp.where(low_half, rot(x[1], 2 * LANE_GROUP), x[3])
    return [jnp.where(even_group, y0, rot(y1, LANE_GROUP)),
            jnp.where(even_group, rot(y0, 3 * LANE_GROUP), y1),
            jnp.where(even_group, y2, rot(y3, LANE_GROUP)),
            jnp.where(even_group, rot(y2, 3 * LANE_GROUP), y3)]


def _prep_seq_kernel(pr, pk, pv, plo, tr, tk, tv, tlo, sr, sk, sv, slo,
                     mr, mk, mv, mlo, w0, a0, w2p, a2p,
                     o_r, o_w, o_k, o_v, o_a):
    i = pl.program_id(1)

    def prev(p_ref, tail_ref, s_ref):
        p = p_ref[0]
        last = tail_ref[0][V7X_SUBLANES - 1:V7X_SUBLANES, :]
        first = jnp.where(i == 0, s_ref[0], last)
        rolled = pltpu.roll(p, 1, axis=0)
        row = lax.broadcasted_iota(jnp.int32, p.shape, 0)
        return p, jnp.where(row == 0, first, rolled)

    pq = [prev(x, y, z) for x, y, z in ((pr, tr, sr), (pk, tk, sk), (pv, tv, sv), (plo, tlo, slo))]
    outs = _prep_core([x[0] for x in pq], [x[1] for x in pq],
                      (mr[...], mk[...], mv[...], mlo[...]), w0[...], a0[...], w2p[...], a2p[...])
    for o, val in zip((o_r, o_w, o_k, o_v, o_a), outs):
        for q in range(B_W // V7X_LANES):
            o[0, q] = val[:, q * V7X_LANES:(q + 1) * V7X_LANES]


def _prep_tok_kernel(pr, pk, pv, plo, sr, sk, sv, slo,
                     mr, mk, mv, mlo, w0, a0, w2p, a2p,
                     o_r, o_w, o_k, o_v, o_a):
    outs = _prep_core((pr[0], pk[0], pv[0], plo[0]), (sr[...], sk[...], sv[...], slo[...]),
                      (mr[...], mk[...], mv[...], mlo[...]), w0[...], a0[...], w2p[...], a2p[...])
    for o, val in zip((o_r, o_w, o_k, o_v, o_a), outs):
        o[0] = val


def _rwkv_prep_seq(proj, shift0, mu, w0, a0, w2p, a2p, tm=256):
    bsz, t, _ = proj.shape
    tail = tm // V7X_SUBLANES
    nq = B_W // V7X_LANES
    cols = ((B_W, COL_R), (B_W, COL_K), (B_W, COL_V), (LORA_W, COL_LORA))
    cur = [pl.BlockSpec((1, tm, w), functools.partial(lambda b, i, c: (b, i, c), c=c)) for w, c in cols]
    tails = [pl.BlockSpec((1, V7X_SUBLANES, w),
                          functools.partial(lambda b, i, c: (b, jnp.maximum(i * tail - 1, 0), c), c=c))
             for w, c in cols]
    s0 = [pl.BlockSpec((1, 1, w), lambda b, i: (b, 0, 0)) for w, _ in cols]
    rowv = [pl.BlockSpec((1, w), lambda b, i: (0, 0)) for w, _ in cols]
    full = lambda shape: pl.BlockSpec(shape, lambda b, i: (0,) * len(shape))
    out_spec = pl.BlockSpec((1, nq, tm, V7X_LANES), lambda b, i: (b, 0, i, 0))
    return pl.pallas_call(
        _prep_seq_kernel,
        grid=(bsz, t // tm),
        in_specs=cur + tails + s0 + rowv + [full((1, B_W)), full((1, B_W)),
                                            full((LORA_W, B_W)), full((LORA_W, B_W))],
        out_specs=[out_spec] * 5,
        out_shape=[jax.ShapeDtypeStruct((bsz, nq, t, V7X_LANES), F32)] * 5,
        compiler_params=_cparams("arbitrary", "arbitrary"),
        name="rwkv_prep_seq",
    )(proj, proj, proj, proj, proj, proj, proj, proj, *shift0, *mu, w0, a0, w2p, a2p)


def _rwkv_prep_tok(proj, shift0, mu, w0, a0, w2p, a2p):
    _, m, _ = proj.shape
    cols = ((B_W, COL_R), (B_W, COL_K), (B_W, COL_V), (LORA_W, COL_LORA))
    cur = [pl.BlockSpec((1, m, w), functools.partial(lambda i, c: (0, 0, c), c=c)) for w, c in cols]
    s0 = [pl.BlockSpec((m, w), lambda i: (0, 0)) for w, _ in cols]
    rowv = [pl.BlockSpec((1, w), lambda i: (0, 0)) for w, _ in cols]
    full = lambda shape: pl.BlockSpec(shape, lambda i: (0,) * len(shape))
    out_spec = pl.BlockSpec((1, m, B_W), lambda i: (0, 0, 0))
    return pl.pallas_call(
        _prep_tok_kernel,
        grid=(1,),
        in_specs=cur + s0 + rowv + [full((1, B_W)), full((1, B_W)),
                                    full((LORA_W, B_W)), full((LORA_W, B_W))],
        out_specs=[out_spec] * 5,
        out_shape=[jax.ShapeDtypeStruct((1, m, B_W), F32)] * 5,
        compiler_params=_cparams("arbitrary"),
        name="rwkv_prep_tok",
    )(proj, proj, proj, proj, *shift0, *mu, w0, a0, w2p, a2p)


WKV_ROWS = 16


AHEAD = 2


def _wkv_kernel(xr, xw, xk, xv, xa, hr, hw, hk, hv, ha, s0_ref, kk_c, ka_c, rk_c, gg_c, gb_c,
                y_ref, s_ref, r_s, w_s, k_s, a_s, vn_s, vt_s, nkk_s, b_s, kp_s,
                p1_s, p2_s, p3_s, p4_s, p5_s, rks_s, cba_s, cka_s, cbr_s, ckr_s, ys_s,
                *, tt, nb):
    step = pl.program_id(0)
    ng = tt // V7X_SUBLANES
    pitch = tt + V7X_SUBLANES
    vpitch = HEAD_SIZE + V7X_SUBLANES

    @pl.when(step == 0)
    def _():
        s_ref[...] = s0_ref[...]

    dsts = (r_s, w_s, k_s, a_s, vn_s)

    def group_rows(rg):
        start = rg * V7X_SUBLANES
        if not isinstance(rg, int):
            start = pl.multiple_of(start, V7X_SUBLANES)
        return pl.ds(start, V7X_SUBLANES)

    def exchange(srcs, src_rg, dst_rg, q):
        src_rows = group_rows(src_rg)
        dst_rows = group_rows(dst_rg)
        for x, dst in zip(srcs, dsts):
            for b in range(nb):
                tile = x[b, q, src_rows, :]
                lanes = slice(b * LANE_GROUP, (b + 1) * LANE_GROUP)
                for g in range(nb):
                    dst[nb * q + g, dst_rows, lanes] = _move_lane_group(tile, g, b)[:, lanes]

    def v_by_token(rg, q):
        rows = group_rows(rg)
        for g in range(nb):
            n = nb * q + g
            vt_s[pl.ds(rg * V7X_SUBLANES * vpitch + n, V7X_SUBLANES, stride=vpitch), :] = vn_s[n, rows, :]

    def prep(m):
        ts = group_rows(m)
        sq = None
        for n in range(HEAD_SIZE):
            kk = k_s[n, ts, :] * kk_c[n]
            sq = kk * kk if sq is None else sq + kk * kk
        den = jnp.maximum(jnp.sqrt(sq), 1e-12)
        nxt_tok = lambda x: pltpu.roll(x, V7X_SUBLANES - 1, axis=0)
        acc = [None] * 5
        for n in range(HEAD_SIZE):
            k = k_s[n, ts, :]
            a = a_s[n, ts, :]
            r = r_s[n, ts, :]
            w = w_s[n, ts, :]
            kk = (k * kk_c[n]) / den
            nkk = -kk
            b = kk * a
            kp = k * (1.0 + (a - 1.0) * ka_c[n])
            nkk_n = nxt_tok(nkk)
            w_n = nxt_tok(w)
            nkk_s[n, ts, :] = nkk
            b_s[n, ts, :] = b
            kp_s[n, ts, :] = kp
            p1_s[n, ts, :] = w * r
            p2_s[n, ts, :] = w * w_n
            p3_s[n, ts, :] = b * w_n
            p4_s[n, ts, :] = kp * w_n
            p5_s[n, ts, :] = w * nkk_n
            kpr = kp * r
            terms = (b * nkk_n, kp * nkk_n, b * r, kpr, kpr * rk_c[n])
            acc = [x if y is None else y + x for x, y in zip(terms, acc)]
        for ref, val in zip((cba_s, cka_s, cbr_s, ckr_s, rks_s), acc):
            ref[ts, :] = val

    cur = (xr, xw, xk, xa, xv)
    nxt = (hr, hw, hk, ha, hv)

    @pl.when(step == 0)
    def _():
        for rg in range(AHEAD):
            for q in range(B_W // V7X_LANES):
                exchange(cur, rg, rg, q)
                v_by_token(rg, q)

    @pl.when(step > 0)
    def _():
        head = AHEAD * V7X_SUBLANES
        for dst in dsts:
            dst[:, 0:head, :] = dst[:, tt:tt + head, :]
        vt_s[0:head * vpitch, :] = vt_s[tt * vpitch:(tt + head) * vpitch, :]

    prep(0)

    parts = [h * WKV_ROWS for h in range(HEAD_SIZE // WKV_ROWS)]
    zeros = lambda: jnp.zeros((WKV_ROWS, V7X_LANES), F32)

    def bcast(ref, t, j):
        return jnp.broadcast_to(ref[j, pl.ds(t, 1), :], (WKV_ROWS, V7X_LANES))

    def lane_row(ref, t):
        return jnp.broadcast_to(ref[pl.ds(t, 1), :], (WKV_ROWS, V7X_LANES))

    def first_carry(i0):
        ua, xb = zeros(), zeros()
        for j in range(HEAD_SIZE):
            s = s_ref[j, i0:i0 + WKV_ROWS, :]
            ua = ua + s * bcast(nkk_s, 0, j)
            xb = xb + s * bcast(p5_s, 0, j)
        return ua, xb

    def group(m, carry, srcs, src_rg):
        if m + 1 < ng:
            prep(m + 1)
        pairs = V7X_SUBLANES // 2

        def pair(p, carry):
            ta = m * V7X_SUBLANES + 2 * p
            tb = ta + 1
            tc = jnp.minimum(ta + 2, tt - 2)
            out = []
            for i0, (ua, xb) in zip(parts, carry):
                va = vt_s[pl.ds(pl.multiple_of(ta * vpitch + i0, V7X_SUBLANES), WKV_ROWS), :]
                vb = vt_s[pl.ds(pl.multiple_of(tb * vpitch + i0, V7X_SUBLANES), WKV_ROWS), :]
                ub = (xb + ua * lane_row(cba_s, ta)) + va * lane_row(cka_s, ta)
                ya, yb, uc, xd = zeros(), zeros(), zeros(), zeros()
                for j in range(HEAD_SIZE):
                    s = s_ref[j, i0:i0 + WKV_ROWS, :]
                    ya = ya + s * bcast(p1_s, ta, j)
                    sn = ((((s * bcast(p2_s, ta, j) + ua * bcast(p3_s, ta, j)) + va * bcast(p4_s, ta, j))
                           + ub * bcast(b_s, tb, j)) + vb * bcast(kp_s, tb, j))
                    s_ref[j, i0:i0 + WKV_ROWS, :] = sn
                    yb = yb + sn * bcast(r_s, tb, j)
                    uc = uc + sn * bcast(nkk_s, tc, j)
                    xd = xd + sn * bcast(p5_s, tc, j)
                ya = (ya + ua * lane_row(cbr_s, ta)) + va * lane_row(ckr_s, ta)
                ys_s[pl.ds(i0 * pitch + ta, WKV_ROWS, stride=pitch), :] = ya
                ys_s[pl.ds(i0 * pitch + tb, WKV_ROWS, stride=pitch), :] = yb
                out.append((uc, xd))
            prev = m * pairs + p + pairs - 1
            prev_rg = lax.shift_right_logical(prev, 2) - 1 + AHEAD
            prev_q = 4 * (prev & (pairs - 1))
            for d in range(4):
                v_by_token(prev_rg, prev_q + d)
                exchange(srcs, src_rg, m + AHEAD, 4 * p + d)
            return tuple(out)

        return lax.fori_loop(0, pairs, pair, carry)

    carry = tuple(first_carry(i0) for i0 in parts)
    for m in range(ng):
        carry = (group(m, carry, cur, m + AHEAD) if m + AHEAD < ng
                 else group(m, carry, nxt, m + AHEAD - ng))
    for d in range(4):
        v_by_token(ng - 1 + AHEAD, B_W // V7X_LANES - 4 + d)

    def post(c, carry):
        t0 = pl.multiple_of(c * V7X_SUBLANES, V7X_SUBLANES)
        ys = [ys_s[pl.ds(i * pitch + t0, V7X_SUBLANES), :] for i in range(HEAD_SIZE)]
        mu = sum(ys[1:], ys[0]) * (1.0 / HEAD_SIZE)
        yc = [y - mu for y in ys]
        sq = [y * y for y in yc]
        rstd = lax.rsqrt(sum(sq[1:], sq[0]) * (1.0 / HEAD_SIZE) + GN_EPS)
        scale = rks_s[pl.ds(t0, V7X_SUBLANES), :]
        out = [yc[i] * rstd * gg_c[i:i + 1, :] + gb_c[i:i + 1, :]
               + scale * vn_s[i, pl.ds(t0, V7X_SUBLANES), :] for i in range(HEAD_SIZE)]
        for q in range(HEAD_SIZE // nb):
            nat = _swap_lane_groups(out[nb * q:nb * (q + 1)])
            for b in range(nb):
                y_ref[b, pl.ds(t0, V7X_SUBLANES), q * V7X_LANES:(q + 1) * V7X_LANES] = nat[b]
        return carry

    lax.fori_loop(0, ng, post, 0, unroll=2)


def _wkv(r, w, k, v, a, s0, consts, tt):
    nb, nq, t, lanes = r.shape
    n = HEAD_SIZE
    assert lanes == V7X_LANES and nb * B_HEADS == lanes and nq * lanes == B_W
    head = AHEAD * V7X_SUBLANES
    assert tt % head == 0 and tt // V7X_SUBLANES > AHEAD
    seq = pl.BlockSpec((nb, nq, tt, lanes), lambda i: (0, 0, i, 0))
    nxt = pl.BlockSpec((nb, nq, head, lanes),
                       lambda i: (0, 0, jnp.minimum((i + 1) * (tt // head), t // head - 1), 0))
    st = pl.BlockSpec((n, n, lanes), lambda i: (0, 0, 0))
    c3 = pl.BlockSpec((n, 1, lanes), lambda i: (0, 0, 0))
    c2 = pl.BlockSpec((n, lanes), lambda i: (0, 0))
    kk_c, ka_c, rk_c, gg_c, gb_c = consts
    col = lambda c: c.reshape(n, 1, lanes)
    pitch = tt + V7X_SUBLANES
    by_n = pltpu.VMEM((n, tt, lanes), F32)
    lead_n = pltpu.VMEM((n, tt + head, lanes), F32)
    return pl.pallas_call(
        functools.partial(_wkv_kernel, tt=tt, nb=nb),
        grid=(t // tt,),
        in_specs=[seq] * 5 + [nxt] * 5
        + [pl.BlockSpec((n, n, lanes), lambda i: (0, 0, 0), pipeline_mode=pl.Buffered(1)),
           c3, c3, c3, c2, c2],
        out_specs=[pl.BlockSpec((nb, tt, B_W), lambda i: (0, i, 0)), st],
        out_shape=[jax.ShapeDtypeStruct((nb, t, B_W), F32),
                   jax.ShapeDtypeStruct((n, n, lanes), F32)],
        scratch_shapes=[lead_n] * 5
        + [pltpu.VMEM(((tt + head) * (n + V7X_SUBLANES), lanes), F32)] + [by_n] * 8
        + [pltpu.VMEM((tt, lanes), F32)] * 5 + [pltpu.VMEM((n * pitch, lanes), F32)],
        compiler_params=_cparams("arbitrary"),
        name="wkv",
    )(r, w, k, v, a, r, w, k, v, a, s0, col(kk_c), col(ka_c), col(rk_c), gg_c, gb_c)


def _wkv_tok_kernel(r_ref, w_ref, k_ref, v_ref, a_ref, s0_ref, kk_c, ka_c, rk_c, gg_c, gb_c,
                    y_ref, s_ref, ys_s):
    k = k_ref[0]
    a = a_ref[0]
    r = r_ref[0]
    w = w_ref[0]
    v = v_ref[0]
    kk = k * kk_c[0]
    nrm = jnp.sqrt(jnp.sum(kk * kk, axis=0, keepdims=True))
    kk = kk / jnp.maximum(nrm, 1e-12)
    nkk = -kk
    bvec = kk * a
    kp = k * (1.0 + (a - 1.0) * ka_c[0])
    wr = w * r
    br = jnp.sum(bvec * r, axis=0, keepdims=True)
    kr = jnp.sum(kp * r, axis=0, keepdims=True)
    for i in range(HEAD_SIZE):
        s = s0_ref[0, i]
        u = jnp.sum(s * nkk, axis=0, keepdims=True)
        yw = jnp.sum(s * wr, axis=0, keepdims=True)
        vi = v[i:i + 1, :]
        s_ref[0, i] = (s * w + u * bvec) + vi * kp
        ys_s[i:i + 1, :] = (yw + u * br) + vi * kr
    y = ys_s[...]
    mu = jnp.mean(y, axis=0, keepdims=True)
    yc = y - mu
    var = jnp.mean(yc * yc, axis=0, keepdims=True)
    yn = yc * lax.rsqrt(var + GN_EPS) * gg_c[0] + gb_c[0]
    bonus = jnp.sum(r * kp * rk_c[0], axis=0, keepdims=True) * v
    y_ref[0] = yn + bonus


def _wkv_tok(r, w, k, v, a, s0, consts):
    nh, n, bsz = r.shape
    vec = pl.BlockSpec((1, n, bsz), lambda h: (h, 0, 0))
    st = pl.BlockSpec((1, n, n, bsz), lambda h: (h, 0, 0, 0))
    return pl.pallas_call(
        _wkv_tok_kernel,
        grid=(nh,),
        in_specs=[vec] * 5 + [st] + [vec] * 5,
        out_specs=[vec, st],
        out_shape=[jax.ShapeDtypeStruct((nh, n, bsz), F32),
                   jax.ShapeDtypeStruct(s0.shape, F32)],
        scratch_shapes=[pltpu.VMEM((n, bsz), F32)],
        compiler_params=_cparams("arbitrary"),
        name="wkv_tok",
    )(r, w, k, v, a, s0, *consts)


def _head_const(x, bsz):
    return jnp.tile(x.reshape(B_HEADS, HEAD_SIZE).T, (1, bsz))


def _gelu(x):
    return jax.nn.gelu(x)


def _layernorm(x, g, b):
    mu = jnp.mean(x, axis=-1, keepdims=True)
    xc = x - mu
    var = jnp.mean(xc * xc, axis=-1, keepdims=True)
    return xc * lax.rsqrt(var + NORM_EPS) * g + b


def _branch_a_seq_kernel(u_ref, v_ref, z_ref, g_ref, lg, lb, ws_ref, bias_ref, pa_ref, o_ref, *, tm):
    u = _gelu(u_ref[0])
    v = _layernorm(_gelu(v_ref[0]), lg[...], lb[...])
    vb = v.astype(BF16)
    row = lax.broadcasted_iota(jnp.int32, (CHUNK, CHUNK), 0)
    col = lax.broadcasted_iota(jnp.int32, (CHUNK, CHUNK), 1)
    rows = []
    for c in range(tm // CHUNK):
        blocks = []
        for g in range(A_GROUPS):
            wm = jnp.where(row >= col, ws_ref[g], 0.0).astype(BF16)
            blk = vb[c * CHUNK:(c + 1) * CHUNK, g * CHUNK:(g + 1) * CHUNK]
            blocks.append(jnp.dot(wm, blk, preferred_element_type=F32))
        rows.append(jnp.concatenate(blocks, axis=1) + bias_ref[...])
    mix = jnp.concatenate(rows, axis=0)
    ya = u * mix * jax.nn.silu(z_ref[0])
    o_ref[0] = jax.nn.sigmoid(g_ref[0]) * jnp.dot(ya.astype(BF16), pa_ref[...],
                                                  preferred_element_type=F32)


def _branch_a_tok_kernel(u_ref, v_ref, z_ref, g_ref, lg, lb, w00, b00, pa_ref, o_ref, vo_ref):
    u = _gelu(u_ref[0])
    v = _layernorm(_gelu(v_ref[0]), lg[...], lb[...])
    vo_ref[...] = v
    mix = v * w00[...] + b00[...]
    ya = u * mix * jax.nn.silu(z_ref[0])
    o_ref[0] = jax.nn.sigmoid(g_ref[0]) * jnp.dot(ya.astype(BF16), pa_ref[...],
                                                  preferred_element_type=F32)


def _branch_a_seq(proj, ln_g, ln_b, w_s, bias2d, p_a, tm=512):
    bsz, t, _ = proj.shape
    tm = min(tm, t)
    full = lambda shape: pl.BlockSpec(shape, lambda b, i: (0,) * len(shape))
    sec = lambda w, c: pl.BlockSpec((1, tm, w), lambda b, i: (b, i, c))
    return pl.pallas_call(
        functools.partial(_branch_a_seq_kernel, tm=tm),
        grid=(bsz, t // tm),
        in_specs=[sec(A_W, COL_UA), sec(A_W, COL_VA), sec(A_W, COL_ZA), sec(D_MODEL, COL_GA),
                  full((1, A_W)), full((1, A_W)), full((A_GROUPS, CHUNK, CHUNK)),
                  full((CHUNK, A_W)), full((A_W, D_MODEL))],
        out_specs=pl.BlockSpec((1, tm, D_MODEL), lambda b, i: (b, i, 0)),
        out_shape=jax.ShapeDtypeStruct((bsz, t, D_MODEL), F32),
        compiler_params=_cparams("arbitrary", "arbitrary"),
        name="branch_a_seq",
    )(proj, proj, proj, proj, ln_g, ln_b, w_s, bias2d, p_a)


def _branch_a_tok(proj, ln_g, ln_b, w00, b00, p_a):
    _, m, _ = proj.shape
    full = lambda shape: pl.BlockSpec(shape, lambda i: (0,) * len(shape))
    sec = lambda w, c: pl.BlockSpec((1, m, w), lambda i: (0, 0, c))
    return pl.pallas_call(
        _branch_a_tok_kernel,
        grid=(1,),
        in_specs=[sec(A_W, COL_UA), sec(A_W, COL_VA), sec(A_W, COL_ZA), sec(D_MODEL, COL_GA),
                  full((1, A_W)), full((1, A_W)), full((1, A_W)), full((1, A_W)),
                  full((A_W, D_MODEL))],
        out_specs=[pl.BlockSpec((1, m, D_MODEL), lambda i: (0, 0, 0)),
                   pl.BlockSpec((m, A_W), lambda i: (0, 0))],
        out_shape=[jax.ShapeDtypeStruct((1, m, D_MODEL), F32),
                   jax.ShapeDtypeStruct((m, A_W), F32)],
        compiler_params=_cparams("arbitrary"),
        name="branch_a_tok",
    )(proj, proj, proj, proj, ln_g, ln_b, w00, b00, p_a)


def _merge_kernel(yb_ref, zb_ref, gb_ref, oa_ref, x_ref, cg_ref, pb_ref, wo_ref, fg_ref, o_ref):
    yb = (yb_ref[0] * jax.nn.silu(zb_ref[0])).astype(BF16)
    merged = oa_ref[0] + jax.nn.sigmoid(gb_ref[0]) * jnp.dot(yb, pb_ref[...],
                                                            preferred_element_type=F32)
    out = x_ref[0] + cg_ref[0] * jnp.dot(merged.astype(BF16), wo_ref[...],
                                         preferred_element_type=F32)
    ms = jnp.mean(out * out, axis=-1, keepdims=True)
    o_ref[0] = out * lax.rsqrt(ms + NORM_EPS) * fg_ref[...]


def _merge(yb, proj, oa, x, c_gate, p_b, w_out, final_g, tm):
    bsz, t, d = x.shape
    tm = min(tm, t)
    tg = c_gate.shape[1]
    row = lambda: pl.BlockSpec((1, tm, d), lambda b, i: (b, i, 0))
    sec = lambda c: pl.BlockSpec((1, tm, d), lambda b, i: (b, i, c))
    gate = (pl.BlockSpec((1, 1, d), lambda b, i: (b, 0, 0)) if tg == 1
            else pl.BlockSpec((1, tm, d), lambda b, i: (b, i, 0)))
    wspec = lambda: pl.BlockSpec((d, d), lambda b, i: (0, 0), pipeline_mode=pl.Buffered(1))
    return pl.pallas_call(
        _merge_kernel,
        grid=(bsz, t // tm),
        in_specs=[row(), sec(COL_ZB), sec(COL_GB), row(), row(), gate, wspec(), wspec(),
                  pl.BlockSpec((1, d), lambda b, i: (0, 0))],
        out_specs=row(),
        out_shape=jax.ShapeDtypeStruct((bsz, t, d), F32),
        compiler_params=_cparams("arbitrary", "arbitrary"),
        name="merge",
    )(yb, proj, proj, oa, x, c_gate, p_b, w_out, final_g.reshape(1, d))


def _split(t, sizes):
    out, o = [], 0
    for s in sizes:
        out.append(t[..., o:o + s])
        o += s
    return out


def _permute_shift_cols(x):
    r, wd, k, v, ad = _split(x, (B_W, LORA, B_W, B_W, LORA))
    pad = jnp.zeros(x.shape[:-1] + (LORA_W - 2 * LORA,), x.dtype)
    return r, k, v, jnp.concatenate([wd, ad, pad], axis=-1)


def _nh_order(x, axis=-1):
    axis = axis % x.ndim
    shape = x.shape
    x = x.reshape(shape[:axis] + (B_HEADS, HEAD_SIZE) + shape[axis + 1:])
    return jnp.swapaxes(x, axis, axis + 1).reshape(shape)


def _hn_order(x, axis=-1):
    axis = axis % x.ndim
    shape = x.shape
    x = x.reshape(shape[:axis] + (HEAD_SIZE, B_HEADS) + shape[axis + 1:])
    return jnp.swapaxes(x, axis, axis + 1).reshape(shape)


def _shift_row(proj_row):
    r = _hn_order(proj_row[..., 0:B_W])
    k = _hn_order(proj_row[..., B_W:2 * B_W])
    v = _hn_order(proj_row[..., 2 * B_W:3 * B_W])
    lo = proj_row[..., COL_LORA * LORA_W:]
    return jnp.concatenate([r, lo[..., :LORA], k, v, lo[..., LORA:2 * LORA]], axis=-1)


def kernel(x_prompt, x_sample, c_prompt, c_sample, state_wkv, state_shift, norm_g, w_c, b_c, w_in, ln_v_g, ln_v_b, w_s, b_s, mu_shift, w0, w2, a0, a2, k_k, k_a, r_k, gn_g, gn_b, p_a, p_b, w_out, final_g):
    assert norm_g.shape[0] == 1, "single-layer trunk"
    bp, t, d = x_prompt.shape
    bs = x_sample.shape[0]

    w_rows = jnp.swapaxes(w_in[0], 0, 1)
    bounds = np.cumsum([0, A_W, A_W, A_W, B_W, LORA, B_W, B_W, LORA, B_W, D_MODEL, D_MODEL])
    ua, va, za, wr, wwd, wk, wv, wad, zb, ga, gb = (w_rows[a:b] for a, b in zip(bounds[:-1], bounds[1:]))
    wpad = jnp.zeros((LORA_W - 2 * LORA, d), F32)
    nh0 = lambda x: _nh_order(x, axis=0)
    w_p = jnp.concatenate([nh0(wr), nh0(wk), nh0(wv), nh0(zb), ga, gb, ua, va, za, wwd, wad, wpad],
                          axis=0).astype(BF16)

    def shift_cols(x):
        xr, xk, xv, xlo = _permute_shift_cols(x)
        return _nh_order(xr), _nh_order(xk), _nh_order(xv), xlo

    mu = [m.reshape(1, -1) for m in shift_cols(mu_shift[0])]
    zrow = jnp.zeros((LORA_W - LORA, B_W), F32)
    w2p = jnp.concatenate([_nh_order(w2[0]), zrow], axis=0).astype(BF16)
    a2p = jnp.concatenate([jnp.zeros((LORA, B_W), F32), _nh_order(a2[0]),
                           jnp.zeros((LORA_W - 2 * LORA, B_W), F32)], axis=0).astype(BF16)
    w0r, a0r = _nh_order(w0[0]).reshape(1, B_W), _nh_order(a0[0]).reshape(1, B_W)
    p_a_b, w_out_b = p_a[0].astype(BF16), w_out[0].astype(BF16)
    p_b_b = nh0(p_b[0]).astype(BF16)
    ln_g, ln_b = ln_v_g[0].reshape(1, A_W), ln_v_b[0].reshape(1, A_W)
    bias2d = jnp.repeat(b_s[0].T, CHUNK, axis=1)
    w00 = jnp.repeat(w_s[0][:, 0, 0], CHUNK).reshape(1, A_W)
    b00 = jnp.repeat(b_s[0][:, 0], CHUNK).reshape(1, A_W)

    m_rows = bp + bs
    m_pad = -m_rows % V7X_SUBLANES
    c_all = jnp.concatenate([c_prompt, c_sample, jnp.zeros((m_pad, d), F32)], axis=0)
    mod = _modulation(c_all, w_c[0], b_c[0])
    shift_p, scale_p, gate_p = (mod[:bp, i * d:(i + 1) * d].reshape(bp, 1, d) for i in range(3))
    shift_s, scale_s, gate_s = (mod[bp:m_rows, i * d:(i + 1) * d].reshape(1, bs, d) for i in range(3))

    def wkv_consts(bsz):
        return [_head_const(c, bsz) for c in (k_k[0], k_a[0], r_k[0], gn_g[0], gn_b[0])]

    xs = x_sample.reshape(1, bs, d)
    h_p = _norm_modulate(x_prompt, norm_g[0], scale_p, shift_p, tm=1024)
    h_s = _norm_modulate(xs, norm_g[0], scale_s, shift_s, tm=bs)
    proj_p, proj_s = _in_projection(h_p, h_s[0], w_p)
    proj_s = proj_s[None]

    zeros_shift = [jnp.zeros((bp, 1, w), F32) for w in (B_W, B_W, B_W, LORA_W)]
    feats = _rwkv_prep_seq(proj_p, zeros_shift, mu, w0r, a0r, w2p, a2p)
    s0_p = jnp.zeros((HEAD_SIZE, HEAD_SIZE, bp * B_HEADS), F32)
    yb_p, s_hl = _wkv(*feats, s0_p, wkv_consts(bp), tt=32)
    oa_p = _branch_a_seq(proj_p, ln_g, ln_b, w_s[0], bias2d, p_a_b)
    y_prompt = _merge(yb_p, proj_p, oa_p, x_prompt, gate_p, p_b_b, w_out_b, final_g, tm=256)
    wkv_prompt = jnp.transpose(s_hl.reshape(HEAD_SIZE, HEAD_SIZE, bp, B_HEADS), (2, 3, 1, 0))[None]
    shift_prompt = _shift_row(proj_p[:, t - 1, :])[None]

    feats = _rwkv_prep_tok(proj_s, shift_cols(state_shift[0]), mu, w0r, a0r, w2p, a2p)
    feats = [jnp.transpose(f.reshape(bs, HEAD_SIZE, B_HEADS), (2, 1, 0)) for f in feats]
    tok_consts = [jnp.broadcast_to(c.reshape(B_HEADS, HEAD_SIZE, 1), (B_HEADS, HEAD_SIZE, bs))
                  for c in (k_k[0], k_a[0], r_k[0], gn_g[0], gn_b[0])]
    yb_s, wkv_s = _wkv_tok(*feats, jnp.transpose(state_wkv[0], (1, 2, 3, 0)), tok_consts)
    yb_s = jnp.transpose(yb_s, (2, 1, 0)).reshape(1, bs, B_W)
    oa_s, v_s = _branch_a_tok(proj_s, ln_g, ln_b, w00, b00, p_a_b)
    y_s = _merge(yb_s, proj_s, oa_s, xs, gate_s, p_b_b, w_out_b, final_g, tm=bs)
    y_sample = y_s.reshape(bs, 1, d)
    wkv_sample = jnp.transpose(wkv_s, (3, 0, 1, 2))[None]
    shift_sample = _shift_row(proj_s[0])[None]
    chunk_v_sample = v_s.reshape(1, bs, 1, A_W)

    return (y_prompt, y_sample, wkv_prompt, shift_prompt, wkv_sample, shift_sample, chunk_v_sample)
```

```python
import functools
import math

import jax
import jax.numpy as jnp
import numpy as np
from jax import lax
from jax.experimental import pallas as pl
from jax.experimental.pallas import tpu as pltpu

F32 = jnp.float32
BF16 = jnp.bfloat16

D_MODEL = 2048
A_W = 1024
A_GROUPS = 8
CHUNK = 128
B_HEADS = 32
HEAD_SIZE = 64
B_W = B_HEADS * HEAD_SIZE
LORA = 96
NORM_EPS = 1e-6
GN_EPS = HEAD_SIZE * 1e-5
DECAY_SCALE = math.exp(-0.5)

LORA_W = 512
PROJ_W = 6 * B_W + 3 * A_W + LORA_W
COL_R, COL_K, COL_V, COL_ZB, COL_GA, COL_GB = 0, 1, 2, 3, 4, 5
COL_UA, COL_VA, COL_ZA = 12, 13, 14
COL_LORA = (6 * B_W + 3 * A_W) // LORA_W

V7X_LANES = 128
V7X_SUBLANES = 8
LANE_GROUP = B_HEADS
VMEM_LIMIT = 56 * 1024 * 1024


def _cparams(*sem):
    return pltpu.CompilerParams(dimension_semantics=sem, vmem_limit_bytes=VMEM_LIMIT)


def _mod_kernel(c_ref, w_ref, b_ref, o_ref):
    acc = jnp.dot(c_ref[...].astype(BF16), w_ref[...].astype(BF16), preferred_element_type=F32)
    o_ref[...] = acc + b_ref[...]


def _modulation(c_all, w_c, b_c):
    m, d = c_all.shape
    n = w_c.shape[1]
    tn = 768
    return pl.pallas_call(
        _mod_kernel,
        grid=(n // tn,),
        in_specs=[pl.BlockSpec((m, d), lambda j: (0, 0)),
                  pl.BlockSpec((d, tn), lambda j: (0, j)),
                  pl.BlockSpec((1, tn), lambda j: (0, j))],
        out_specs=pl.BlockSpec((m, tn), lambda j: (0, j)),
        out_shape=jax.ShapeDtypeStruct((m, n), F32),
        compiler_params=_cparams("arbitrary"),
        name="modulation",
    )(c_all, w_c, b_c.reshape(1, n))


def _normmod_kernel(x_ref, g_ref, sc_ref, sh_ref, o_ref):
    x = x_ref[0]
    ms = jnp.mean(x * x, axis=-1, keepdims=True)
    y = x * lax.rsqrt(ms + NORM_EPS) * g_ref[...]
    o_ref[0] = (y * (1.0 + sc_ref[0]) + sh_ref[0]).astype(BF16)


def _norm_modulate(x, g, scale, shift, tm):
    bsz, t, d = x.shape
    tm = min(tm, t)
    ts = scale.shape[1]
    sspec = (pl.BlockSpec((1, 1, d), lambda b, i: (b, 0, 0)) if ts == 1
             else pl.BlockSpec((1, tm, d), lambda b, i: (b, i, 0)))
    return pl.pallas_call(
        _normmod_kernel,
        grid=(bsz, t // tm),
        in_specs=[pl.BlockSpec((1, tm, d), lambda b, i: (b, i, 0)),
                  pl.BlockSpec((1, d), lambda b, i: (0, 0)),
                  sspec, sspec],
        out_specs=pl.BlockSpec((1, tm, d), lambda b, i: (b, i, 0)),
        out_shape=jax.ShapeDtypeStruct((bsz, t, d), BF16),
        compiler_params=_cparams("arbitrary", "arbitrary"),
        name="norm_modulate",
    )(x, g.reshape(1, d), scale, shift)


def _inproj_kernel(h_ref, hs_ref, w_ref, o_ref, os_ref):
    nt = (((1,), (1,)), ((), ()))
    o_ref[0] = lax.dot_general(h_ref[0], w_ref[...], nt, preferred_element_type=F32)

    @pl.when(pl.program_id(0) == 0)
    def _():
        os_ref[...] = lax.dot_general(w_ref[...], hs_ref[...], nt, preferred_element_type=F32)


def _in_projection(h, h_tok, w_t, tn=512):
    bsz, t, d = h.shape
    m = h_tok.shape[0]
    n = w_t.shape[0]
    nj = n // tn
    tok_out = pl.BlockSpec((tn, m), lambda b, j: (jnp.where(b == 0, j, nj - 1), 0))
    return pl.pallas_call(
        _inproj_kernel,
        grid=(bsz, nj),
        in_specs=[pl.BlockSpec((1, t, d), lambda b, j: (b, 0, 0)),
                  pl.BlockSpec((m, d), lambda b, j: (0, 0)),
                  pl.BlockSpec((tn, d), lambda b, j: (j, 0))],
        out_specs=[pl.BlockSpec((1, t, tn), lambda b, j: (b, 0, j)), tok_out],
        out_shape=[jax.ShapeDtypeStruct((bsz, t, n), F32), jax.ShapeDtypeStruct((n, m), F32)],
        compiler_params=_cparams("arbitrary", "arbitrary"),
        name="in_projection",
    )(h, h_tok, w_t)


def _lerp(p, q, mu):
    return p + mu * (q - p)


def _lora_logits(lo, w0, a0, w2p, a2p):
    col = lax.broadcasted_iota(jnp.int32, lo.shape, 1)
    lt = jnp.where(col < LORA, jnp.tanh(lo), lo).astype(BF16)
    wl = w0 + jnp.dot(lt, w2p, preferred_element_type=F32)
    al = a0 + jnp.dot(lt, a2p, preferred_element_type=F32)
    return wl, al


def _decay(wl):
    return jnp.exp(-DECAY_SCALE * jax.nn.sigmoid(wl))


def _prep_core(p, q, mu, w0, a0, w2p, a2p):
    r, k, v, lo = (_lerp(pi, qi, mi) for pi, qi, mi in zip(p, q, mu))
    wl, al = _lora_logits(lo, w0, a0, w2p, a2p)
    return r, _decay(wl), k, v, jax.nn.sigmoid(al)


def _move_lane_group(x, src_g, dst_g):
    shift = ((dst_g - src_g) % (V7X_LANES // LANE_GROUP)) * LANE_GROUP
    return x if shift == 0 else pltpu.roll(x, shift, axis=1)


def _swap_lane_groups(x):
    lane = lax.broadcasted_iota(jnp.int32, x[0].shape, 1)
    low_half = lane < 2 * LANE_GROUP
    even_group = (lane // LANE_GROUP) % 2 == 0
    rot = lambda v, s: pltpu.roll(v, s, axis=1)
    y0 = jnp.where(low_half, x[0], rot(x[2], 2 * LANE_GROUP))
    y2 = jnp.where(low_half, rot(x[0], 2 * LANE_GROUP), x[2])
    y1 = jnp.where(low_half, x[1], rot(x[3], 2 * LANE_GROUP))
    y3 = jnp.where(low_half, rot(x[1], 2 * LANE_GROUP), x[3])
    return [jnp.where(even_group, y0, rot(y1, LANE_GROUP)),
            jnp.where(even_group, rot(y0, 3 * LANE_GROUP), y1),
            jnp.where(even_group, y2, rot(y3, LANE_GROUP)),
            jnp.where(even_group, rot(y2, 3 * LANE_GROUP), y3)]


def _prep_seq_kernel(pr, pk, pv, plo, tr, tk, tv, tlo, sr, sk, sv, slo,
                     mr, mk, mv, mlo, w0, a0, w2p, a2p,
                     o_r, o_w, o_k, o_v, o_a):
    i = pl.program_id(1)

    def prev(p_ref, tail_ref, s_ref):
        p = p_ref[0]
        last = tail_ref[0][V7X_SUBLANES - 1:V7X_SUBLANES, :]
        first = jnp.where(i == 0, s_ref[0], last)
        rolled = pltpu.roll(p, 1, axis=0)
        row = lax.broadcasted_iota(jnp.int32, p.shape, 0)
        return p, jnp.where(row == 0, first, rolled)

    pq = [prev(x, y, z) for x, y, z in ((pr, tr, sr), (pk, tk, sk), (pv, tv, sv), (plo, tlo, slo))]
    outs = _prep_core([x[0] for x in pq], [x[1] for x in pq],
                      (mr[...], mk[...], mv[...], mlo[...]), w0[...], a0[...], w2p[...], a2p[...])
    for o, val in zip((o_r, o_w, o_k, o_v, o_a), outs):
        for q in range(B_W // V7X_LANES):
            o[0, q] = val[:, q * V7X_LANES:(q + 1) * V7X_LANES]


def _prep_tok_kernel(pr, pk, pv, plo, sr, sk, sv, slo,
                     mr, mk, mv, mlo, w0, a0, w2p, a2p,
                     o_r, o_w, o_k, o_v, o_a):
    outs = _prep_core((pr[0], pk[0], pv[0], plo[0]), (sr[...], sk[...], sv[...], slo[...]),
                      (mr[...], mk[...], mv[...], mlo[...]), w0[...], a0[...], w2p[...], a2p[...])
    for o, val in zip((o_r, o_w, o_k, o_v, o_a), outs):
        o[0] = val


def _rwkv_prep_seq(proj, shift0, mu, w0, a0, w2p, a2p, tm=256):
    bsz, t, _ = proj.shape
    tail = tm // V7X_SUBLANES
    nq = B_W // V7X_LANES
    cols = ((B_W, COL_R), (B_W, COL_K), (B_W, COL_V), (LORA_W, COL_LORA))
    cur = [pl.BlockSpec((1, tm, w), functools.partial(lambda b, i, c: (b, i, c), c=c)) for w, c in cols]
    tails = [pl.BlockSpec((1, V7X_SUBLANES, w),
                          functools.partial(lambda b, i, c: (b, jnp.maximum(i * tail - 1, 0), c), c=c))
             for w, c in cols]
    s0 = [pl.BlockSpec((1, 1, w), lambda b, i: (b, 0, 0)) for w, _ in cols]
    rowv = [pl.BlockSpec((1, w), lambda b, i: (0, 0)) for w, _ in cols]
    full = lambda shape: pl.BlockSpec(shape, lambda b, i: (0,) * len(shape))
    out_spec = pl.BlockSpec((1, nq, tm, V7X_LANES), lambda b, i: (b, 0, i, 0))
    return pl.pallas_call(
        _prep_seq_kernel,
        grid=(bsz, t // tm),
        in_specs=cur + tails + s0 + rowv + [full((1, B_W)), full((1, B_W)),
                                            full((LORA_W, B_W)), full((LORA_W, B_W))],
        out_specs=[out_spec] * 5,
        out_shape=[jax.ShapeDtypeStruct((bsz, nq, t, V7X_LANES), F32)] * 5,
        compiler_params=_cparams("arbitrary", "arbitrary"),
        name="rwkv_prep_seq",
    )(proj, proj, proj, proj, proj, proj, proj, proj, *shift0, *mu, w0, a0, w2p, a2p)


def _rwkv_prep_tok(proj, shift0, mu, w0, a0, w2p, a2p):
    _, m, _ = proj.shape
    cols = ((B_W, COL_R), (B_W, COL_K), (B_W, COL_V), (LORA_W, COL_LORA))
    cur = [pl.BlockSpec((1, m, w), functools.partial(lambda i, c: (0, 0, c), c=c)) for w, c in cols]
    s0 = [pl.BlockSpec((m, w), lambda i: (0, 0)) for w, _ in cols]
    rowv = [pl.BlockSpec((1, w), lambda i: (0, 0)) for w, _ in cols]
    full = lambda shape: pl.BlockSpec(shape, lambda i: (0,) * len(shape))
    out_spec = pl.BlockSpec((1, m, B_W), lambda i: (0, 0, 0))
    return pl.pallas_call(
        _prep_tok_kernel,
        grid=(1,),
        in_specs=cur + s0 + rowv + [full((1, B_W)), full((1, B_W)),
                                    full((LORA_W, B_W)), full((LORA_W, B_W))],
        out_specs=[out_spec] * 5,
        out_shape=[jax.ShapeDtypeStruct((1, m, B_W), F32)] * 5,
        compiler_params=_cparams("arbitrary"),
        name="rwkv_prep_tok",
    )(proj, proj, proj, proj, *shift0, *mu, w0, a0, w2p, a2p)


WKV_ROWS = 16


AHEAD = 2


def _wkv_kernel(xr, xw, xk, xv, xa, hr, hw, hk, hv, ha, s0_ref, kk_c, ka_c, rk_c, gg_c, gb_c,
                y_ref, s_ref, r_s, w_s, k_s, a_s, vn_s, vt_s, nkk_s, b_s, kp_s,
                p1_s, p2_s, p3_s, p4_s, p5_s, rks_s, cba_s, cka_s, cbr_s, ckr_s, ys_s,
                *, tt, nb):
    step = pl.program_id(0)
    ng = tt // V7X_SUBLANES
    pitch = tt + V7X_SUBLANES
    vpitch = HEAD_SIZE + V7X_SUBLANES

    @pl.when(step == 0)
    def _():
        s_ref[...] = s0_ref[...]

    dsts = (r_s, w_s, k_s, a_s, vn_s)

    def group_rows(rg):
        start = rg * V7X_SUBLANES
        if not isinstance(rg, int):
            start = pl.multiple_of(start, V7X_SUBLANES)
        return pl.ds(start, V7X_SUBLANES)

    def exchange(srcs, src_rg, dst_rg, q):
        src_rows = group_rows(src_rg)
        dst_rows = group_rows(dst_rg)
        for x, dst in zip(srcs, dsts):
            for b in range(nb):
                tile = x[b, q, src_rows, :]
                lanes = slice(b * LANE_GROUP, (b + 1) * LANE_GROUP)
                for g in range(nb):
                    dst[nb * q + g, dst_rows, lanes] = _move_lane_group(tile, g, b)[:, lanes]

    def v_by_token(rg, q):
        rows = group_rows(rg)
        for g in range(nb):
            n = nb * q + g
            vt_s[pl.ds(rg * V7X_SUBLANES * vpitch + n, V7X_SUBLANES, stride=vpitch), :] = vn_s[n, rows, :]

    def prep(m):
        ts = group_rows(m)
        sq = None
        for n in range(HEAD_SIZE):
            kk = k_s[n, ts, :] * kk_c[n]
            sq = kk * kk if sq is None else sq + kk * kk
        den = jnp.maximum(jnp.sqrt(sq), 1e-12)
        nxt_tok = lambda x: pltpu.roll(x, V7X_SUBLANES - 1, axis=0)
        acc = [None] * 5
        for n in range(HEAD_SIZE):
            k = k_s[n, ts, :]
            a = a_s[n, ts, :]
            r = r_s[n, ts, :]
            w = w_s[n, ts, :]
            kk = (k * kk_c[n]) / den
            nkk = -kk
            b = kk * a
            kp = k * (1.0 + (a - 1.0) * ka_c[n])
            nkk_n = nxt_tok(nkk)
            w_n = nxt_tok(w)
            nkk_s[n, ts, :] = nkk
            b_s[n, ts, :] = b
            kp_s[n, ts, :] = kp
            p1_s[n, ts, :] = w * r
            p2_s[n, ts, :] = w * w_n
            p3_s[n, ts, :] = b * w_n
            p4_s[n, ts, :] = kp * w_n
            p5_s[n, ts, :] = w * nkk_n
            kpr = kp * r
            terms = (b * nkk_n, kp * nkk_n, b * r, kpr, kpr * rk_c[n])
            acc = [x if y is None else y + x for x, y in zip(terms, acc)]
        for ref, val in zip((cba_s, cka_s, cbr_s, ckr_s, rks_s), acc):
            ref[ts, :] = val

    cur = (xr, xw, xk, xa, xv)
    nxt = (hr, hw, hk, ha, hv)

    @pl.when(step == 0)
    def _():
        for rg in range(AHEAD):
            for q in range(B_W // V7X_LANES):
                exchange(cur, rg, rg, q)
                v_by_token(rg, q)

    @pl.when(step > 0)
    def _():
        head = AHEAD * V7X_SUBLANES
        for dst in dsts:
            dst[:, 0:head, :] = dst[:, tt:tt + head, :]
        vt_s[0:head * vpitch, :] = vt_s[tt * vpitch:(tt + head) * vpitch, :]

    prep(0)

    parts = [h * WKV_ROWS for h in range(HEAD_SIZE // WKV_ROWS)]
    zeros = lambda: jnp.zeros((WKV_ROWS, V7X_LANES), F32)

    def bcast(ref, t, j):
        return jnp.broadcast_to(ref[j, pl.ds(t, 1), :], (WKV_ROWS, V7X_LANES))

    def lane_row(ref, t):
        return jnp.broadcast_to(ref[pl.ds(t, 1), :], (WKV_ROWS, V7X_LANES))

    def first_carry(i0):
        ua, xb = zeros(), zeros()
        for j in range(HEAD_SIZE):
            s = s_ref[j, i0:i0 + WKV_ROWS, :]
            ua = ua + s * bcast(nkk_s, 0, j)
            xb = xb + s * bcast(p5_s, 0, j)
        return ua, xb

    def group(m, carry, srcs, src_rg):
        if m + 1 < ng:
            prep(m + 1)
        pairs = V7X_SUBLANES // 2

        def pair(p, carry):
            ta = m * V7X_SUBLANES + 2 * p
            tb = ta + 1
            tc = jnp.minimum(ta + 2, tt - 2)
            out = []
            for i0, (ua, xb) in zip(parts, carry):
                va = vt_s[pl.ds(pl.multiple_of(ta * vpitch + i0, V7X_SUBLANES), WKV_ROWS), :]
                vb = vt_s[pl.ds(pl.multiple_of(tb * vpitch + i0, V7X_SUBLANES), WKV_ROWS), :]
                ub = (xb + ua * lane_row(cba_s, ta)) + va * lane_row(cka_s, ta)
                ya, yb, uc, xd = zeros(), zeros(), zeros(), zeros()
                for j in range(HEAD_SIZE):
                    s = s_ref[j, i0:i0 + WKV_ROWS, :]
                    ya = ya + s * bcast(p1_s, ta, j)
                    sn = ((((s * bcast(p2_s, ta, j) + ua * bcast(p3_s, ta, j)) + va * bcast(p4_s, ta, j))
                           + ub * bcast(b_s, tb, j)) + vb * bcast(kp_s, tb, j))
                    s_ref[j, i0:i0 + WKV_ROWS, :] = sn
                    yb = yb + sn * bcast(r_s, tb, j)
                    uc = uc + sn * bcast(nkk_s, tc, j)
                    xd = xd + sn * bcast(p5_s, tc, j)
                ya = (ya + ua * lane_row(cbr_s, ta)) + va * lane_row(ckr_s, ta)
                ys_s[pl.ds(i0 * pitch + ta, WKV_ROWS, stride=pitch), :] = ya
                ys_s[pl.ds(i0 * pitch + tb, WKV_ROWS, stride=pitch), :] = yb
                out.append((uc, xd))
            prev = m * pairs + p + pairs - 1
            prev_rg = lax.shift_right_logical(prev, 2) - 1 + AHEAD
            prev_q = 4 * (prev & (pairs - 1))
            for d in range(4):
                v_by_token(prev_rg, prev_q + d)
                exchange(srcs, src_rg, m + AHEAD, 4 * p + d)
            return tuple(out)

        return lax.fori_loop(0, pairs, pair, carry)

    carry = tuple(first_carry(i0) for i0 in parts)
    for m in range(ng):
        carry = (group(m, carry, cur, m + AHEAD) if m + AHEAD < ng
                 else group(m, carry, nxt, m + AHEAD - ng))
    for d in range(4):
        v_by_token(ng - 1 + AHEAD, B_W // V7X_LANES - 4 + d)

    def post(c, carry):
        t0 = pl.multiple_of(c * V7X_SUBLANES, V7X_SUBLANES)
        ys = [ys_s[pl.ds(i * pitch + t0, V7X_SUBLANES), :] for i in range(HEAD_SIZE)]
        mu = sum(ys[1:], ys[0]) * (1.0 / HEAD_SIZE)
        yc = [y - mu for y in ys]
        sq = [y * y for y in yc]
        rstd = lax.rsqrt(sum(sq[1:], sq[0]) * (1.0 / HEAD_SIZE) + GN_EPS)
        scale = rks_s[pl.ds(t0, V7X_SUBLANES), :]
        out = [yc[i] * rstd * gg_c[i:i + 1, :] + gb_c[i:i + 1, :]
               + scale * vn_s[i, pl.ds(t0, V7X_SUBLANES), :] for i in range(HEAD_SIZE)]
        for q in range(HEAD_SIZE // nb):
            nat = _swap_lane_groups(out[nb * q:nb * (q + 1)])
            for b in range(nb):
                y_ref[b, pl.ds(t0, V7X_SUBLANES), q * V7X_LANES:(q + 1) * V7X_LANES] = nat[b]
        return carry

    lax.fori_loop(0, ng, post, 0, unroll=2)


def _wkv(r, w, k, v, a, s0, consts, tt):
    nb, nq, t, lanes = r.shape
    n = HEAD_SIZE
    assert lanes == V7X_LANES and nb * B_HEADS == lanes and nq * lanes == B_W
    head = AHEAD * V7X_SUBLANES
    assert tt % head == 0 and tt // V7X_SUBLANES > AHEAD
    seq = pl.BlockSpec((nb, nq, tt, lanes), lambda i: (0, 0, i, 0))
    nxt = pl.BlockSpec((nb, nq, head, lanes),
                       lambda i: (0, 0, jnp.minimum((i + 1) * (tt // head), t // head - 1), 0))
    st = pl.BlockSpec((n, n, lanes), lambda i: (0, 0, 0))
    c3 = pl.BlockSpec((n, 1, lanes), lambda i: (0, 0, 0))
    c2 = pl.BlockSpec((n, lanes), lambda i: (0, 0))
    kk_c, ka_c, rk_c, gg_c, gb_c = consts
    col = lambda c: c.reshape(n, 1, lanes)
    pitch = tt + V7X_SUBLANES
    by_n = pltpu.VMEM((n, tt, lanes), F32)
    lead_n = pltpu.VMEM((n, tt + head, lanes), F32)
    return pl.pallas_call(
        functools.partial(_wkv_kernel, tt=tt, nb=nb),
        grid=(t // tt,),
        in_specs=[seq] * 5 + [nxt] * 5
        + [pl.BlockSpec((n, n, lanes), lambda i: (0, 0, 0), pipeline_mode=pl.Buffered(1)),
           c3, c3, c3, c2, c2],
        out_specs=[pl.BlockSpec((nb, tt, B_W), lambda i: (0, i, 0)), st],
        out_shape=[jax.ShapeDtypeStruct((nb, t, B_W), F32),
                   jax.ShapeDtypeStruct((n, n, lanes), F32)],
        scratch_shapes=[lead_n] * 5
        + [pltpu.VMEM(((tt + head) * (n + V7X_SUBLANES), lanes), F32)] + [by_n] * 8
        + [pltpu.VMEM((tt, lanes), F32)] * 5 + [pltpu.VMEM((n * pitch, lanes), F32)],
        compiler_params=_cparams("arbitrary"),
        name="wkv",
    )(r, w, k, v, a, r, w, k, v, a, s0, col(kk_c), col(ka_c), col(rk_c), gg_c, gb_c)


def _wkv_tok_kernel(r_ref, w_ref, k_ref, v_ref, a_ref, s0_ref, kk_c, ka_c, rk_c, gg_c, gb_c,
                    y_ref, s_ref, ys_s):
    k = k_ref[0]
    a = a_ref[0]
    r = r_ref[0]
    w = w_ref[0]
    v = v_ref[0]
    kk = k * kk_c[0]
    nrm = jnp.sqrt(jnp.sum(kk * kk, axis=0, keepdims=True))
    kk = kk / jnp.maximum(nrm, 1e-12)
    nkk = -kk
    bvec = kk * a
    kp = k * (1.0 + (a - 1.0) * ka_c[0])
    wr = w * r
    br = jnp.sum(bvec * r, axis=0, keepdims=True)
    kr = jnp.sum(kp * r, axis=0, keepdims=True)
    for i in range(HEAD_SIZE):
        s = s0_ref[0, i]
        u = jnp.sum(s * nkk, axis=0, keepdims=True)
        yw = jnp.sum(s * wr, axis=0, keepdims=True)
        vi = v[i:i + 1, :]
        s_ref[0, i] = (s * w + u * bvec) + vi * kp
        ys_s[i:i + 1, :] = (yw + u * br) + vi * kr
    y = ys_s[...]
    mu = jnp.mean(y, axis=0, keepdims=True)
    yc = y - mu
    var = jnp.mean(yc * yc, axis=0, keepdims=True)
    yn = yc * lax.rsqrt(var + GN_EPS) * gg_c[0] + gb_c[0]
    bonus = jnp.sum(r * kp * rk_c[0], axis=0, keepdims=True) * v
    y_ref[0] = yn + bonus


def _wkv_tok(r, w, k, v, a, s0, consts):
    nh, n, bsz = r.shape
    vec = pl.BlockSpec((1, n, bsz), lambda h: (h, 0, 0))
    st = pl.BlockSpec((1, n, n, bsz), lambda h: (h, 0, 0, 0))
    return pl.pallas_call(
        _wkv_tok_kernel,
        grid=(nh,),
        in_specs=[vec] * 5 + [st] + [vec] * 5,
        out_specs=[vec, st],
        out_shape=[jax.ShapeDtypeStruct((nh, n, bsz), F32),
                   jax.ShapeDtypeStruct(s0.shape, F32)],
        scratch_shapes=[pltpu.VMEM((n, bsz), F32)],
        compiler_params=_cparams("arbitrary"),
        name="wkv_tok",
    )(r, w, k, v, a, s0, *consts)


def _head_const(x, bsz):
    return jnp.tile(x.reshape(B_HEADS, HEAD_SIZE).T, (1, bsz))


def _gelu(x):
    return jax.nn.gelu(x)


def _layernorm(x, g, b):
    mu = jnp.mean(x, axis=-1, keepdims=True)
    xc = x - mu
    var = jnp.mean(xc * xc, axis=-1, keepdims=True)
    return xc * lax.rsqrt(var + NORM_EPS) * g + b


def _branch_a_seq_kernel(u_ref, v_ref, z_ref, g_ref, lg, lb, ws_ref, bias_ref, pa_ref, o_ref, *, tm):
    u = _gelu(u_ref[0])
    v = _layernorm(_gelu(v_ref[0]), lg[...], lb[...])
    vb = v.astype(BF16)
    row = lax.broadcasted_iota(jnp.int32, (CHUNK, CHUNK), 0)
    col = lax.broadcasted_iota(jnp.int32, (CHUNK, CHUNK), 1)
    rows = []
    for c in range(tm // CHUNK):
        blocks = []
        for g in range(A_GROUPS):
            wm = jnp.where(row >= col, ws_ref[g], 0.0).astype(BF16)
            blk = vb[c * CHUNK:(c + 1) * CHUNK, g * CHUNK:(g + 1) * CHUNK]
            blocks.append(jnp.dot(wm, blk, preferred_element_type=F32))
        rows.append(jnp.concatenate(blocks, axis=1) + bias_ref[...])
    mix = jnp.concatenate(rows, axis=0)
    ya = u * mix * jax.nn.silu(z_ref[0])
    o_ref[0] = jax.nn.sigmoid(g_ref[0]) * jnp.dot(ya.astype(BF16), pa_ref[...],
                                                  preferred_element_type=F32)


def _branch_a_tok_kernel(u_ref, v_ref, z_ref, g_ref, lg, lb, w00, b00, pa_ref, o_ref, vo_ref):
    u = _gelu(u_ref[0])
    v = _layernorm(_gelu(v_ref[0]), lg[...], lb[...])
    vo_ref[...] = v
    mix = v * w00[...] + b00[...]
    ya = u * mix * jax.nn.silu(z_ref[0])
    o_ref[0] = jax.nn.sigmoid(g_ref[0]) * jnp.dot(ya.astype(BF16), pa_ref[...],
                                                  preferred_element_type=F32)


def _branch_a_seq(proj, ln_g, ln_b, w_s, bias2d, p_a, tm=512):
    bsz, t, _ = proj.shape
    tm = min(tm, t)
    full = lambda shape: pl.BlockSpec(shape, lambda b, i: (0,) * len(shape))
    sec = lambda w, c: pl.BlockSpec((1, tm, w), lambda b, i: (b, i, c))
    return pl.pallas_call(
        functools.partial(_branch_a_seq_kernel, tm=tm),
        grid=(bsz, t // tm),
        in_specs=[sec(A_W, COL_UA), sec(A_W, COL_VA), sec(A_W, COL_ZA), sec(D_MODEL, COL_GA),
                  full((1, A_W)), full((1, A_W)), full((A_GROUPS, CHUNK, CHUNK)),
                  full((CHUNK, A_W)), full((A_W, D_MODEL))],
        out_specs=pl.BlockSpec((1, tm, D_MODEL), lambda b, i: (b, i, 0)),
        out_shape=jax.ShapeDtypeStruct((bsz, t, D_MODEL), F32),
        compiler_params=_cparams("arbitrary", "arbitrary"),
        name="branch_a_seq",
    )(proj, proj, proj, proj, ln_g, ln_b, w_s, bias2d, p_a)


def _branch_a_tok(proj, ln_g, ln_b, w00, b00, p_a):
    _, m, _ = proj.shape
    full = lambda shape: pl.BlockSpec(shape, lambda i: (0,) * len(shape))
    sec = lambda w, c: pl.BlockSpec((1, m, w), lambda i: (0, 0, c))
    return pl.pallas_call(
        _branch_a_tok_kernel,
        grid=(1,),
        in_specs=[sec(A_W, COL_UA), sec(A_W, COL_VA), sec(A_W, COL_ZA), sec(D_MODEL, COL_GA),
                  full((1, A_W)), full((1, A_W)), full((1, A_W)), full((1, A_W)),
                  full((A_W, D_MODEL))],
        out_specs=[pl.BlockSpec((1, m, D_MODEL), lambda i: (0, 0, 0)),
                   pl.BlockSpec((m, A_W), lambda i: (0, 0))],
        out_shape=[jax.ShapeDtypeStruct((1, m, D_MODEL), F32),
                   jax.ShapeDtypeStruct((m, A_W), F32)],
        compiler_params=_cparams("arbitrary"),
        name="branch_a_tok",
    )(proj, proj, proj, proj, ln_g, ln_b, w00, b00, p_a)


def _merge_kernel(yb_ref, zb_ref, gb_ref, oa_ref, x_ref, cg_ref, pb_ref, wo_ref, fg_ref, o_ref):
    yb = (yb_ref[0] * jax.nn.silu(zb_ref[0])).astype(BF16)
    merged = oa_ref[0] + jax.nn.sigmoid(gb_ref[0]) * jnp.dot(yb, pb_ref[...],
                                                            preferred_element_type=F32)
    out = x_ref[0] + cg_ref[0] * jnp.dot(merged.astype(BF16), wo_ref[...],
                                         preferred_element_type=F32)
    ms = jnp.mean(out * out, axis=-1, keepdims=True)
    o_ref[0] = out * lax.rsqrt(ms + NORM_EPS) * fg_ref[...]


def _merge(yb, proj, oa, x, c_gate, p_b, w_out, final_g, tm):
    bsz, t, d = x.shape
    tm = min(tm, t)
    tg = c_gate.shape[1]
    row = lambda: pl.BlockSpec((1, tm, d), lambda b, i: (b, i, 0))
    sec = lambda c: pl.BlockSpec((1, tm, d), lambda b, i: (b, i, c))
    gate = (pl.BlockSpec((1, 1, d), lambda b, i: (b, 0, 0)) if tg == 1
            else pl.BlockSpec((1, tm, d), lambda b, i: (b, i, 0)))
    wspec = lambda: pl.BlockSpec((d, d), lambda b, i: (0, 0), pipeline_mode=pl.Buffered(1))
    return pl.pallas_call(
        _merge_kernel,
        grid=(bsz, t // tm),
        in_specs=[row(), sec(COL_ZB), sec(COL_GB), row(), row(), gate, wspec(), wspec(),
                  pl.BlockSpec((1, d), lambda b, i: (0, 0))],
        out_specs=row(),
        out_shape=jax.ShapeDtypeStruct((bsz, t, d), F32),
        compiler_params=_cparams("arbitrary", "arbitrary"),
        name="merge",
    )(yb, proj, proj, oa, x, c_gate, p_b, w_out, final_g.reshape(1, d))


def _split(t, sizes):
    out, o = [], 0
    for s in sizes:
        out.append(t[..., o:o + s])
        o += s
    return out


def _permute_shift_cols(x):
    r, wd, k, v, ad = _split(x, (B_W, LORA, B_W, B_W, LORA))
    pad = jnp.zeros(x.shape[:-1] + (LORA_W - 2 * LORA,), x.dtype)
    return r, k, v, jnp.concatenate([wd, ad, pad], axis=-1)


def _nh_order(x, axis=-1):
    axis = axis % x.ndim
    shape = x.shape
    x = x.reshape(shape[:axis] + (B_HEADS, HEAD_SIZE) + shape[axis + 1:])
    return jnp.swapaxes(x, axis, axis + 1).reshape(shape)


def _hn_order(x, axis=-1):
    axis = axis % x.ndim
    shape = x.shape
    x = x.reshape(shape[:axis] + (HEAD_SIZE, B_HEADS) + shape[axis + 1:])
    return jnp.swapaxes(x, axis, axis + 1).reshape(shape)


def _shift_row(proj_row):
    r = _hn_order(proj_row[..., 0:B_W])
    k = _hn_order(proj_row[..., B_W:2 * B_W])
    v = _hn_order(proj_row[..., 2 * B_W:3 * B_W])
    lo = proj_row[..., COL_LORA * LORA_W:]
    return jnp.concatenate([r, lo[..., :LORA], k, v, lo[..., LORA:2 * LORA]], axis=-1)


def kernel(x_prompt, x_sample, c_prompt, c_sample, state_wkv, state_shift, norm_g, w_c, b_c, w_in, ln_v_g, ln_v_b, w_s, b_s, mu_shift, w0, w2, a0, a2, k_k, k_a, r_k, gn_g, gn_b, p_a, p_b, w_out, final_g):
    assert norm_g.shape[0] == 1, "single-layer trunk"
    bp, t, d = x_prompt.shape
    bs = x_sample.shape[0]

    w_rows = jnp.swapaxes(w_in[0], 0, 1)
    bounds = np.cumsum([0, A_W, A_W, A_W, B_W, LORA, B_W, B_W, LORA, B_W, D_MODEL, D_MODEL])
    ua, va, za, wr, wwd, wk, wv, wad, zb, ga, gb = (w_rows[a:b] for a, b in zip(bounds[:-1], bounds[1:]))
    wpad = jnp.zeros((LORA_W - 2 * LORA, d), F32)
    nh0 = lambda x: _nh_order(x, axis=0)
    w_p = jnp.concatenate([nh0(wr), nh0(wk), nh0(wv), nh0(zb), ga, gb, ua, va, za, wwd, wad, wpad],
                          axis=0).astype(BF16)

    def shift_cols(x):
        xr, xk, xv, xlo = _permute_shift_cols(x)
        return _nh_order(xr), _nh_order(xk), _nh_order(xv), xlo

    mu = [m.reshape(1, -1) for m in shift_cols(mu_shift[0])]
    zrow = jnp.zeros((LORA_W - LORA, B_W), F32)
    w2p = jnp.concatenate([_nh_order(w2[0]), zrow], axis=0).astype(BF16)
    a2p = jnp.concatenate([jnp.zeros((LORA, B_W), F32), _nh_order(a2[0]),
                           jnp.zeros((LORA_W - 2 * LORA, B_W), F32)], axis=0).astype(BF16)
    w0r, a0r = _nh_order(w0[0]).reshape(1, B_W), _nh_order(a0[0]).reshape(1, B_W)
    p_a_b, w_out_b = p_a[0].astype(BF16), w_out[0].astype(BF16)
    p_b_b = nh0(p_b[0]).astype(BF16)
    ln_g, ln_b = ln_v_g[0].reshape(1, A_W), ln_v_b[0].reshape(1, A_W)
    bias2d = jnp.repeat(b_s[0].T, CHUNK, axis=1)
    w00 = jnp.repeat(w_s[0][:, 0, 0], CHUNK).reshape(1, A_W)
    b00 = jnp.repeat(b_s[0][:, 0], CHUNK).reshape(1, A_W)

    m_rows = bp + bs
    m_pad = -m_rows % V7X_SUBLANES
    c_all = jnp.concatenate([c_prompt, c_sample, jnp.zeros((m_pad, d), F32)], axis=0)
    mod = _modulation(c_all, w_c[0], b_c[0])
    shift_p, scale_p, gate_p = (mod[:bp, i * d:(i + 1) * d].reshape(bp, 1, d) for i in range(3))
    shift_s, scale_s, gate_s = (mod[bp:m_rows, i * d:(i + 1) * d].reshape(1, bs, d) for i in range(3))

    def wkv_consts(bsz):
        return [_head_const(c, bsz) for c in (k_k[0], k_a[0], r_k[0], gn_g[0], gn_b[0])]

    xs = x_sample.reshape(1, bs, d)
    h_p = _norm_modulate(x_prompt, norm_g[0], scale_p, shift_p, tm=1024)
    h_s = _norm_modulate(xs, norm_g[0], scale_s, shift_s, tm=bs)
    proj_p, proj_s = _in_projection(h_p, h_s[0], w_p)
    proj_s = proj_s.T[None]

    zeros_shift = [jnp.zeros((bp, 1, w), F32) for w in (B_W, B_W, B_W, LORA_W)]
    feats = _rwkv_prep_seq(proj_p, zeros_shift, mu, w0r, a0r, w2p, a2p)
    s0_p = jnp.zeros((HEAD_SIZE, HEAD_SIZE, bp * B_HEADS), F32)
    yb_p, s_hl = _wkv(*feats, s0_p, wkv_consts(bp), tt=32)
    oa_p = _branch_a_seq(proj_p, ln_g, ln_b, w_s[0], bias2d, p_a_b)
    y_prompt = _merge(yb_p, proj_p, oa_p, x_prompt, gate_p, p_b_b, w_out_b, final_g, tm=256)
    wkv_prompt = jnp.transpose(s_hl.reshape(HEAD_SIZE, HEAD_SIZE, bp, B_HEADS), (2, 3, 1, 0))[None]
    shift_prompt = _shift_row(proj_p[:, t - 1, :])[None]

    feats = _rwkv_prep_tok(proj_s, shift_cols(state_shift[0]), mu, w0r, a0r, w2p, a2p)
    feats = [jnp.transpose(f.reshape(bs, HEAD_SIZE, B_HEADS), (2, 1, 0)) for f in feats]
    tok_consts = [jnp.broadcast_to(c.reshape(B_HEADS, HEAD_SIZE, 1), (B_HEADS, HEAD_SIZE, bs))
                  for c in (k_k[0], k_a[0], r_k[0], gn_g[0], gn_b[0])]
    yb_s, wkv_s = _wkv_tok(*feats, jnp.transpose(state_wkv[0], (1, 2, 3, 0)), tok_consts)
    yb_s = jnp.transpose(yb_s, (2, 1, 0)).reshape(1, bs, B_W)
    oa_s, v_s = _branch_a_tok(proj_s, ln_g, ln_b, w00, b00, p_a_b)
    y_s = _merge(yb_s, proj_s, oa_s, xs, gate_s, p_b_b, w_out_b, final_g, tm=bs)
    y_sample = y_s.reshape(bs, 1, d)
    wkv_sample = jnp.transpose(wkv_s, (3, 0, 1, 2))[None]
    shift_sample = _shift_row(proj_s[0])[None]
    chunk_v_sample = v_s.reshape(1, bs, 1, A_W)

    return (y_prompt, y_sample, wkv_prompt, shift_prompt, wkv_sample, shift_sample, chunk_v_sample)
```

```python
import functools
import math

import jax
import jax.numpy as jnp
import numpy as np
from jax import lax
from jax.experimental import pallas as pl
from jax.experimental.pallas import tpu as pltpu

F32 = jnp.float32
BF16 = jnp.bfloat16

D_MODEL = 2048
A_W = 1024
A_GROUPS = 8
CHUNK = 128
B_HEADS = 32
HEAD_SIZE = 64
B_W = B_HEADS * HEAD_SIZE
LORA = 96
NORM_EPS = 1e-6
GN_EPS = HEAD_SIZE * 1e-5
DECAY_SCALE = math.exp(-0.5)

LORA_W = 512
PROJ_W = 6 * B_W + 3 * A_W + LORA_W
COL_R, COL_K, COL_V, COL_ZB, COL_GA, COL_GB = 0, 1, 2, 3, 4, 5
COL_UA, COL_VA, COL_ZA = 12, 13, 14
COL_LORA = (6 * B_W + 3 * A_W) // LORA_W

V7X_LANES = 128
V7X_SUBLANES = 8
LANE_GROUP = B_HEADS
VMEM_LIMIT = 56 * 1024 * 1024


def _cparams(*sem):
    return pltpu.CompilerParams(dimension_semantics=sem, vmem_limit_bytes=VMEM_LIMIT)


def _mod_kernel(c_ref, w_ref, b_ref, o_ref):
    acc = jnp.dot(c_ref[...].astype(BF16), w_ref[...].astype(BF16), preferred_element_type=F32)
    o_ref[...] = acc + b_ref[...]


def _modulation(c_all, w_c, b_c):
    m, d = c_all.shape
    n = w_c.shape[1]
    tn = 768
    return pl.pallas_call(
        _mod_kernel,
        grid=(n // tn,),
        in_specs=[pl.BlockSpec((m, d), lambda j: (0, 0)),
                  pl.BlockSpec((d, tn), lambda j: (0, j)),
                  pl.BlockSpec((1, tn), lambda j: (0, j))],
        out_specs=pl.BlockSpec((m, tn), lambda j: (0, j)),
        out_shape=jax.ShapeDtypeStruct((m, n), F32),
        compiler_params=_cparams("arbitrary"),
        name="modulation",
    )(c_all, w_c, b_c.reshape(1, n))


def _normmod_kernel(x_ref, g_ref, sc_ref, sh_ref, o_ref):
    x = x_ref[0]
    ms = jnp.mean(x * x, axis=-1, keepdims=True)
    y = x * lax.rsqrt(ms + NORM_EPS) * g_ref[...]
    o_ref[0] = (y * (1.0 + sc_ref[0]) + sh_ref[0]).astype(BF16)


def _norm_modulate(x, g, scale, shift, tm):
    bsz, t, d = x.shape
    tm = min(tm, t)
    ts = scale.shape[1]
    sspec = (pl.BlockSpec((1, 1, d), lambda b, i: (b, 0, 0)) if ts == 1
             else pl.BlockSpec((1, tm, d), lambda b, i: (b, i, 0)))
    return pl.pallas_call(
        _normmod_kernel,
        grid=(bsz, t // tm),
        in_specs=[pl.BlockSpec((1, tm, d), lambda b, i: (b, i, 0)),
                  pl.BlockSpec((1, d), lambda b, i: (0, 0)),
                  sspec, sspec],
        out_specs=pl.BlockSpec((1, tm, d), lambda b, i: (b, i, 0)),
        out_shape=jax.ShapeDtypeStruct((bsz, t, d), BF16),
        compiler_params=_cparams("arbitrary", "arbitrary"),
        name="norm_modulate",
    )(x, g.reshape(1, d), scale, shift)


def _inproj_kernel(h_ref, hs_ref, w_ref, o_ref, os_ref):
    nt = (((1,), (1,)), ((), ()))
    o_ref[0] = lax.dot_general(h_ref[0], w_ref[...], nt, preferred_element_type=F32)

    @pl.when(pl.program_id(0) == 0)
    def _():
        os_ref[...] = lax.dot_general(hs_ref[...], w_ref[...], nt, preferred_element_type=F32)


def _in_projection(h, h_tok, w_t, tn=512):
    bsz, t, d = h.shape
    m = h_tok.shape[0]
    n = w_t.shape[0]
    nj = n // tn
    tok_out = pl.BlockSpec((m, tn), lambda b, j: (0, jnp.where(b == 0, j, nj - 1)))
    return pl.pallas_call(
        _inproj_kernel,
        grid=(bsz, nj),
        in_specs=[pl.BlockSpec((1, t, d), lambda b, j: (b, 0, 0)),
                  pl.BlockSpec((m, d), lambda b, j: (0, 0)),
                  pl.BlockSpec((tn, d), lambda b, j: (j, 0))],
        out_specs=[pl.BlockSpec((1, t, tn), lambda b, j: (b, 0, j)), tok_out],
        out_shape=[jax.ShapeDtypeStruct((bsz, t, n), F32), jax.ShapeDtypeStruct((m, n), F32)],
        compiler_params=_cparams("arbitrary", "arbitrary"),
        name="in_projection",
    )(h, h_tok, w_t)


def _lerp(p, q, mu):
    return p + mu * (q - p)


def _lora_logits(lo, w0, a0, w2p, a2p):
    col = lax.broadcasted_iota(jnp.int32, lo.shape, 1)
    lt = jnp.where(col < LORA, jnp.tanh(lo), lo).astype(BF16)
    wl = w0 + jnp.dot(lt, w2p, preferred_element_type=F32)
    al = a0 + jnp.dot(lt, a2p, preferred_element_type=F32)
    return wl, al


def _decay(wl):
    return jnp.exp(-DECAY_SCALE * jax.nn.sigmoid(wl))


def _prep_core(p, q, mu, w0, a0, w2p, a2p):
    r, k, v, lo = (_lerp(pi, qi, mi) for pi, qi, mi in zip(p, q, mu))
    wl, al = _lora_logits(lo, w0, a0, w2p, a2p)
    return r, _decay(wl), k, v, jax.nn.sigmoid(al)


def _move_lane_group(x, src_g, dst_g):
    shift = ((dst_g - src_g) % (V7X_LANES // LANE_GROUP)) * LANE_GROUP
    return x if shift == 0 else pltpu.roll(x, shift, axis=1)


def _swap_lane_groups(x):
    lane = lax.broadcasted_iota(jnp.int32, x[0].shape, 1)
    low_half = lane < 2 * LANE_GROUP
    even_group = (lane // LANE_GROUP) % 2 == 0
    rot = lambda v, s: pltpu.roll(v, s, axis=1)
    y0 = jnp.where(low_half, x[0], rot(x[2], 2 * LANE_GROUP))
    y2 = jnp.where(low_half, rot(x[0], 2 * LANE_GROUP), x[2])
    y1 = jnp.where(low_half, x[1], rot(x[3], 2 * LANE_GROUP))
    y3 = jnp.where(low_half, rot(x[1], 2 * LANE_GROUP), x[3])
    return [jnp.where(even_group, y0, rot(y1, LANE_GROUP)),
            jnp.where(even_group, rot(y0, 3 * LANE_GROUP), y1),
            jnp.where(even_group, y2, rot(y3, LANE_GROUP)),
            jnp.where(even_group, rot(y2, 3 * LANE_GROUP), y3)]


def _prep_seq_kernel(pr, pk, pv, plo, tr, tk, tv, tlo, sr, sk, sv, slo,
                     mr, mk, mv, mlo, w0, a0, w2p, a2p,
                     o_r, o_w, o_k, o_v, o_a):
    i = pl.program_id(1)

    def prev(p_ref, tail_ref, s_ref):
        p = p_ref[0]
        last = tail_ref[0][V7X_SUBLANES - 1:V7X_SUBLANES, :]
        first = jnp.where(i == 0, s_ref[0], last)
        rolled = pltpu.roll(p, 1, axis=0)
        row = lax.broadcasted_iota(jnp.int32, p.shape, 0)
        return p, jnp.where(row == 0, first, rolled)

    pq = [prev(x, y, z) for x, y, z in ((pr, tr, sr), (pk, tk, sk), (pv, tv, sv), (plo, tlo, slo))]
    outs = _prep_core([x[0] for x in pq], [x[1] for x in pq],
                      (mr[...], mk[...], mv[...], mlo[...]), w0[...], a0[...], w2p[...], a2p[...])
    for o, val in zip((o_r, o_w, o_k, o_v, o_a), outs):
        for q in range(B_W // V7X_LANES):
            o[0, q] = val[:, q * V7X_LANES:(q + 1) * V7X_LANES]


def _prep_tok_kernel(pr, pk, pv, plo, sr, sk, sv, slo,
                     mr, mk, mv, mlo, w0, a0, w2p, a2p,
                     o_r, o_w, o_k, o_v, o_a):
    outs = _prep_core((pr[0], pk[0], pv[0], plo[0]), (sr[...], sk[...], sv[...], slo[...]),
                      (mr[...], mk[...], mv[...], mlo[...]), w0[...], a0[...], w2p[...], a2p[...])
    for o, val in zip((o_r, o_w, o_k, o_v, o_a), outs):
        o[0] = val


def _rwkv_prep_seq(proj, shift0, mu, w0, a0, w2p, a2p, tm=256):
    bsz, t, _ = proj.shape
    tail = tm // V7X_SUBLANES
    nq = B_W // V7X_LANES
    cols = ((B_W, COL_R), (B_W, COL_K), (B_W, COL_V), (LORA_W, COL_LORA))
    cur = [pl.BlockSpec((1, tm, w), functools.partial(lambda b, i, c: (b, i, c), c=c)) for w, c in cols]
    tails = [pl.BlockSpec((1, V7X_SUBLANES, w),
                          functools.partial(lambda b, i, c: (b, jnp.maximum(i * tail - 1, 0), c), c=c))
             for w, c in cols]
    s0 = [pl.BlockSpec((1, 1, w), lambda b, i: (b, 0, 0)) for w, _ in cols]
    rowv = [pl.BlockSpec((1, w), lambda b, i: (0, 0)) for w, _ in cols]
    full = lambda shape: pl.BlockSpec(shape, lambda b, i: (0,) * len(shape))
    out_spec = pl.BlockSpec((1, nq, tm, V7X_LANES), lambda b, i: (b, 0, i, 0))
    return pl.pallas_call(
        _prep_seq_kernel,
        grid=(bsz, t // tm),
        in_specs=cur + tails + s0 + rowv + [full((1, B_W)), full((1, B_W)),
                                            full((LORA_W, B_W)), full((LORA_W, B_W))],
        out_specs=[out_spec] * 5,
        out_shape=[jax.ShapeDtypeStruct((bsz, nq, t, V7X_LANES), F32)] * 5,
        compiler_params=_cparams("arbitrary", "arbitrary"),
        name="rwkv_prep_seq",
    )(proj, proj, proj, proj, proj, proj, proj, proj, *shift0, *mu, w0, a0, w2p, a2p)


def _rwkv_prep_tok(proj, shift0, mu, w0, a0, w2p, a2p):
    _, m, _ = proj.shape
    cols = ((B_W, COL_R), (B_W, COL_K), (B_W, COL_V), (LORA_W, COL_LORA))
    cur = [pl.BlockSpec((1, m, w), functools.partial(lambda i, c: (0, 0, c), c=c)) for w, c in cols]
    s0 = [pl.BlockSpec((m, w), lambda i: (0, 0)) for w, _ in cols]
    rowv = [pl.BlockSpec((1, w), lambda i: (0, 0)) for w, _ in cols]
    full = lambda shape: pl.BlockSpec(shape, lambda i: (0,) * len(shape))
    out_spec = pl.BlockSpec((1, m, B_W), lambda i: (0, 0, 0))
    return pl.pallas_call(
        _prep_tok_kernel,
        grid=(1,),
        in_specs=cur + s0 + rowv + [full((1, B_W)), full((1, B_W)),
                                    full((LORA_W, B_W)), full((LORA_W, B_W))],
        out_specs=[out_spec] * 5,
        out_shape=[jax.ShapeDtypeStruct((1, m, B_W), F32)] * 5,
        compiler_params=_cparams("arbitrary"),
        name="rwkv_prep_tok",
    )(proj, proj, proj, proj, *shift0, *mu, w0, a0, w2p, a2p)


WKV_ROWS = 16


AHEAD = 2


def _wkv_kernel(xr, xw, xk, xv, xa, hr, hw, hk, hv, ha, s0_ref, kk_c, ka_c, rk_c, gg_c, gb_c,
                y_ref, s_ref, r_s, w_s, k_s, a_s, vn_s, vt_s, nkk_s, b_s, kp_s,
                p1_s, p2_s, p3_s, p4_s, p5_s, rks_s, cba_s, cka_s, cbr_s, ckr_s, ys_s,
                *, tt, nb):
    step = pl.program_id(0)
    ng = tt // V7X_SUBLANES
    pitch = tt + V7X_SUBLANES
    vpitch = HEAD_SIZE + V7X_SUBLANES

    @pl.when(step == 0)
    def _():
        s_ref[...] = s0_ref[...]

    dsts = (r_s, w_s, k_s, a_s, vn_s)

    def group_rows(rg):
        start = rg * V7X_SUBLANES
        if not isinstance(rg, int):
            start = pl.multiple_of(start, V7X_SUBLANES)
        return pl.ds(start, V7X_SUBLANES)

    def exchange(srcs, src_rg, dst_rg, q):
        src_rows = group_rows(src_rg)
        dst_rows = group_rows(dst_rg)
        for x, dst in zip(srcs, dsts):
            for b in range(nb):
                tile = x[b, q, src_rows, :]
                lanes = slice(b * LANE_GROUP, (b + 1) * LANE_GROUP)
                for g in range(nb):
                    dst[nb * q + g, dst_rows, lanes] = _move_lane_group(tile, g, b)[:, lanes]

    def v_by_token(rg, q):
        rows = group_rows(rg)
        for g in range(nb):
            n = nb * q + g
            vt_s[pl.ds(rg * V7X_SUBLANES * vpitch + n, V7X_SUBLANES, stride=vpitch), :] = vn_s[n, rows, :]

    def prep(m):
        ts = group_rows(m)
        sq = None
        for n in range(HEAD_SIZE):
            kk = k_s[n, ts, :] * kk_c[n]
            sq = kk * kk if sq is None else sq + kk * kk
        den = jnp.maximum(jnp.sqrt(sq), 1e-12)
        nxt_tok = lambda x: pltpu.roll(x, V7X_SUBLANES - 1, axis=0)
        acc = [None] * 5
        for n in range(HEAD_SIZE):
            k = k_s[n, ts, :]
            a = a_s[n, ts, :]
            r = r_s[n, ts, :]
            w = w_s[n, ts, :]
            kk = (k * kk_c[n]) / den
            nkk = -kk
            b = kk * a
            kp = k * (1.0 + (a - 1.0) * ka_c[n])
            nkk_n = nxt_tok(nkk)
            w_n = nxt_tok(w)
            nkk_s[n, ts, :] = nkk
            b_s[n, ts, :] = b
            kp_s[n, ts, :] = kp
            p1_s[n, ts, :] = w * r
            p2_s[n, ts, :] = w * w_n
            p3_s[n, ts, :] = b * w_n
            p4_s[n, ts, :] = kp * w_n
            p5_s[n, ts, :] = w * nkk_n
            kpr = kp * r
            terms = (b * nkk_n, kp * nkk_n, b * r, kpr, kpr * rk_c[n])
            acc = [x if y is None else y + x for x, y in zip(terms, acc)]
        for ref, val in zip((cba_s, cka_s, cbr_s, ckr_s, rks_s), acc):
            ref[ts, :] = val

    cur = (xr, xw, xk, xa, xv)
    nxt = (hr, hw, hk, ha, hv)

    @pl.when(step == 0)
    def _():
        for rg in range(AHEAD):
            for q in range(B_W // V7X_LANES):
                exchange(cur, rg, rg, q)
                v_by_token(rg, q)

    @pl.when(step > 0)
    def _():
        head = AHEAD * V7X_SUBLANES
        for dst in dsts:
            dst[:, 0:head, :] = dst[:, tt:tt + head, :]
        vt_s[0:head * vpitch, :] = vt_s[tt * vpitch:(tt + head) * vpitch, :]

    prep(0)

    parts = [h * WKV_ROWS for h in range(HEAD_SIZE // WKV_ROWS)]
    zeros = lambda: jnp.zeros((WKV_ROWS, V7X_LANES), F32)

    def bcast(ref, t, j):
        return jnp.broadcast_to(ref[j, pl.ds(t, 1), :], (WKV_ROWS, V7X_LANES))

    def lane_row(ref, t):
        return jnp.broadcast_to(ref[pl.ds(t, 1), :], (WKV_ROWS, V7X_LANES))

    def first_carry(i0):
        ua, xb = zeros(), zeros()
        for j in range(HEAD_SIZE):
            s = s_ref[j, i0:i0 + WKV_ROWS, :]
            ua = ua + s * bcast(nkk_s, 0, j)
            xb = xb + s * bcast(p5_s, 0, j)
        return ua, xb

    def group(m, carry, srcs, src_rg):
        if m + 1 < ng:
            prep(m + 1)
        pairs = V7X_SUBLANES // 2

        def pair(p, carry):
            ta = m * V7X_SUBLANES + 2 * p
            tb = ta + 1
            tc = jnp.minimum(ta + 2, tt - 2)
            out = []
            for i0, (ua, xb) in zip(parts, carry):
                va = vt_s[pl.ds(pl.multiple_of(ta * vpitch + i0, V7X_SUBLANES), WKV_ROWS), :]
                vb = vt_s[pl.ds(pl.multiple_of(tb * vpitch + i0, V7X_SUBLANES), WKV_ROWS), :]
                ub = (xb + ua * lane_row(cba_s, ta)) + va * lane_row(cka_s, ta)
                ya, yb, uc, xd = zeros(), zeros(), zeros(), zeros()
                for j in range(HEAD_SIZE):
                    s = s_ref[j, i0:i0 + WKV_ROWS, :]
                    ya = ya + s * bcast(p1_s, ta, j)
                    sn = ((((s * bcast(p2_s, ta, j) + ua * bcast(p3_s, ta, j)) + va * bcast(p4_s, ta, j))
                           + ub * bcast(b_s, tb, j)) + vb * bcast(kp_s, tb, j))
                    s_ref[j, i0:i0 + WKV_ROWS, :] = sn
                    yb = yb + sn * bcast(r_s, tb, j)
                    uc = uc + sn * bcast(nkk_s, tc, j)
                    xd = xd + sn * bcast(p5_s, tc, j)
                ya = (ya + ua * lane_row(cbr_s, ta)) + va * lane_row(ckr_s, ta)
                ys_s[pl.ds(i0 * pitch + ta, WKV_ROWS, stride=pitch), :] = ya
                ys_s[pl.ds(i0 * pitch + tb, WKV_ROWS, stride=pitch), :] = yb
                out.append((uc, xd))
            prev = m * pairs + p + pairs - 1
            prev_rg = lax.shift_right_logical(prev, 2) - 1 + AHEAD
            prev_q = 4 * (prev & (pairs - 1))
            for d in range(4):
                v_by_token(prev_rg, prev_q + d)
                exchange(srcs, src_rg, m + AHEAD, 4 * p + d)
            return tuple(out)

        return lax.fori_loop(0, pairs, pair, carry)

    carry = tuple(first_carry(i0) for i0 in parts)
    for m in range(ng):
        carry = (group(m, carry, cur, m + AHEAD) if m + AHEAD < ng
                 else group(m, carry, nxt, m + AHEAD - ng))
    for d in range(4):
        v_by_token(ng - 1 + AHEAD, B_W // V7X_LANES - 4 + d)

    def post(c, carry):
        t0 = pl.multiple_of(c * V7X_SUBLANES, V7X_SUBLANES)
        ys = [ys_s[pl.ds(i * pitch + t0, V7X_SUBLANES), :] for i in range(HEAD_SIZE)]
        mu = sum(ys[1:], ys[0]) * (1.0 / HEAD_SIZE)
        yc = [y - mu for y in ys]
        sq = [y * y for y in yc]
        rstd = lax.rsqrt(sum(sq[1:], sq[0]) * (1.0 / HEAD_SIZE) + GN_EPS)
        scale = rks_s[pl.ds(t0, V7X_SUBLANES), :]
        out = [yc[i] * rstd * gg_c[i:i + 1, :] + gb_c[i:i + 1, :]
               + scale * vn_s[i, pl.ds(t0, V7X_SUBLANES), :] for i in range(HEAD_SIZE)]
        for q in range(HEAD_SIZE // nb):
            nat = _swap_lane_groups(out[nb * q:nb * (q + 1)])
            for b in range(nb):
                y_ref[b, pl.ds(t0, V7X_SUBLANES), q * V7X_LANES:(q + 1) * V7X_LANES] = nat[b]
        return carry

    lax.fori_loop(0, ng, post, 0, unroll=2)


def _wkv(r, w, k, v, a, s0, consts, tt):
    nb, nq, t, lanes = r.shape
    n = HEAD_SIZE
    assert lanes == V7X_LANES and nb * B_HEADS == lanes and nq * lanes == B_W
    head = AHEAD * V7X_SUBLANES
    assert tt % head == 0 and tt // V7X_SUBLANES > AHEAD
    seq = pl.BlockSpec((nb, nq, tt, lanes), lambda i: (0, 0, i, 0))
    nxt = pl.BlockSpec((nb, nq, head, lanes),
                       lambda i: (0, 0, jnp.minimum((i + 1) * (tt // head), t // head - 1), 0))
    st = pl.BlockSpec((n, n, lanes), lambda i: (0, 0, 0))
    c3 = pl.BlockSpec((n, 1, lanes), lambda i: (0, 0, 0))
    c2 = pl.BlockSpec((n, lanes), lambda i: (0, 0))
    kk_c, ka_c, rk_c, gg_c, gb_c = consts
    col = lambda c: c.reshape(n, 1, lanes)
    pitch = tt + V7X_SUBLANES
    by_n = pltpu.VMEM((n, tt, lanes), F32)
    lead_n = pltpu.VMEM((n, tt + head, lanes), F32)
    return pl.pallas_call(
        functools.partial(_wkv_kernel, tt=tt, nb=nb),
        grid=(t // tt,),
        in_specs=[seq] * 5 + [nxt] * 5
        + [pl.BlockSpec((n, n, lanes), lambda i: (0, 0, 0), pipeline_mode=pl.Buffered(1)),
           c3, c3, c3, c2, c2],
        out_specs=[pl.BlockSpec((nb, tt, B_W), lambda i: (0, i, 0)), st],
        out_shape=[jax.ShapeDtypeStruct((nb, t, B_W), F32),
                   jax.ShapeDtypeStruct((n, n, lanes), F32)],
        scratch_shapes=[lead_n] * 5
        + [pltpu.VMEM(((tt + head) * (n + V7X_SUBLANES), lanes), F32)] + [by_n] * 8
        + [pltpu.VMEM((tt, lanes), F32)] * 5 + [pltpu.VMEM((n * pitch, lanes), F32)],
        compiler_params=_cparams("arbitrary"),
        name="wkv",
    )(r, w, k, v, a, r, w, k, v, a, s0, col(kk_c), col(ka_c), col(rk_c), gg_c, gb_c)


def _wkv_tok_kernel(r_ref, w_ref, k_ref, v_ref, a_ref, s0_ref, kk_c, ka_c, rk_c, gg_c, gb_c,
                    y_ref, s_ref, ys_s):
    for hd in range(r_ref.shape[0]):
        _wkv_tok_head(hd, r_ref, w_ref, k_ref, v_ref, a_ref, s0_ref, kk_c, ka_c, rk_c, gg_c, gb_c,
                      y_ref, s_ref, ys_s)


def _wkv_tok_head(hd, r_ref, w_ref, k_ref, v_ref, a_ref, s0_ref, kk_c, ka_c, rk_c, gg_c, gb_c,
                  y_ref, s_ref, ys_s):
    k = k_ref[hd]
    a = a_ref[hd]
    r = r_ref[hd]
    w = w_ref[hd]
    v = v_ref[hd]
    kk = k * kk_c[hd]
    nrm = jnp.sqrt(jnp.sum(kk * kk, axis=0, keepdims=True))
    kk = kk / jnp.maximum(nrm, 1e-12)
    nkk = -kk
    bvec = kk * a
    kp = k * (1.0 + (a - 1.0) * ka_c[hd])
    wr = w * r
    br = jnp.sum(bvec * r, axis=0, keepdims=True)
    kr = jnp.sum(kp * r, axis=0, keepdims=True)
    for i in range(HEAD_SIZE):
        s = s0_ref[hd, i]
        u = jnp.sum(s * nkk, axis=0, keepdims=True)
        yw = jnp.sum(s * wr, axis=0, keepdims=True)
        vi = v[i:i + 1, :]
        s_ref[hd, i] = (s * w + u * bvec) + vi * kp
        ys_s[i:i + 1, :] = (yw + u * br) + vi * kr
    y = ys_s[...]
    mu = jnp.mean(y, axis=0, keepdims=True)
    yc = y - mu
    var = jnp.mean(yc * yc, axis=0, keepdims=True)
    yn = yc * lax.rsqrt(var + GN_EPS) * gg_c[hd] + gb_c[hd]
    bonus = jnp.sum(r * kp * rk_c[hd], axis=0, keepdims=True) * v
    y_ref[hd] = yn + bonus


def _wkv_tok(r, w, k, v, a, s0, consts):
    nh, n, bsz = r.shape
    hb = 2
    vec = pl.BlockSpec((hb, n, bsz), lambda h: (h, 0, 0))
    st = pl.BlockSpec((hb, n, n, bsz), lambda h: (h, 0, 0, 0))
    return pl.pallas_call(
        _wkv_tok_kernel,
        grid=(nh // hb,),
        in_specs=[vec] * 5 + [st] + [vec] * 5,
        out_specs=[vec, st],
        out_shape=[jax.ShapeDtypeStruct((nh, n, bsz), F32),
                   jax.ShapeDtypeStruct(s0.shape, F32)],
        scratch_shapes=[pltpu.VMEM((n, bsz), F32)],
        compiler_params=_cparams("arbitrary"),
        name="wkv_tok",
    )(r, w, k, v, a, s0, *consts)


def _head_const(x, bsz):
    return jnp.tile(x.reshape(B_HEADS, HEAD_SIZE).T, (1, bsz))


def _gelu(x):
    return jax.nn.gelu(x)


def _layernorm(x, g, b):
    mu = jnp.mean(x, axis=-1, keepdims=True)
    xc = x - mu
    var = jnp.mean(xc * xc, axis=-1, keepdims=True)
    return xc * lax.rsqrt(var + NORM_EPS) * g + b


def _branch_a_seq_kernel(u_ref, v_ref, z_ref, g_ref, lg, lb, ws_ref, bias_ref, pa_ref, o_ref, *, tm):
    u = _gelu(u_ref[0])
    v = _layernorm(_gelu(v_ref[0]), lg[...], lb[...])
    vb = v.astype(BF16)
    row = lax.broadcasted_iota(jnp.int32, (CHUNK, CHUNK), 0)
    col = lax.broadcasted_iota(jnp.int32, (CHUNK, CHUNK), 1)
    rows = []
    for c in range(tm // CHUNK):
        blocks = []
        for g in range(A_GROUPS):
            wm = jnp.where(row >= col, ws_ref[g], 0.0).astype(BF16)
            blk = vb[c * CHUNK:(c + 1) * CHUNK, g * CHUNK:(g + 1) * CHUNK]
            blocks.append(jnp.dot(wm, blk, preferred_element_type=F32))
        rows.append(jnp.concatenate(blocks, axis=1) + bias_ref[...])
    mix = jnp.concatenate(rows, axis=0)
    ya = u * mix * jax.nn.silu(z_ref[0])
    o_ref[0] = jax.nn.sigmoid(g_ref[0]) * jnp.dot(ya.astype(BF16), pa_ref[...],
                                                  preferred_element_type=F32)


def _branch_a_tok_kernel(u_ref, v_ref, z_ref, g_ref, lg, lb, w00, b00, pa_ref, o_ref, vo_ref):
    u = _gelu(u_ref[0])
    v = _layernorm(_gelu(v_ref[0]), lg[...], lb[...])
    vo_ref[...] = v
    mix = v * w00[...] + b00[...]
    ya = u * mix * jax.nn.silu(z_ref[0])
    o_ref[0] = jax.nn.sigmoid(g_ref[0]) * jnp.dot(ya.astype(BF16), pa_ref[...],
                                                  preferred_element_type=F32)


def _branch_a_seq(proj, ln_g, ln_b, w_s, bias2d, p_a, tm=512):
    bsz, t, _ = proj.shape
    tm = min(tm, t)
    full = lambda shape: pl.BlockSpec(shape, lambda b, i: (0,) * len(shape))
    sec = lambda w, c: pl.BlockSpec((1, tm, w), lambda b, i: (b, i, c))
    return pl.pallas_call(
        functools.partial(_branch_a_seq_kernel, tm=tm),
        grid=(bsz, t // tm),
        in_specs=[sec(A_W, COL_UA), sec(A_W, COL_VA), sec(A_W, COL_ZA), sec(D_MODEL, COL_GA),
                  full((1, A_W)), full((1, A_W)), full((A_GROUPS, CHUNK, CHUNK)),
                  full((CHUNK, A_W)), full((A_W, D_MODEL))],
        out_specs=pl.BlockSpec((1, tm, D_MODEL), lambda b, i: (b, i, 0)),
        out_shape=jax.ShapeDtypeStruct((bsz, t, D_MODEL), F32),
        compiler_params=_cparams("arbitrary", "arbitrary"),
        name="branch_a_seq",
    )(proj, proj, proj, proj, ln_g, ln_b, w_s, bias2d, p_a)


def _branch_a_tok(proj, ln_g, ln_b, w00, b00, p_a):
    _, m, _ = proj.shape
    full = lambda shape: pl.BlockSpec(shape, lambda i: (0,) * len(shape))
    sec = lambda w, c: pl.BlockSpec((1, m, w), lambda i: (0, 0, c))
    return pl.pallas_call(
        _branch_a_tok_kernel,
        grid=(1,),
        in_specs=[sec(A_W, COL_UA), sec(A_W, COL_VA), sec(A_W, COL_ZA), sec(D_MODEL, COL_GA),
                  full((1, A_W)), full((1, A_W)), full((1, A_W)), full((1, A_W)),
                  full((A_W, D_MODEL))],
        out_specs=[pl.BlockSpec((1, m, D_MODEL), lambda i: (0, 0, 0)),
                   pl.BlockSpec((m, A_W), lambda i: (0, 0))],
        out_shape=[jax.ShapeDtypeStruct((1, m, D_MODEL), F32),
                   jax.ShapeDtypeStruct((m, A_W), F32)],
        compiler_params=_cparams("arbitrary"),
        name="branch_a_tok",
    )(proj, proj, proj, proj, ln_g, ln_b, w00, b00, p_a)


def _merge_kernel(yb_ref, zb_ref, gb_ref, oa_ref, x_ref, cg_ref, pb_ref, wo_ref, fg_ref, o_ref):
    yb = (yb_ref[0] * jax.nn.silu(zb_ref[0])).astype(BF16)
    merged = oa_ref[0] + jax.nn.sigmoid(gb_ref[0]) * jnp.dot(yb, pb_ref[...],
                                                            preferred_element_type=F32)
    out = x_ref[0] + cg_ref[0] * jnp.dot(merged.astype(BF16), wo_ref[...],
                                         preferred_element_type=F32)
    ms = jnp.mean(out * out, axis=-1, keepdims=True)
    o_ref[0] = out * lax.rsqrt(ms + NORM_EPS) * fg_ref[...]


def _merge(yb, proj, oa, x, c_gate, p_b, w_out, final_g, tm):
    bsz, t, d = x.shape
    tm = min(tm, t)
    tg = c_gate.shape[1]
    row = lambda: pl.BlockSpec((1, tm, d), lambda b, i: (b, i, 0))
    sec = lambda c: pl.BlockSpec((1, tm, d), lambda b, i: (b, i, c))
    gate = (pl.BlockSpec((1, 1, d), lambda b, i: (b, 0, 0)) if tg == 1
            else pl.BlockSpec((1, tm, d), lambda b, i: (b, i, 0)))
    wspec = lambda: pl.BlockSpec((d, d), lambda b, i: (0, 0), pipeline_mode=pl.Buffered(1))
    return pl.pallas_call(
        _merge_kernel,
        grid=(bsz, t // tm),
        in_specs=[row(), sec(COL_ZB), sec(COL_GB), row(), row(), gate, wspec(), wspec(),
                  pl.BlockSpec((1, d), lambda b, i: (0, 0))],
        out_specs=row(),
        out_shape=jax.ShapeDtypeStruct((bsz, t, d), F32),
        compiler_params=_cparams("arbitrary", "arbitrary"),
        name="merge",
    )(yb, proj, proj, oa, x, c_gate, p_b, w_out, final_g.reshape(1, d))


def _split(t, sizes):
    out, o = [], 0
    for s in sizes:
        out.append(t[..., o:o + s])
        o += s
    return out


def _permute_shift_cols(x):
    r, wd, k, v, ad = _split(x, (B_W, LORA, B_W, B_W, LORA))
    pad = jnp.zeros(x.shape[:-1] + (LORA_W - 2 * LORA,), x.dtype)
    return r, k, v, jnp.concatenate([wd, ad, pad], axis=-1)


def _nh_order(x, axis=-1):
    axis = axis % x.ndim
    shape = x.shape
    x = x.reshape(shape[:axis] + (B_HEADS, HEAD_SIZE) + shape[axis + 1:])
    return jnp.swapaxes(x, axis, axis + 1).reshape(shape)


def _hn_order(x, axis=-1):
    axis = axis % x.ndim
    shape = x.shape
    x = x.reshape(shape[:axis] + (HEAD_SIZE, B_HEADS) + shape[axis + 1:])
    return jnp.swapaxes(x, axis, axis + 1).reshape(shape)


def _shift_row(proj_row):
    r = _hn_order(proj_row[..., 0:B_W])
    k = _hn_order(proj_row[..., B_W:2 * B_W])
    v = _hn_order(proj_row[..., 2 * B_W:3 * B_W])
    lo = proj_row[..., COL_LORA * LORA_W:]
    return jnp.concatenate([r, lo[..., :LORA], k, v, lo[..., LORA:2 * LORA]], axis=-1)


def kernel(x_prompt, x_sample, c_prompt, c_sample, state_wkv, state_shift, norm_g, w_c, b_c, w_in, ln_v_g, ln_v_b, w_s, b_s, mu_shift, w0, w2, a0, a2, k_k, k_a, r_k, gn_g, gn_b, p_a, p_b, w_out, final_g):
    assert norm_g.shape[0] == 1, "single-layer trunk"
    bp, t, d = x_prompt.shape
    bs = x_sample.shape[0]

    w_rows = jnp.swapaxes(w_in[0], 0, 1)
    bounds = np.cumsum([0, A_W, A_W, A_W, B_W, LORA, B_W, B_W, LORA, B_W, D_MODEL, D_MODEL])
    ua, va, za, wr, wwd, wk, wv, wad, zb, ga, gb = (w_rows[a:b] for a, b in zip(bounds[:-1], bounds[1:]))
    wpad = jnp.zeros((LORA_W - 2 * LORA, d), F32)
    nh0 = lambda x: _nh_order(x, axis=0)
    w_p = jnp.concatenate([nh0(wr), nh0(wk), nh0(wv), nh0(zb), ga, gb, ua, va, za, wwd, wad, wpad],
                          axis=0).astype(BF16)

    def shift_cols(x):
        xr, xk, xv, xlo = _permute_shift_cols(x)
        return _nh_order(xr), _nh_order(xk), _nh_order(xv), xlo

    mu = [m.reshape(1, -1) for m in shift_cols(mu_shift[0])]
    zrow = jnp.zeros((LORA_W - LORA, B_W), F32)
    w2p = jnp.concatenate([_nh_order(w2[0]), zrow], axis=0).astype(BF16)
    a2p = jnp.concatenate([jnp.zeros((LORA, B_W), F32), _nh_order(a2[0]),
                           jnp.zeros((LORA_W - 2 * LORA, B_W), F32)], axis=0).astype(BF16)
    w0r, a0r = _nh_order(w0[0]).reshape(1, B_W), _nh_order(a0[0]).reshape(1, B_W)
    p_a_b, w_out_b = p_a[0].astype(BF16), w_out[0].astype(BF16)
    p_b_b = nh0(p_b[0]).astype(BF16)
    ln_g, ln_b = ln_v_g[0].reshape(1, A_W), ln_v_b[0].reshape(1, A_W)
    bias2d = jnp.repeat(b_s[0].T, CHUNK, axis=1)
    w00 = jnp.repeat(w_s[0][:, 0, 0], CHUNK).reshape(1, A_W)
    b00 = jnp.repeat(b_s[0][:, 0], CHUNK).reshape(1, A_W)

    m_rows = bp + bs
    m_pad = -m_rows % V7X_SUBLANES
    c_all = jnp.concatenate([c_prompt, c_sample, jnp.zeros((m_pad, d), F32)], axis=0)
    mod = _modulation(c_all, w_c[0], b_c[0])
    shift_p, scale_p, gate_p = (mod[:bp, i * d:(i + 1) * d].reshape(bp, 1, d) for i in range(3))
    shift_s, scale_s, gate_s = (mod[bp:m_rows, i * d:(i + 1) * d].reshape(1, bs, d) for i in range(3))

    def wkv_consts(bsz):
        return [_head_const(c, bsz) for c in (k_k[0], k_a[0], r_k[0], gn_g[0], gn_b[0])]

    xs = x_sample.reshape(1, bs, d)
    h_p = _norm_modulate(x_prompt, norm_g[0], scale_p, shift_p, tm=1024)
    h_s = _norm_modulate(xs, norm_g[0], scale_s, shift_s, tm=bs)
    proj_p, proj_s = _in_projection(h_p, h_s[0], w_p)
    proj_s = proj_s[None]

    zeros_shift = [jnp.zeros((bp, 1, w), F32) for w in (B_W, B_W, B_W, LORA_W)]
    feats = _rwkv_prep_seq(proj_p, zeros_shift, mu, w0r, a0r, w2p, a2p)
    s0_p = jnp.zeros((HEAD_SIZE, HEAD_SIZE, bp * B_HEADS), F32)
    yb_p, s_hl = _wkv(*feats, s0_p, wkv_consts(bp), tt=32)
    oa_p = _branch_a_seq(proj_p, ln_g, ln_b, w_s[0], bias2d, p_a_b)
    y_prompt = _merge(yb_p, proj_p, oa_p, x_prompt, gate_p, p_b_b, w_out_b, final_g, tm=256)
    wkv_prompt = jnp.transpose(s_hl.reshape(HEAD_SIZE, HEAD_SIZE, bp, B_HEADS), (2, 3, 1, 0))[None]
    shift_prompt = _shift_row(proj_p[:, t - 1, :])[None]

    feats = _rwkv_prep_tok(proj_s, shift_cols(state_shift[0]), mu, w0r, a0r, w2p, a2p)
    feats = [jnp.transpose(f.reshape(bs, HEAD_SIZE, B_HEADS), (2, 1, 0)) for f in feats]
    tok_consts = [jnp.broadcast_to(c.reshape(B_HEADS, HEAD_SIZE, 1), (B_HEADS, HEAD_SIZE, bs))
                  for c in (k_k[0], k_a[0], r_k[0], gn_g[0], gn_b[0])]
    yb_s, wkv_s = _wkv_tok(*feats, jnp.transpose(state_wkv[0], (1, 2, 3, 0)), tok_consts)
    yb_s = jnp.transpose(yb_s, (2, 1, 0)).reshape(1, bs, B_W)
    oa_s, v_s = _branch_a_tok(proj_s, ln_g, ln_b, w00, b00, p_a_b)
    y_s = _merge(yb_s, proj_s, oa_s, xs, gate_s, p_b_b, w_out_b, final_g, tm=bs)
    y_sample = y_s.reshape(bs, 1, d)
    wkv_sample = jnp.transpose(wkv_s, (3, 0, 1, 2))[None]
    shift_sample = _shift_row(proj_s[0])[None]
    chunk_v_sample = v_s.reshape(1, bs, 1, A_W)

    return (y_prompt, y_sample, wkv_prompt, shift_prompt, wkv_sample, shift_sample, chunk_v_sample)
```
